```python
import jax, jax.numpy as jnp
from jax import lax
import numpy as np

D_MODEL = 1024
BATCH = 8
SEQ = 2048
DEPTH = 1
DEC_BATCH = 128
DEC_SEQ = 4
PAST_LEN = 16384
PAGE_SIZE = 128

MIX_WIDTH = D_MODEL
GDN_HEAD_DIM = 128
GDN_WIDTH = MIX_WIDTH // 2
GDN_HEADS = GDN_WIDTH // GDN_HEAD_DIM
CONV_K = 4
CHUNK = 64
RWKV_HEAD_DIM = 64
RWKV_WIDTH = MIX_WIDTH - GDN_WIDTH
RWKV_HEADS = RWKV_WIDTH // RWKV_HEAD_DIM
DECAY_LORA = 64
AAA_LORA = 64
GATE_LORA = 128
RWKV_GN_EPS = RWKV_HEAD_DIM * 1e-5
GDN_QKV = 3 * GDN_WIDTH
OFF_Z = GDN_QKV
OFF_B = OFF_Z + GDN_WIDTH
OFF_A = OFF_B + GDN_HEADS
OFF_RWKV = OFF_A + GDN_HEADS
RWKV_COLS = 3 * RWKV_WIDTH + DECAY_LORA + AAA_LORA + GATE_LORA
IN_COLS = OFF_RWKV + RWKV_COLS
N_EXPERTS = 32
TOP_K = 4
D_EXPERT = D_MODEL
SWIGLU_LIMIT = 7.0
SWIGLU_ALPHA = 1.702
MOE_BLOCK = 128
ALPHA = (2.0 * DEPTH) ** 0.25
BETA_INIT = (8.0 * DEPTH) ** -0.25
LN_EPS = 1e-5
F32 = jnp.float32

kernel_name = "hymba_gdn_rwkv7_moe_deepnorm_step"


def _layernorm(x, g, b):
    xf = x.astype(F32)
    mu = jnp.mean(xf, -1, keepdims=True)
    var = jnp.mean(jnp.square(xf - mu), -1, keepdims=True)
    return ((xf - mu) * lax.rsqrt(var + LN_EPS) * g.astype(F32) + b.astype(F32)).astype(x.dtype)


def _l2norm(x, eps=1e-6):
    return x * lax.rsqrt(jnp.sum(x * x, -1, keepdims=True) + eps)


def _gated_delta_chunked(q, k, v, g, beta, s0):
    bsz, t, h, dk = q.shape
    dv = v.shape[-1]
    c = min(CHUNK, t)
    pad = (-t) % c
    n = (t + pad) // c

    def prep(a):
        a = jnp.pad(a.astype(F32), [(0, 0), (0, pad)] + [(0, 0)] * (a.ndim - 2))
        a = jnp.moveaxis(a, 2, 1)
        return a.reshape(a.shape[:2] + (n, c) + a.shape[3:])

    q = prep(q) * (dk ** -0.5)
    k = prep(k)
    v = prep(v)
    g = jnp.cumsum(prep(g), axis=-1)
    beta = prep(beta)
    lower = jnp.tril(jnp.ones((c, c), bool))
    strict = jnp.tril(jnp.ones((c, c), bool), -1)
    decay = jnp.exp(jnp.where(lower, g[..., :, None] - g[..., None, :], -jnp.inf))
    k_beta = k * beta[..., None]
    m = jnp.where(strict, jnp.einsum('bhnik,bhnjk->bhnij', k_beta, k) * decay, 0.0)
    eye = jnp.eye(c, dtype=F32)
    tinv = lax.linalg.triangular_solve(eye + m, jnp.broadcast_to(eye, m.shape),
                                       left_side=True, lower=True, unit_diagonal=True)
    u = jnp.einsum('bhnij,bhnjv->bhniv', tinv, v * beta[..., None])
    w = jnp.einsum('bhnij,bhnjk->bhnik', tinv, k_beta * jnp.exp(g)[..., None])
    attn = jnp.where(lower, jnp.einsum('bhnik,bhnjk->bhnij', q, k) * decay, 0.0)
    q_dec = q * jnp.exp(g)[..., None]
    k_dec = k * jnp.exp(g[..., -1:] - g)[..., None]
    g_last = jnp.exp(g[..., -1])
    xs = tuple(jnp.moveaxis(a, 2, 0) for a in (u, w, attn, q_dec, k_dec, g_last))

    def step(s, inp):
        u_i, w_i, a_i, qd_i, kd_i, gl_i = inp
        v_new = u_i - jnp.einsum('bhck,bhkv->bhcv', w_i, s)
        o_i = jnp.einsum('bhck,bhkv->bhcv', qd_i, s) + jnp.einsum('bhij,bhjv->bhiv', a_i, v_new)
        s = s * gl_i[..., None, None] + jnp.einsum('bhck,bhcv->bhkv', kd_i, v_new)
        return s, o_i

    s, o = lax.scan(step, s0.astype(F32), xs)
    o = jnp.moveaxis(o, 0, 2).reshape(bsz, h, n * c, dv)[:, :, :t]
    return jnp.moveaxis(o, 1, 2), s


def _rwkv7_scan(r, w, k, v, kk, a, s0):
    def step(s, inp):
        r_t, w_t, k_t, v_t, kk_t, a_t = inp
        s_kk = jnp.einsum('bhvk,bhk->bhv', s, kk_t)
        s = (s * w_t[:, :, None, :] - s_kk[..., None] * (kk_t * a_t)[:, :, None, :]
             + v_t[..., None] * k_t[:, :, None, :])
        return s, jnp.einsum('bhvk,bhk->bhv', s, r_t)
    xs = tuple(jnp.moveaxis(a_, 1, 0) for a_ in (r, w, k, v, kk, a))
    s, o = lax.scan(step, s0.astype(F32), xs)
    return jnp.moveaxis(o, 0, 1), s


def _moe(h2d, w_router, b_router, w_gate, b_gate, w_up, b_up, w_down, b_down):
    n_tok, d = h2d.shape
    logits = h2d.astype(F32) @ w_router.astype(F32) + b_router.astype(F32)
    top_val, top_idx = lax.top_k(logits, TOP_K)
    gates = jax.nn.softmax(top_val, axis=-1)
    n_assign = n_tok * TOP_K
    n_blocks = -(-n_assign // MOE_BLOCK) + N_EXPERTS
    n_rows = n_blocks * MOE_BLOCK
    flat_e = top_idx.reshape(-1)
    order = jnp.argsort(flat_e)
    sorted_e = flat_e[order]
    tok = (order // TOP_K).astype(jnp.int32)
    gate_sorted = gates.reshape(-1)[order]
    counts = jnp.bincount(flat_e, length=N_EXPERTS)
    starts = jnp.cumsum(counts) - counts
    padded = ((counts + MOE_BLOCK - 1) // MOE_BLOCK) * MOE_BLOCK
    pend = jnp.cumsum(padded)
    pstart = pend - padded
    dest = pstart[sorted_e] + (jnp.arange(n_assign) - starts[sorted_e])
    row_tok = jnp.zeros((n_rows,), jnp.int32).at[dest].set(tok)
    row_gate = jnp.zeros((n_rows,), F32).at[dest].set(gate_sorted)
    block_expert = jnp.minimum(
        jnp.searchsorted(pend, jnp.arange(n_blocks) * MOE_BLOCK, side='right'), N_EXPERTS - 1)
    xb = h2d[row_tok].reshape(n_blocks, MOE_BLOCK, d)

    def expert_block(args):
        x_blk, e = args
        gt = x_blk @ w_gate[e] + b_gate[e]
        up = x_blk @ w_up[e] + b_up[e]
        gt = jnp.minimum(gt, SWIGLU_LIMIT)
        up = jnp.clip(up, -SWIGLU_LIMIT, SWIGLU_LIMIT)
        hid = (up + 1.0) * gt * jax.nn.sigmoid(SWIGLU_ALPHA * gt)
        return hid @ w_down[e] + b_down[e]

    yb = lax.map(expert_block, (xb, block_expert)).reshape(n_rows, d)
    out = jnp.zeros((n_tok, d), F32).at[row_tok].add(yb.astype(F32) * row_gate[:, None])
    return out.astype(h2d.dtype)


def _hybrid_layer(x, conv_buf, shift_buf, s_gdn, s_rwkv, p):
    bsz, t, _ = x.shape
    dt = x.dtype
    proj = x @ p['w_in']
    seq = jnp.concatenate([conv_buf.astype(dt), proj[..., :GDN_QKV]], axis=1)
    new_conv = seq[:, -(CONV_K - 1):]
    seq32 = seq.astype(F32)
    cw = p['conv_w'].astype(F32)
    conv = seq32[:, 0:t] * cw[0]
    for j in range(1, CONV_K):
        conv = conv + seq32[:, j:j + t] * cw[j]
    conv = jax.nn.silu(conv)
    q = _l2norm(conv[..., :GDN_WIDTH].reshape(bsz, t, GDN_HEADS, GDN_HEAD_DIM))
    k = _l2norm(conv[..., GDN_WIDTH:2 * GDN_WIDTH].reshape(bsz, t, GDN_HEADS, GDN_HEAD_DIM))
    v = conv[..., 2 * GDN_WIDTH:].reshape(bsz, t, GDN_HEADS, GDN_HEAD_DIM)
    z = proj[..., OFF_Z:OFF_B].astype(F32).reshape(bsz, t, GDN_HEADS, GDN_HEAD_DIM)
    beta = jax.nn.sigmoid(proj[..., OFF_B:OFF_A].astype(F32))
    g = -jnp.exp(p['a_log'].astype(F32)) * jax.nn.softplus(
        proj[..., OFF_A:OFF_RWKV].astype(F32) + p['dt_bias'].astype(F32))
    o_gdn, new_gdn = _gated_delta_chunked(q, k, v, g, beta, s_gdn)
    o_gdn = (o_gdn * lax.rsqrt(jnp.mean(o_gdn * o_gdn, -1, keepdims=True) + 1e-6)
             * p['gdn_norm_w'].astype(F32) * jax.nn.silu(z)).reshape(bsz, t, GDN_WIDTH)
    rw_raw = proj[..., OFF_RWKV:]
    new_shift = rw_raw[:, -1]
    rw = rw_raw.astype(F32)
    prev = jnp.concatenate([shift_buf[:, None].astype(F32), rw[:, :-1]], axis=1)
    rs = rw + (prev - rw) * p['mu_shift'].astype(F32)
    o1 = RWKV_WIDTH
    o2 = 2 * RWKV_WIDTH
    o3 = 3 * RWKV_WIDTH
    o4 = o3 + DECAY_LORA
    o5 = o4 + AAA_LORA
    r, kr, vr = rs[..., :o1], rs[..., o1:o2], rs[..., o2:o3]
    wl, al, gl = rs[..., o3:o4], rs[..., o4:o5], rs[..., o5:]
    w_log = -jax.nn.softplus(-(p['w0'].astype(F32) + jnp.tanh(wl) @ p['w2'].astype(F32))) - 0.5
    decay = jnp.exp(-jnp.exp(w_log))
    a = jax.nn.sigmoid(p['a0'].astype(F32) + al @ p['a2'].astype(F32))
    gate = jax.nn.sigmoid(gl) @ p['g2'].astype(F32)
    heads = lambda a_: a_.reshape(bsz, t, RWKV_HEADS, RWKV_HEAD_DIM)
    kk = _l2norm(heads(kr * p['k_k'].astype(F32)))
    kr = kr * (1.0 + (a - 1.0) * p['k_a'].astype(F32))
    o_r, new_rwkv = _rwkv7_scan(heads(r), heads(decay), heads(kr), heads(vr), kk, heads(a), s_rwkv)
    mu = jnp.mean(o_r, -1, keepdims=True)
    var = jnp.mean(jnp.square(o_r - mu), -1, keepdims=True)
    o_r = ((o_r - mu) * lax.rsqrt(var + RWKV_GN_EPS)).reshape(bsz, t, RWKV_WIDTH)
    o_r = o_r * p['lnx_w'].astype(F32) + p['lnx_b'].astype(F32)
    bonus = jnp.sum(heads(r) * heads(kr) * p['r_k'].astype(F32), -1, keepdims=True) * heads(vr)
    o_rwkv = (o_r + bonus.reshape(bsz, t, RWKV_WIDTH)) * gate
    mix = jnp.concatenate([o_gdn, o_rwkv], axis=-1).astype(dt) @ p['w_o']
    h = _layernorm(ALPHA * x + mix, p['ln1_g'], p['ln1_b'])
    f = _moe(h.reshape(bsz * t, D_MODEL), p['w_router'], p['b_router'], p['w_gate'], p['b_gate'],
             p['w_up'], p['b_up'], p['w_down'], p['b_down']).reshape(bsz, t, D_MODEL)
    y = _layernorm(ALPHA * h + f, p['ln2_g'], p['ln2_b'])
    return y, new_conv, new_shift, new_gdn, new_rwkv


def setup_inputs(seed: int = 0) -> dict:
    key = jax.random.key(seed)
    ks = iter(jax.random.split(key, 48))
    nrm = lambda shape, scale: jax.random.normal(next(ks), shape, F32) * scale
    uni = lambda shape, lo, hi: jax.random.uniform(next(ks), shape, F32, lo, hi)
    x_prompt = nrm((BATCH, SEQ, D_MODEL), 1.0)
    x_sample = nrm((DEC_BATCH, DEC_SEQ, D_MODEL), 1.0)
    state_conv = nrm((DEC_BATCH, CONV_K - 1, GDN_QKV), 1.0)
    state_shift = nrm((DEC_BATCH, RWKV_COLS), 1.0)
    state_gdn = nrm((DEC_BATCH, GDN_HEADS, GDN_HEAD_DIM, GDN_HEAD_DIM), 0.3)
    state_rwkv = nrm((DEC_BATCH, RWKV_HEADS, RWKV_HEAD_DIM, RWKV_HEAD_DIM), 0.3)
    col_scale = (jnp.ones((IN_COLS,), F32)
                 .at[2 * GDN_WIDTH:3 * GDN_WIDTH].set(BETA_INIT)
                 .at[OFF_RWKV + 2 * RWKV_WIDTH:OFF_RWKV + 3 * RWKV_WIDTH].set(BETA_INIT))
    w_in = nrm((D_MODEL, IN_COLS), D_MODEL ** -0.5) * col_scale
    conv_w = nrm((CONV_K, GDN_QKV), CONV_K ** -0.5)
    a_log = jnp.log(uni((GDN_HEADS,), 1.0, 16.0))
    dt0 = jnp.exp(uni((GDN_HEADS,), float(np.log(1e-3)), float(np.log(1e-1))))
    dt_bias = dt0 + jnp.log(-jnp.expm1(-dt0))
    gdn_norm_w = 1.0 + nrm((GDN_HEAD_DIM,), 0.02)
    mu_shift = uni((RWKV_COLS,), 0.0, 1.0)
    w0 = uni((RWKV_WIDTH,), -6.5, -1.5)
    w2 = nrm((DECAY_LORA, RWKV_WIDTH), 0.1)
    a0 = nrm((RWKV_WIDTH,), 0.1)
    a2 = nrm((AAA_LORA, RWKV_WIDTH), 0.1)
    g2 = nrm((GATE_LORA, RWKV_WIDTH), GATE_LORA ** -0.5)
    k_k = 0.85 + nrm((RWKV_WIDTH,), 0.02)
    k_a = 1.0 + nrm((RWKV_WIDTH,), 0.02)
    r_k = nrm((RWKV_HEADS, RWKV_HEAD_DIM), 0.1)
    lnx_w = 1.0 + nrm((RWKV_WIDTH,), 0.02)
    lnx_b = nrm((RWKV_WIDTH,), 0.02)
    w_o = nrm((MIX_WIDTH, D_MODEL), MIX_WIDTH ** -0.5 * BETA_INIT)
    ln1_g = 1.0 + nrm((D_MODEL,), 0.02)
    ln1_b = nrm((D_MODEL,), 0.02)
    w_router = nrm((D_MODEL, N_EXPERTS), D_MODEL ** -0.5)
    b_router = nrm((N_EXPERTS,), 0.01)
    w_gate = nrm((N_EXPERTS, D_MODEL, D_EXPERT), D_MODEL ** -0.5 * BETA_INIT)
    b_gate = nrm((N_EXPERTS, D_EXPERT), 0.02)
    w_up = nrm((N_EXPERTS, D_MODEL, D_EXPERT), D_MODEL ** -0.5 * BETA_INIT)
    b_up = nrm((N_EXPERTS, D_EXPERT), 0.02)
    w_down = nrm((N_EXPERTS, D_EXPERT, D_MODEL), D_EXPERT ** -0.5 * BETA_INIT)
    b_down = nrm((N_EXPERTS, D_MODEL), 0.02)
    ln2_g = 1.0 + nrm((D_MODEL,), 0.02)
    ln2_b = nrm((D_MODEL,), 0.02)
    return {"x_prompt": x_prompt, "x_sample": x_sample, "state_conv": state_conv,
            "state_shift": state_shift, "state_gdn": state_gdn, "state_rwkv": state_rwkv,
            "w_in": w_in, "conv_w": conv_w, "a_log": a_log, "dt_bias": dt_bias,
            "gdn_norm_w": gdn_norm_w, "mu_shift": mu_shift, "w0": w0, "w2": w2, "a0": a0,
            "a2": a2, "g2": g2, "k_k": k_k, "k_a": k_a, "r_k": r_k, "lnx_w": lnx_w,
            "lnx_b": lnx_b, "w_o": w_o, "ln1_g": ln1_g, "ln1_b": ln1_b, "w_router": w_router,
            "b_router": b_router, "w_gate": w_gate, "b_gate": b_gate, "w_up": w_up,
            "b_up": b_up, "w_down": w_down, "b_down": b_down, "ln2_g": ln2_g, "ln2_b": ln2_b}


def reference(x_prompt, x_sample, state_conv, state_shift, state_gdn, state_rwkv,
              w_in, conv_w, a_log, dt_bias, gdn_norm_w, mu_shift, w0, w2, a0, a2, g2,
              k_k, k_a, r_k, lnx_w, lnx_b, w_o, ln1_g, ln1_b, w_router, b_router,
              w_gate, b_gate, w_up, b_up, w_down, b_down, ln2_g, ln2_b):
    p = dict(w_in=w_in, conv_w=conv_w, a_log=a_log, dt_bias=dt_bias, gdn_norm_w=gdn_norm_w,
             mu_shift=mu_shift, w0=w0, w2=w2, a0=a0, a2=a2, g2=g2, k_k=k_k, k_a=k_a, r_k=r_k,
             lnx_w=lnx_w, lnx_b=lnx_b, w_o=w_o, ln1_g=ln1_g, ln1_b=ln1_b, w_router=w_router,
             b_router=b_router, w_gate=w_gate, b_gate=b_gate, w_up=w_up, b_up=b_up,
             w_down=w_down, b_down=b_down, ln2_g=ln2_g, ln2_b=ln2_b)
    bp = x_prompt.shape[0]
    yp = x_prompt
    p_conv = jnp.zeros((bp, CONV_K - 1, GDN_QKV), x_prompt.dtype)
    p_shift = jnp.zeros((bp, RWKV_COLS), x_prompt.dtype)
    p_gdn = jnp.zeros((bp, GDN_HEADS, GDN_HEAD_DIM, GDN_HEAD_DIM), F32)
    p_rwkv = jnp.zeros((bp, RWKV_HEADS, RWKV_HEAD_DIM, RWKV_HEAD_DIM), F32)
    ys = x_sample
    s_conv, s_shift, s_gdn, s_rwkv = state_conv, state_shift, state_gdn, state_rwkv
    for _ in range(DEPTH):
        yp, p_conv, p_shift, p_gdn, p_rwkv = _hybrid_layer(yp, p_conv, p_shift, p_gdn, p_rwkv, p)
        ys, s_conv, s_shift, s_gdn, s_rwkv = _hybrid_layer(ys, s_conv, s_shift, s_gdn, s_rwkv, p)
    return (yp, ys, p_conv, p_shift, p_gdn, p_rwkv, s_conv, s_shift, s_gdn, s_rwkv)
```

```python
import functools
import math

import jax
import jax.numpy as jnp
from jax import lax
from jax.experimental import pallas as pl
from jax.experimental.pallas import tpu as pltpu

F32 = jnp.float32
BF16 = jnp.bfloat16

D_MODEL = 1024
GDN_HEADS = 4
GDN_HEAD_DIM = 128
GDN_WIDTH = GDN_HEADS * GDN_HEAD_DIM
GDN_QKV = 3 * GDN_WIDTH
CONV_K = 4
RWKV_HEADS = 8
RWKV_HEAD_DIM = 64
RWKV_WIDTH = RWKV_HEADS * RWKV_HEAD_DIM
DECAY_LORA = 64
AAA_LORA = 64
GATE_LORA = 128
RWKV_COLS = 3 * RWKV_WIDTH + DECAY_LORA + AAA_LORA + GATE_LORA
RWKV_GN_EPS = RWKV_HEAD_DIM * 1e-5
N_EXPERTS = 32
TOP_K = 4
SWIGLU_LIMIT = 7.0
SWIGLU_ALPHA = 1.702
DEPTH = 1
ALPHA = (2.0 * DEPTH) ** 0.25
LN_EPS = 1e-5

LANES = 128
SUBLANES = 8
VMEM_LIMIT_BYTES = 56 * 1024 * 1024

COL_Z = GDN_QKV
COL_RWKV = COL_Z + GDN_WIDTH
COL_BA = COL_RWKV + RWKV_COLS
PROJ_COLS = COL_BA + LANES

CHUNK = 64
GROUP = 4
PROJ_TILE = 256
TOK_TILE = 256
MOE_TILE = 256
DISPATCH_TILE = 256
COMBINE_TILE = 128


def _dot(a, b, dims, precise):
    if precise:
        return lax.dot_general(a, b, dims, precision=lax.Precision.HIGHEST, preferred_element_type=F32)
    return lax.dot_general(a.astype(BF16), b.astype(BF16), dims, preferred_element_type=F32)


def _mm(a, b, precise=False):
    return _dot(a, b, (((1,), (0,)), ((), ())), precise)


def _mm_nt(a, b, precise=False):
    return _dot(a, b, (((1,), (1,)), ((), ())), precise)


def _mm_tn(a, b, precise=False):
    return _mm(a.T, b, precise)


def _sigmoid(x):
    return 1.0 / (1.0 + jnp.exp(-x))


def _softplus(x):
    return jnp.maximum(x, 0.0) + jnp.log1p(jnp.exp(-jnp.abs(x)))


def _proj_kernel(x_ref, w_ref, o_ref):
    o_ref[...] = jnp.dot(x_ref[...].astype(BF16), w_ref[...], preferred_element_type=F32)


def _input_projection(x2d, w_in_bf16):
    n_tok = x2d.shape[0]
    assert n_tok % PROJ_TILE == 0
    return pl.pallas_call(
        _proj_kernel,
        grid=(n_tok // PROJ_TILE,),
        in_specs=[pl.BlockSpec((PROJ_TILE, D_MODEL), lambda i: (i, 0)),
                  pl.BlockSpec((D_MODEL, PROJ_COLS), lambda i: (0, 0))],
        out_specs=pl.BlockSpec((PROJ_TILE, PROJ_COLS), lambda i: (i, 0)),
        out_shape=jax.ShapeDtypeStruct((n_tok, PROJ_COLS), F32),
        compiler_params=pltpu.CompilerParams(dimension_semantics=("parallel",),
                                             vmem_limit_bytes=VMEM_LIMIT_BYTES),
        name="input_projection",
    )(x2d, w_in_bf16)


def _shift_rows(cur, prev, s, row_ids):
    return jnp.where(row_ids < s, pltpu.roll(prev, s, axis=0), pltpu.roll(cur, s, axis=0))


def _block_stack(x, n, width):
    grp = lax.broadcasted_iota(jnp.int32, x.shape, 1) // width
    return jnp.concatenate([jnp.where(grp == i, x, 0.0) for i in range(n)], axis=0)


def _inv_series(nil, c, n, precise):
    row = lax.broadcasted_iota(jnp.int32, nil.shape, 0)
    col = lax.broadcasted_iota(jnp.int32, nil.shape, 1) % c
    x = jnp.where(row == col, 1.0, 0.0) + nil
    q = nil
    for _ in range(int(math.log2(c)) - 1):
        q = _mm(q, _block_stack(q, n, c), precise)
        x = x + _mm(x, _block_stack(q, n, c), precise)
    return x


def _head_sum(x, width):
    pieces = []
    for j in range(x.shape[1] // LANES):
        xb = x[:, j * LANES:(j + 1) * LANES]
        if width == LANES:
            pieces.append(jnp.broadcast_to(jnp.sum(xb, axis=-1, keepdims=True), xb.shape))
        else:
            lo = lax.broadcasted_iota(jnp.int32, xb.shape, 1) < width
            s0 = jnp.sum(jnp.where(lo, xb, 0.0), axis=-1, keepdims=True)
            s1 = jnp.sum(jnp.where(lo, 0.0, xb), axis=-1, keepdims=True)
            pieces.append(jnp.where(lo, s0, s1))
    return jnp.concatenate(pieces, axis=1)


def _mixer_kernel(c, rows_in, t_last, n_chunks, precise,
                  proj_ref, cinit_ref, sinit_ref, sg0_ref, sr0_ref,
                  convw_ref, gvec_ref, mu_ref, rvec_ref, loraw_ref, g2_ref,
                  mix_ref, convo_ref, shifto_ref, sgo_ref, sro_ref,
                  prev_scr, sg_scr, sr_scr):
    ci = pl.program_id(1)
    gc4 = GROUP * c

    @pl.when(ci == 0)
    def _():
        prev_scr[...] = jnp.zeros(prev_scr.shape, F32)
        prev_scr[c - SUBLANES:c, 0:GDN_QKV] = cinit_ref[0]
        prev_scr[c - 1:c, COL_RWKV:COL_BA] = sinit_ref[0]
        sg_scr[...] = sg0_ref[0]
        sr_scr[...] = sr0_ref[0]

    x = proj_ref[0]
    if rows_in < c:
        x = jnp.concatenate([x, jnp.zeros((c - rows_in, PROJ_COLS), F32)], axis=0)
    prev = prev_scr[...]
    masked = t_last < c
    valid1 = lax.broadcasted_iota(jnp.int32, (c, LANES), 0) < t_last

    def vmask(a):
        if not masked:
            return a
        ok = lax.broadcasted_iota(jnp.int32, a.shape, 0) < t_last
        return jnp.where(ok, a, 0.0)

    row_p = lax.broadcasted_iota(jnp.int32, (c, gc4), 0)
    col_p = lax.broadcasted_iota(jnp.int32, (c, gc4), 1) % c
    grp_p = lax.broadcasted_iota(jnp.int32, (c, gc4), 1) // c
    lower_p = col_p <= row_p
    strict_p = col_p < row_p
    tri = jnp.where(lax.broadcasted_iota(jnp.int32, (c, c), 1) <= lax.broadcasted_iota(jnp.int32, (c, c), 0),
                    1.0, 0.0)

    xq = x[:, 0:GDN_QKV]
    pq = prev[:, 0:GDN_QKV]
    rid = lax.broadcasted_iota(jnp.int32, (c, GDN_QKV), 0)
    conv = _shift_rows(xq, pq, 3, rid) * convw_ref[0:1, :]
    conv = conv + _shift_rows(xq, pq, 2, rid) * convw_ref[1:2, :]
    conv = conv + _shift_rows(xq, pq, 1, rid) * convw_ref[2:3, :]
    conv = conv + xq * convw_ref[3:4, :]
    conv = conv * _sigmoid(conv)

    ba = x[:, COL_BA:COL_BA + LANES]
    beta_blk = _sigmoid(ba)
    g_blk = -jnp.exp(gvec_ref[0:1, :]) * _softplus(ba + gvec_ref[1:2, :])
    if masked:
        beta_blk = jnp.where(valid1, beta_blk, 0.0)
        g_blk = jnp.where(valid1, g_blk, 0.0)

    qn, kn, vv, kb, vb, beta_h = [], [], [], [], [], []
    for h in range(GDN_HEADS):
        sl = slice(h * GDN_HEAD_DIM, (h + 1) * GDN_HEAD_DIM)
        qh = conv[:, sl]
        kh = conv[:, GDN_WIDTH + h * GDN_HEAD_DIM:GDN_WIDTH + (h + 1) * GDN_HEAD_DIM]
        vh = conv[:, 2 * GDN_WIDTH + h * GDN_HEAD_DIM:2 * GDN_WIDTH + (h + 1) * GDN_HEAD_DIM]
        qh = qh * (lax.rsqrt(jnp.sum(qh * qh, axis=-1, keepdims=True) + 1e-6) * (GDN_HEAD_DIM ** -0.5))
        kh = vmask(kh * lax.rsqrt(jnp.sum(kh * kh, axis=-1, keepdims=True) + 1e-6))
        vh = vmask(vh)
        bh = beta_blk[:, h:h + 1]
        qn.append(qh), kn.append(kh), vv.append(vh), beta_h.append(bh)
        kb.append(kh * bh), vb.append(vh * bh)

    kn_all = jnp.concatenate(kn, axis=1)
    ma = _mm_nt(jnp.concatenate([jnp.concatenate(kb, axis=1), jnp.concatenate(qn, axis=1)], axis=0),
                _block_stack(kn_all, GDN_HEADS, GDN_HEAD_DIM), precise)
    gexp = jnp.zeros((c, gc4), F32)
    for h in range(GDN_HEADS):
        gexp = jnp.where(grp_p == h, g_blk[:, GDN_HEADS + h:GDN_HEADS + h + 1], gexp)
    gcol = _mm(tri, gexp, True)
    grow = jnp.sum(jnp.where(row_p == col_p, gcol, 0.0), axis=0, keepdims=True)
    decay = jnp.exp(jnp.where(lower_p, gcol - grow, -jnp.inf))
    m_p = jnp.where(strict_p, ma[0:c] * decay, 0.0)
    attn = ma[c:2 * c] * decay
    tinv = _inv_series(-m_p, c, GDN_HEADS, precise)

    egc, kdec, glast = [], [], []
    for h in range(GDN_HEADS):
        gch = gcol[:, h * c:h * c + 1]
        glh = gcol[c - 1:c, h * c:h * c + 1]
        egc.append(jnp.exp(gch))
        kdec.append(kn[h] * jnp.exp(glh - gch))
        glast.append(jnp.exp(glh))
    rhs2 = jnp.concatenate(
        [_block_stack(jnp.concatenate(vb, axis=1), GDN_HEADS, GDN_HEAD_DIM),
         _block_stack(jnp.concatenate([kb[h] * egc[h] for h in range(GDN_HEADS)], axis=1),
                      GDN_HEADS, GDN_HEAD_DIM)], axis=1)
    uw = _mm(tinv, rhs2, precise)
    vnew, o1 = [], []
    for h in range(GDN_HEADS):
        sl = slice(h * GDN_HEAD_DIM, (h + 1) * GDN_HEAD_DIM)
        s_h = sg_scr[h]
        ws = _mm(jnp.concatenate([uw[:, GDN_WIDTH + h * GDN_HEAD_DIM:GDN_WIDTH + (h + 1) * GDN_HEAD_DIM],
                                  qn[h] * egc[h]], axis=0), s_h, precise)
        vnew.append(uw[:, sl] - ws[0:c])
        o1.append(ws[c:2 * c])
    o2 = _mm(attn, _block_stack(jnp.concatenate(vnew, axis=1), GDN_HEADS, GDN_HEAD_DIM), precise)
    o_gdn = []
    for h in range(GDN_HEADS):
        sl = slice(h * GDN_HEAD_DIM, (h + 1) * GDN_HEAD_DIM)
        sg_scr[h] = sg_scr[h] * glast[h] + _mm_tn(kdec[h], vnew[h], precise)
        oh = o1[h] + o2[:, sl]
        zh = x[:, COL_Z + h * GDN_HEAD_DIM:COL_Z + (h + 1) * GDN_HEAD_DIM]
        oh = oh * lax.rsqrt(jnp.mean(oh * oh, axis=-1, keepdims=True) + 1e-6) * gvec_ref[2:3, :]
        o_gdn.append(oh * (zh * _sigmoid(zh)))

    rw = x[:, COL_RWKV:COL_BA]
    rid_r = lax.broadcasted_iota(jnp.int32, (c, RWKV_COLS), 0)
    prev_row = _shift_rows(rw, prev[:, COL_RWKV:COL_BA], 1, rid_r)
    rs = rw + (prev_row - rw) * mu_ref[...]
    o1_, o2_, o3_ = RWKV_WIDTH, 2 * RWKV_WIDTH, 3 * RWKV_WIDTH
    r = rs[:, 0:o1_]
    kr = rs[:, o1_:o2_]
    vr = rs[:, o2_:o3_]
    la = rs[:, o3_:o3_ + LANES]
    gl = rs[:, o3_ + LANES:o3_ + 2 * LANES]
    lane1 = lax.broadcasted_iota(jnp.int32, (c, LANES), 1)
    wa = _mm(jnp.where(lane1 < DECAY_LORA, jnp.tanh(la), la), loraw_ref[...])
    logw = -math.exp(-0.5) * _sigmoid(rvec_ref[0:1, :] + wa[:, 0:RWKV_WIDTH])
    a = _sigmoid(rvec_ref[1:2, :] + wa[:, RWKV_WIDTH:2 * RWKV_WIDTH])
    gate = _mm(_sigmoid(gl), g2_ref[...])
    kkr = kr * rvec_ref[2:3, :]
    kk = kkr * lax.rsqrt(_head_sum(kkr * kkr, RWKV_HEAD_DIM) + 1e-6)
    kr2 = kr * (1.0 + (a - 1.0) * rvec_ref[3:4, :])
    logw, kk, kr2 = vmask(logw), vmask(kk), vmask(kr2)
    gcum = _mm(tri, logw, True)
    e_pos = jnp.exp(gcum)
    e_neg = jnp.exp(-gcum)
    a_t = -kk * jnp.exp(gcum - logw)
    b_t = kk * a * e_neg
    k_t = kr2 * e_neg
    r_t = r * e_pos
    gw = GROUP * RWKV_HEAD_DIM
    bd_mask = (lax.broadcasted_iota(jnp.int32, (gw, gw), 0) // RWKV_HEAD_DIM
               == lax.broadcasted_iota(jnp.int32, (gw, gw), 1) // RWKV_HEAD_DIM)
    o_r = []
    for g in range(RWKV_HEADS // GROUP):
        sl = slice(g * gw, (g + 1) * gw)
        at_g, bt_g, kt_g, rt_g, v_g = a_t[:, sl], b_t[:, sl], k_t[:, sl], r_t[:, sl], vr[:, sl]
        aa = _mm_nt(jnp.concatenate([at_g, rt_g], axis=0),
                    jnp.concatenate([_block_stack(bt_g, GROUP, RWKV_HEAD_DIM),
                                     _block_stack(kt_g, GROUP, RWKV_HEAD_DIM)], axis=0), precise)
        a_ab = jnp.where(strict_p, aa[0:c, 0:gc4], 0.0)
        a_ak = jnp.where(strict_p, aa[0:c, gc4:2 * gc4], 0.0)
        a_rb = jnp.where(lower_p, aa[c:2 * c, 0:gc4], 0.0)
        a_rk = jnp.where(lower_p, aa[c:2 * c, gc4:2 * gc4], 0.0)
        tinv_r = _inv_series(a_ab, c, GROUP, precise)
        akv = _mm(a_ak, _block_stack(v_g, GROUP, RWKV_HEAD_DIM), precise)
        wu0 = _mm(tinv_r, jnp.concatenate([_block_stack(at_g, GROUP, RWKV_HEAD_DIM),
                                           _block_stack(akv, GROUP, RWKV_HEAD_DIM)], axis=1), precise)
        s_g = sr_scr[g]
        wr = _mm_nt(jnp.concatenate([wu0[:, 0:gw], rt_g], axis=0), s_g, precise)
        u = wr[0:c] + wu0[:, gw:2 * gw]
        o_g = wr[c:2 * c] + _mm(jnp.concatenate([a_rb, a_rk], axis=1),
                                jnp.concatenate([_block_stack(u, GROUP, RWKV_HEAD_DIM),
                                                 _block_stack(v_g, GROUP, RWKV_HEAD_DIM)], axis=0), precise)
        upd = _mm_tn(jnp.concatenate([u, v_g], axis=0), jnp.concatenate([bt_g, kt_g], axis=0), precise)
        sr_scr[g] = (s_g + jnp.where(bd_mask, upd, 0.0)) * e_pos[c - 1:c, sl]
        o_r.append(o_g)
    o_r = jnp.concatenate(o_r, axis=1)
    mean_o = _head_sum(o_r, RWKV_HEAD_DIM) * (1.0 / RWKV_HEAD_DIM)
    dev = o_r - mean_o
    var_o = _head_sum(dev * dev, RWKV_HEAD_DIM) * (1.0 / RWKV_HEAD_DIM)
    o_n = dev * lax.rsqrt(var_o + RWKV_GN_EPS) * rvec_ref[5:6, :] + rvec_ref[6:7, :]
    bonus = _head_sum(r * kr2 * rvec_ref[4:5, :], RWKV_HEAD_DIM) * vr
    o_rwkv = (o_n + bonus) * gate

    mix = jnp.concatenate(o_gdn + [o_rwkv], axis=1)
    mix_ref[0] = mix[0:rows_in].astype(mix_ref.dtype)
    prev_scr[...] = x[:, 0:COL_BA]

    @pl.when(ci == n_chunks - 1)
    def _():
        convo_ref[0] = x[t_last - (CONV_K - 1):t_last, 0:GDN_QKV]
        shifto_ref[0] = x[t_last - 1:t_last, COL_RWKV:COL_BA]
        sgo_ref[0] = sg_scr[...]
        sro_ref[0] = sr_scr[...]


def _mixer(proj3d, seq_len, conv_init8, shift_init, s_gdn, s_rwkv_bd, mp, precise):
    bsz, rows, _ = proj3d.shape
    c = CHUNK
    if rows >= c:
        assert rows == seq_len and seq_len % c == 0
        n_chunks, rows_in, t_last = seq_len // c, c, c
    else:
        n_chunks, rows_in, t_last = 1, rows, seq_len
    assert t_last >= CONV_K - 1
    gw = GROUP * RWKV_HEAD_DIM
    n_grp = RWKV_HEADS // GROUP
    const2 = lambda b, i: (0, 0)
    perb3 = lambda b, i: (b, 0, 0)
    perb4 = lambda b, i: (b, 0, 0, 0)
    kern = functools.partial(_mixer_kernel, c, rows_in, t_last, n_chunks, precise)
    return pl.pallas_call(
        kern,
        grid=(bsz, n_chunks),
        in_specs=[pl.BlockSpec((1, rows_in, PROJ_COLS), lambda b, i: (b, i, 0)),
                  pl.BlockSpec((1, SUBLANES, GDN_QKV), perb3),
                  pl.BlockSpec((1, 1, RWKV_COLS), perb3),
                  pl.BlockSpec((1, GDN_HEADS, GDN_HEAD_DIM, GDN_HEAD_DIM), perb4),
                  pl.BlockSpec((1, n_grp, gw, gw), perb4),
                  pl.BlockSpec((CONV_K, GDN_QKV), const2),
                  pl.BlockSpec((SUBLANES, LANES), const2),
                  pl.BlockSpec((1, RWKV_COLS), const2),
                  pl.BlockSpec((SUBLANES, RWKV_WIDTH), const2),
                  pl.BlockSpec((LANES, 2 * RWKV_WIDTH), const2),
                  pl.BlockSpec((GATE_LORA, RWKV_WIDTH), const2)],
        out_specs=[pl.BlockSpec((1, rows_in, D_MODEL), lambda b, i: (b, i, 0)),
                   pl.BlockSpec((1, CONV_K - 1, GDN_QKV), perb3),
                   pl.BlockSpec((1, 1, RWKV_COLS), perb3),
                   pl.BlockSpec((1, GDN_HEADS, GDN_HEAD_DIM, GDN_HEAD_DIM), perb4),
                   pl.BlockSpec((1, n_grp, gw, gw), perb4)],
        out_shape=[jax.ShapeDtypeStruct((bsz, rows, D_MODEL), BF16),
                   jax.ShapeDtypeStruct((bsz, CONV_K - 1, GDN_QKV), F32),
                   jax.ShapeDtypeStruct((bsz, 1, RWKV_COLS), F32),
                   jax.ShapeDtypeStruct((bsz, GDN_HEADS, GDN_HEAD_DIM, GDN_HEAD_DIM), F32),
                   jax.ShapeDtypeStruct((bsz, n_grp, gw, gw), F32)],
        scratch_shapes=[pltpu.VMEM((c, COL_BA), F32),
                        pltpu.VMEM((GDN_HEADS, GDN_HEAD_DIM, GDN_HEAD_DIM), F32),
                        pltpu.VMEM((n_grp, gw, gw), F32)],
        compiler_params=pltpu.CompilerParams(dimension_semantics=("parallel", "arbitrary"),
                                             vmem_limit_bytes=VMEM_LIMIT_BYTES),
        name="sequence_mixers",
    )(proj3d, conv_init8, shift_init, s_gdn, s_rwkv_bd,
      mp["conv_w"], mp["gvec"], mp["mu"], mp["rvec"], mp["lora_w"], mp["g2"])


def _rwkv_state_to_blockdiag(s):
    bsz = s.shape[0]
    n_grp = RWKV_HEADS // GROUP
    s5 = s.reshape(bsz, n_grp, GROUP, RWKV_HEAD_DIM, RWKV_HEAD_DIM)
    eye = jnp.eye(GROUP, dtype=s.dtype)
    bd = s5[:, :, :, :, None, :] * eye[None, None, :, None, :, None]
    return bd.reshape(bsz, n_grp, GROUP * RWKV_HEAD_DIM, GROUP * RWKV_HEAD_DIM)


def _rwkv_state_from_blockdiag(bd):
    bsz = bd.shape[0]
    n_grp = RWKV_HEADS // GROUP
    b6 = bd.reshape(bsz, n_grp, GROUP, RWKV_HEAD_DIM, GROUP, RWKV_HEAD_DIM)
    blocks = [b6[:, :, h, :, h, :] for h in range(GROUP)]
    return jnp.stack(blocks, axis=2).reshape(bsz, RWKV_HEADS, RWKV_HEAD_DIM, RWKV_HEAD_DIM)


def _layernorm(x, g, b):
    mu = jnp.mean(x, axis=-1, keepdims=True)
    d = x - mu
    var = jnp.mean(d * d, axis=-1, keepdims=True)
    return d * lax.rsqrt(var + LN_EPS) * g + b


def _post_mixer_kernel(mix_ref, x_ref, wo_ref, lnv_ref, wrt_ref, br_ref,
                       h_ref, idx_ref, rank_ref, gate_ref, cnt_ref, base_scr):
    i = pl.program_id(0)
    tt = mix_ref.shape[0]

    @pl.when(i == 0)
    def _():
        base_scr[...] = jnp.zeros(base_scr.shape, F32)

    hp = ALPHA * x_ref[...] + jnp.dot(mix_ref[...], wo_ref[...], preferred_element_type=F32)
    h = _layernorm(hp, lnv_ref[0:1, :], lnv_ref[1:2, :])
    h_ref[...] = h
    lt = lax.dot_general(wrt_ref[...], h, (((1,), (1,)), ((), ())),
                         precision=lax.Precision.HIGHEST, preferred_element_type=F32) + br_ref[...]
    eid = lax.broadcasted_iota(jnp.int32, lt.shape, 0)
    lt = jnp.where(eid < N_EXPERTS, lt, -jnp.inf)
    upper = jnp.where(lax.broadcasted_iota(jnp.int32, (tt, tt), 0) < lax.broadcasted_iota(jnp.int32, (tt, tt), 1),
                      1.0, 0.0).astype(BF16)
    base = base_scr[...]
    vals, idxs, ranks = [], [], []
    for _ in range(TOP_K):
        m = jnp.max(lt, axis=0, keepdims=True)
        sel = jnp.min(jnp.where(lt == m, eid, LANES), axis=0, keepdims=True)
        onehot = eid == sel
        lt = jnp.where(onehot, -jnp.inf, lt)
        oh = jnp.where(onehot, 1.0, 0.0)
        before = jnp.dot(oh.astype(BF16), upper, preferred_element_type=F32)
        ranks.append(jnp.sum(oh * (base + before), axis=0, keepdims=True))
        base = base + jnp.sum(oh, axis=1, keepdims=True)
        vals.append(m)
        idxs.append(sel)
    base_scr[...] = base
    ex = [jnp.exp(v - vals[0]) for v in vals]
    den = ex[0] + ex[1] + ex[2] + ex[3]
    pad_i = jnp.zeros((SUBLANES - TOP_K, tt), jnp.int32)
    idx_ref[...] = jnp.concatenate(idxs + [pad_i], axis=0)
    rank_ref[...] = jnp.concatenate([rk.astype(jnp.int32) for rk in ranks] + [pad_i], axis=0)
    gates = jnp.concatenate([e / den for e in ex] + [jnp.zeros((SUBLANES - TOP_K, tt), F32)], axis=0)
    gate_ref[...] = gates.T

    @pl.when(i == pl.num_programs(0) - 1)
    def _():
        cnt_ref[...] = base[:, 0:LANES].astype(jnp.int32)


def _post_mixer(mix2d, x2d, w_o_bf16, ln1, w_router_t, b_router_col):
    n_tok = x2d.shape[0]
    tt = TOK_TILE
    assert n_tok % tt == 0
    const2 = lambda i: (0, 0)
    return pl.pallas_call(
        _post_mixer_kernel,
        grid=(n_tok // tt,),
        in_specs=[pl.BlockSpec((tt, D_MODEL), lambda i: (i, 0)),
                  pl.BlockSpec((tt, D_MODEL), lambda i: (i, 0)),
                  pl.BlockSpec((D_MODEL, D_MODEL), const2),
                  pl.BlockSpec((SUBLANES, D_MODEL), const2),
                  pl.BlockSpec((LANES, D_MODEL), const2),
                  pl.BlockSpec((LANES, 1), const2)],
        out_specs=[pl.BlockSpec((tt, D_MODEL), lambda i: (i, 0)),
                   pl.BlockSpec((SUBLANES, tt), lambda i: (0, i)),
                   pl.BlockSpec((SUBLANES, tt), lambda i: (0, i)),
                   pl.BlockSpec((tt, SUBLANES), lambda i: (i, 0)),
                   pl.BlockSpec((LANES, LANES), const2)],
        out_shape=[jax.ShapeDtypeStruct((n_tok, D_MODEL), F32),
                   jax.ShapeDtypeStruct((SUBLANES, n_tok), jnp.int32),
                   jax.ShapeDtypeStruct((SUBLANES, n_tok), jnp.int32),
                   jax.ShapeDtypeStruct((n_tok, SUBLANES), F32),
                   jax.ShapeDtypeStruct((LANES, LANES), jnp.int32)],
        scratch_shapes=[pltpu.VMEM((LANES, tt), F32)],
        compiler_params=pltpu.CompilerParams(dimension_semantics=("arbitrary",),
                                             vmem_limit_bytes=VMEM_LIMIT_BYTES),
        name="outproj_norm_router",
    )(mix2d, x2d, w_o_bf16, ln1, w_router_t, b_router_col)


def _dispatch_kernel(pstart_ref, idx_ref, rank_ref, h_hbm, xb_in_hbm, xb_hbm, sem):
    del xb_in_hbm
    i = pl.program_id(0)
    tt = idx_ref.shape[1]

    def row_copy(t, k):
        dest = pstart_ref[idx_ref[k, t]] + rank_ref[k, t]
        return pltpu.make_async_copy(h_hbm.at[pl.ds(i * tt + t, 1), :], xb_hbm.at[pl.ds(dest, 1), :], sem)

    def issue(t, carry):
        for k in range(TOP_K):
            row_copy(t, k).start()
        return carry

    lax.fori_loop(0, tt, issue, 0)

    def drain(t, carry):
        for k in range(TOP_K):
            row_copy(t, k).wait()
        return carry

    lax.fori_loop(0, tt, drain, 0)


def _dispatch(h2d, idx, rank, pstart, n_rows):
    n_tok = h2d.shape[0]
    tt = DISPATCH_TILE
    assert n_tok % tt == 0
    xb0 = jnp.zeros((n_rows, D_MODEL), F32)
    grid_spec = pltpu.PrefetchScalarGridSpec(
        num_scalar_prefetch=1,
        grid=(n_tok // tt,),
        in_specs=[pl.BlockSpec((SUBLANES, tt), lambda i, ps: (0, i), memory_space=pltpu.SMEM),
                  pl.BlockSpec((SUBLANES, tt), lambda i, ps: (0, i), memory_space=pltpu.SMEM),
                  pl.BlockSpec(memory_space=pl.ANY),
                  pl.BlockSpec(memory_space=pl.ANY)],
        out_specs=pl.BlockSpec(memory_space=pl.ANY),
        scratch_shapes=[pltpu.SemaphoreType.DMA(())],
    )
    return pl.pallas_call(
        _dispatch_kernel,
        grid_spec=grid_spec,
        out_shape=jax.ShapeDtypeStruct((n_rows, D_MODEL), F32),
        input_output_aliases={4: 0},
        compiler_params=pltpu.CompilerParams(dimension_semantics=("arbitrary",)),
        name="moe_dispatch",
    )(pstart, idx, rank, h2d, xb0)


def _expert_kernel(be_ref, nused_ref, x_ref, wg_ref, wu_ref, wd_ref, bg_ref, bu_ref, bd_ref, y_ref,
                   wg16, wu16, wd16):
    i = pl.program_id(0)
    prev_e = be_ref[jnp.maximum(i - 1, 0)]
    fresh = jnp.logical_or(i == 0, be_ref[i] != prev_e)

    @pl.when(fresh)
    def _():
        wg16[...] = wg_ref[0].astype(BF16)
        wu16[...] = wu_ref[0].astype(BF16)
        wd16[...] = wd_ref[0].astype(BF16)

    @pl.when(i < nused_ref[0])
    def _():
        x16 = x_ref[...].astype(BF16)
        gt = jnp.dot(x16, wg16[...], preferred_element_type=F32) + bg_ref[0]
        up = jnp.dot(x16, wu16[...], preferred_element_type=F32) + bu_ref[0]
        gt = jnp.minimum(gt, SWIGLU_LIMIT)
        up = jnp.clip(up, -SWIGLU_LIMIT, SWIGLU_LIMIT)
        hid = (up + 1.0) * gt * _sigmoid(SWIGLU_ALPHA * gt)
        y_ref[...] = jnp.dot(hid.astype(BF16), wd16[...], preferred_element_type=F32) + bd_ref[0]

    @pl.when(i >= nused_ref[0])
    def _():
        y_ref[...] = jnp.zeros(y_ref.shape, F32)


def _experts(xb, block_expert, n_used, w_gate, b_gate, w_up, b_up, w_down, b_down):
    n_rows = xb.shape[0]
    tm = MOE_TILE
    n_blocks = n_rows // tm
    d_e = w_gate.shape[2]
    wspec = lambda shape: pl.BlockSpec(shape, lambda i, be, nu: (be[i], 0, 0))
    grid_spec = pltpu.PrefetchScalarGridSpec(
        num_scalar_prefetch=2,
        grid=(n_blocks,),
        in_specs=[pl.BlockSpec((tm, D_MODEL), lambda i, be, nu: (i, 0)),
                  wspec((1, D_MODEL, d_e)), wspec((1, D_MODEL, d_e)), wspec((1, d_e, D_MODEL)),
                  wspec((1, 1, d_e)), wspec((1, 1, d_e)), wspec((1, 1, D_MODEL))],
        out_specs=pl.BlockSpec((tm, D_MODEL), lambda i, be, nu: (i, 0)),
        scratch_shapes=[pltpu.VMEM((D_MODEL, d_e), BF16), pltpu.VMEM((D_MODEL, d_e), BF16),
                        pltpu.VMEM((d_e, D_MODEL), BF16)],
    )
    return pl.pallas_call(
        _expert_kernel,
        grid_spec=grid_spec,
        out_shape=jax.ShapeDtypeStruct((n_rows, D_MODEL), F32),
        compiler_params=pltpu.CompilerParams(dimension_semantics=("arbitrary",),
                                             vmem_limit_bytes=VMEM_LIMIT_BYTES),
        name="moe_experts",
    )(block_expert, n_used, xb, w_gate, w_up, w_down,
      b_gate[:, None, :], b_up[:, None, :], b_down[:, None, :])


def _combine_kernel(pstart_ref, idx_ref, rank_ref, h_ref, gate_ref, lnv_ref, yb_hbm, y_ref, buf, sem):
    tt = h_ref.shape[0]

    def row_copy(t, k):
        src = pstart_ref[idx_ref[k, t]] + rank_ref[k, t]
        return pltpu.make_async_copy(yb_hbm.at[pl.ds(src, 1), :], buf.at[k, pl.ds(t, 1), :], sem)

    def issue(t, carry):
        for k in range(TOP_K):
            row_copy(t, k).start()
        return carry

    lax.fori_loop(0, tt, issue, 0)

    def drain(t, carry):
        for k in range(TOP_K):
            row_copy(t, k).wait()
        return carry

    lax.fori_loop(0, tt, drain, 0)
    gates = gate_ref[...]
    f = buf[0] * gates[:, 0:1]
    for k in range(1, TOP_K):
        f = f + buf[k] * gates[:, k:k + 1]
    y_ref[...] = _layernorm(ALPHA * h_ref[...] + f, lnv_ref[0:1, :], lnv_ref[1:2, :])


def _combine(yb, h2d, idx, rank, gates, pstart, ln2):
    n_tok = h2d.shape[0]
    tt = COMBINE_TILE
    assert n_tok % tt == 0
    grid_spec = pltpu.PrefetchScalarGridSpec(
        num_scalar_prefetch=1,
        grid=(n_tok // tt,),
        in_specs=[pl.BlockSpec((SUBLANES, tt), lambda i, ps: (0, i), memory_space=pltpu.SMEM),
                  pl.BlockSpec((SUBLANES, tt), lambda i, ps: (0, i), memory_space=pltpu.SMEM),
                  pl.BlockSpec((tt, D_MODEL), lambda i, ps: (i, 0)),
                  pl.BlockSpec((tt, SUBLANES), lambda i, ps: (i, 0)),
                  pl.BlockSpec((SUBLANES, D_MODEL), lambda i, ps: (0, 0)),
                  pl.BlockSpec(memory_space=pl.ANY)],
        out_specs=pl.BlockSpec((tt, D_MODEL), lambda i, ps: (i, 0)),
        scratch_shapes=[pltpu.VMEM((TOP_K, tt, D_MODEL), F32), pltpu.SemaphoreType.DMA(())],
    )
    return pl.pallas_call(
        _combine_kernel,
        grid_spec=grid_spec,
        out_shape=jax.ShapeDtypeStruct((n_tok, D_MODEL), F32),
        compiler_params=pltpu.CompilerParams(dimension_semantics=("arbitrary",),
                                             vmem_limit_bytes=VMEM_LIMIT_BYTES),
        name="moe_combine_norm",
    )(pstart, idx, rank, h2d, gates, ln2, yb)


def _pad_rows(v, rows):
    return jnp.concatenate([v, jnp.zeros((rows - v.shape[0],) + v.shape[1:], v.dtype)], axis=0)


def _mixer_params(conv_w, a_log, dt_bias, gdn_norm_w, mu_shift, w0, w2, a0, a2, g2, k_k, k_a, r_k, lnx_w, lnx_b):
    gvec = jnp.zeros((SUBLANES, LANES), F32)
    gvec = gvec.at[0, GDN_HEADS:2 * GDN_HEADS].set(a_log).at[1, GDN_HEADS:2 * GDN_HEADS].set(dt_bias)
    gvec = gvec.at[2, :].set(gdn_norm_w)
    rvec = _pad_rows(jnp.stack([w0, a0, k_k, k_a, r_k.reshape(-1), lnx_w, lnx_b]), SUBLANES)
    lora_w = jnp.zeros((LANES, 2 * RWKV_WIDTH), F32)
    lora_w = lora_w.at[0:DECAY_LORA, 0:RWKV_WIDTH].set(w2).at[DECAY_LORA:, RWKV_WIDTH:].set(a2)
    return dict(conv_w=conv_w, gvec=gvec, mu=mu_shift[None, :], rvec=rvec,
                lora_w=lora_w.astype(BF16), g2=g2.astype(BF16))


def _layer(x_prompt, x_sample, state_conv, state_shift, state_gdn, state_rwkv,
           w_in, mixer_params, w_o, ln1_g, ln1_b, w_router, b_router,
           w_gate, b_gate, w_up, b_up, w_down, b_down, ln2_g, ln2_b, precise):
    bp, tp, d = x_prompt.shape
    bs, ts, _ = x_sample.shape
    n_p, n_s = bp * tp, bs * ts
    n_tok = n_p + n_s
    x2d = jnp.concatenate([x_prompt.reshape(n_p, d), x_sample.reshape(n_s, d)], axis=0)

    in_cols = w_in.shape[1]
    off_ba = COL_Z + GDN_WIDTH
    w_in_r = jnp.concatenate([w_in[:, :off_ba], w_in[:, off_ba + 2 * GDN_HEADS:],
                              w_in[:, off_ba:off_ba + 2 * GDN_HEADS],
                              jnp.zeros((d, PROJ_COLS - in_cols), w_in.dtype)], axis=1).astype(BF16)

    rows_s = -(-ts // SUBLANES) * SUBLANES
    proj_p = _input_projection(x_prompt.reshape(n_p, d), w_in_r).reshape(bp, tp, PROJ_COLS)
    proj_s = _input_projection(x_sample.reshape(n_s, d), w_in_r).reshape(bs, ts, PROJ_COLS)
    if rows_s != ts:
        proj_s = jnp.pad(proj_s, ((0, 0), (0, rows_s - ts), (0, 0)))
    gw = GROUP * RWKV_HEAD_DIM
    zeros_p = (jnp.zeros((bp, SUBLANES, GDN_QKV), F32), jnp.zeros((bp, 1, RWKV_COLS), F32),
               jnp.zeros((bp, GDN_HEADS, GDN_HEAD_DIM, GDN_HEAD_DIM), F32),
               jnp.zeros((bp, RWKV_HEADS // GROUP, gw, gw), F32))
    mix_p, conv_p, shift_p, gdn_p, rwkv_p = _mixer(proj_p, tp, *zeros_p, mixer_params, precise)
    conv8 = jnp.pad(state_conv, ((0, 0), (SUBLANES - (CONV_K - 1), 0), (0, 0)))
    mix_s, conv_s, shift_s, gdn_s, rwkv_s = _mixer(
        proj_s, ts, conv8, state_shift[:, None, :], state_gdn, _rwkv_state_to_blockdiag(state_rwkv),
        mixer_params, precise)
    mix2d = jnp.concatenate([mix_p.reshape(n_p, d), mix_s[:, :ts].reshape(n_s, d)], axis=0)

    ln1 = _pad_rows(jnp.stack([ln1_g, ln1_b]), SUBLANES)
    ln2 = _pad_rows(jnp.stack([ln2_g, ln2_b]), SUBLANES)
    wrt = _pad_rows(w_router.T, LANES)
    brc = _pad_rows(b_router[:, None], LANES)
    h2d, idx, rank, gates, cnt = _post_mixer(mix2d, x2d, w_o.astype(BF16), ln1, wrt, brc)

    counts = cnt[:N_EXPERTS, 0]
    padded = ((counts + MOE_TILE - 1) // MOE_TILE) * MOE_TILE
    pend = jnp.cumsum(padded)
    pstart = (pend - padded).astype(jnp.int32)
    n_blocks = -(-(n_tok * TOP_K) // MOE_TILE) + N_EXPERTS
    n_rows = n_blocks * MOE_TILE
    block_expert = jnp.minimum(
        jnp.sum(pend[None, :] <= (jnp.arange(n_blocks) * MOE_TILE)[:, None], axis=1), N_EXPERTS - 1).astype(jnp.int32)
    n_used = (pend[-1:] // MOE_TILE).astype(jnp.int32)

    xb = _dispatch(h2d, idx, rank, pstart, n_rows)
    yb = _experts(xb, block_expert, n_used, w_gate, b_gate, w_up, b_up, w_down, b_down)
    y2d = _combine(yb, h2d, idx, rank, gates, pstart, ln2)

    y_p = y2d[:n_p].reshape(bp, tp, d)
    y_s = y2d[n_p:].reshape(bs, ts, d)
    return (y_p, y_s, conv_p, shift_p[:, 0], gdn_p, _rwkv_state_from_blockdiag(rwkv_p),
            conv_s, shift_s[:, 0], gdn_s, _rwkv_state_from_blockdiag(rwkv_s))


def kernel(x_prompt, x_sample, state_conv, state_shift, state_gdn, state_rwkv, w_in, conv_w, a_log, dt_bias,
           gdn_norm_w, mu_shift, w0, w2, a0, a2, g2, k_k, k_a, r_k, lnx_w, lnx_b, w_o, ln1_g, ln1_b,
           w_router, b_router, w_gate, b_gate, w_up, b_up, w_down, b_down, ln2_g, ln2_b):
    mp = _mixer_params(conv_w, a_log, dt_bias, gdn_norm_w, mu_shift, w0, w2, a0, a2, g2, k_k, k_a, r_k,
                       lnx_w, lnx_b)
    return _layer(x_prompt, x_sample, state_conv, state_shift, state_gdn, state_rwkv,
                  w_in, mp, w_o, ln1_g, ln1_b, w_router, b_router,
                  w_gate, b_gate, w_up, b_up, w_down, b_down, ln2_g, ln2_b, precise=True)
```

```python
import functools
import math

import jax
import jax.numpy as jnp
from jax import lax
from jax.experimental import pallas as pl
from jax.experimental.pallas import tpu as pltpu

F32 = jnp.float32
BF16 = jnp.bfloat16

D_MODEL = 1024
GDN_HEADS = 4
GDN_HEAD_DIM = 128
GDN_WIDTH = GDN_HEADS * GDN_HEAD_DIM
GDN_QKV = 3 * GDN_WIDTH
CONV_K = 4
RWKV_HEADS = 8
RWKV_HEAD_DIM = 64
RWKV_WIDTH = RWKV_HEADS * RWKV_HEAD_DIM
DECAY_LORA = 64
AAA_LORA = 64
GATE_LORA = 128
RWKV_COLS = 3 * RWKV_WIDTH + DECAY_LORA + AAA_LORA + GATE_LORA
RWKV_GN_EPS = RWKV_HEAD_DIM * 1e-5
N_EXPERTS = 32
TOP_K = 4
SWIGLU_LIMIT = 7.0
SWIGLU_ALPHA = 1.702
DEPTH = 1
ALPHA = (2.0 * DEPTH) ** 0.25
LN_EPS = 1e-5

LANES = 128
SUBLANES = 8
VMEM_LIMIT_BYTES = 56 * 1024 * 1024

COL_Z = GDN_QKV
COL_RWKV = COL_Z + GDN_WIDTH
COL_BA = COL_RWKV + RWKV_COLS
PROJ_COLS = COL_BA + LANES

CHUNK = 64
GROUP = 4
PROJ_TILE = 256
TOK_TILE = 256
MOE_TILE = 256
DISPATCH_TILE = 256
COMBINE_TILE = 128


def _dot(a, b, dims, precise):
    if precise:
        return lax.dot_general(a, b, dims, precision=lax.Precision.HIGHEST, preferred_element_type=F32)
    return lax.dot_general(a.astype(BF16), b.astype(BF16), dims, preferred_element_type=F32)


def _mm(a, b, precise=False):
    return _dot(a, b, (((1,), (0,)), ((), ())), precise)


def _mm_nt(a, b, precise=False):
    return _dot(a, b, (((1,), (1,)), ((), ())), precise)


def _mm_tn(a, b, precise=False):
    return _mm(a.T, b, precise)


def _sigmoid(x):
    return 1.0 / (1.0 + jnp.exp(-x))


def _softplus(x):
    return jnp.maximum(x, 0.0) + jnp.log1p(jnp.exp(-jnp.abs(x)))


def _proj_kernel(x_ref, w_ref, o_ref):
    o_ref[...] = jnp.dot(x_ref[...].astype(BF16), w_ref[...], preferred_element_type=F32)


def _input_projection(x2d, w_in_bf16):
    n_tok = x2d.shape[0]
    assert n_tok % PROJ_TILE == 0
    return pl.pallas_call(
        _proj_kernel,
        grid=(n_tok // PROJ_TILE,),
        in_specs=[pl.BlockSpec((PROJ_TILE, D_MODEL), lambda i: (i, 0)),
                  pl.BlockSpec((D_MODEL, PROJ_COLS), lambda i: (0, 0))],
        out_specs=pl.BlockSpec((PROJ_TILE, PROJ_COLS), lambda i: (i, 0)),
        out_shape=jax.ShapeDtypeStruct((n_tok, PROJ_COLS), F32),
        compiler_params=pltpu.CompilerParams(dimension_semantics=("parallel",),
                                             vmem_limit_bytes=VMEM_LIMIT_BYTES),
        name="input_projection",
    )(x2d, w_in_bf16)


def _shift_rows(cur, prev, s, row_ids):
    return jnp.where(row_ids < s, pltpu.roll(prev, s, axis=0), pltpu.roll(cur, s, axis=0))


def _block_stack(x, n, width):
    grp = lax.broadcasted_iota(jnp.int32, x.shape, 1) // width
    return jnp.concatenate([jnp.where(grp == i, x, 0.0) for i in range(n)], axis=0)


def _inv_series(nil, c, n, precise):
    row = lax.broadcasted_iota(jnp.int32, nil.shape, 0)
    col = lax.broadcasted_iota(jnp.int32, nil.shape, 1) % c
    x = jnp.where(row == col, 1.0, 0.0) + nil
    q = nil
    for _ in range(int(math.log2(c)) - 1):
        q = _mm(q, _block_stack(q, n, c), precise)
        x = x + _mm(x, _block_stack(q, n, c), precise)
    return x


def _head_sum(x, width):
    pieces = []
    for j in range(x.shape[1] // LANES):
        xb = x[:, j * LANES:(j + 1) * LANES]
        if width == LANES:
            pieces.append(jnp.broadcast_to(jnp.sum(xb, axis=-1, keepdims=True), xb.shape))
        else:
            lo = lax.broadcasted_iota(jnp.int32, xb.shape, 1) < width
            s0 = jnp.sum(jnp.where(lo, xb, 0.0), axis=-1, keepdims=True)
            s1 = jnp.sum(jnp.where(lo, 0.0, xb), axis=-1, keepdims=True)
            pieces.append(jnp.where(lo, s0, s1))
    return jnp.concatenate(pieces, axis=1)


def _mixer_kernel(c, rows_in, t_last, n_chunks, precise,
                  proj_ref, cinit_ref, sinit_ref, sg0_ref, sr0_ref,
                  convw_ref, gvec_ref, mu_ref, rvec_ref, loraw_ref, g2_ref,
                  mix_ref, convo_ref, shifto_ref, sgo_ref, sro_ref,
                  prev_scr, sg_scr, sr_scr):
    ci = pl.program_id(1)
    gc4 = GROUP * c

    @pl.when(ci == 0)
    def _():
        prev_scr[...] = jnp.zeros(prev_scr.shape, F32)
        prev_scr[c - SUBLANES:c, 0:GDN_QKV] = cinit_ref[0]
        prev_scr[c - 1:c, COL_RWKV:COL_BA] = sinit_ref[0]
        sg_scr[...] = sg0_ref[0]
        sr_scr[...] = sr0_ref[0]

    x = proj_ref[0]
    if rows_in < c:
        x = jnp.concatenate([x, jnp.zeros((c - rows_in, PROJ_COLS), F32)], axis=0)
    prev = prev_scr[...]
    masked = t_last < c
    valid1 = lax.broadcasted_iota(jnp.int32, (c, LANES), 0) < t_last

    def vmask(a):
        if not masked:
            return a
        ok = lax.broadcasted_iota(jnp.int32, a.shape, 0) < t_last
        return jnp.where(ok, a, 0.0)

    row_p = lax.broadcasted_iota(jnp.int32, (c, gc4), 0)
    col_p = lax.broadcasted_iota(jnp.int32, (c, gc4), 1) % c
    grp_p = lax.broadcasted_iota(jnp.int32, (c, gc4), 1) // c
    lower_p = col_p <= row_p
    strict_p = col_p < row_p
    tri = jnp.where(lax.broadcasted_iota(jnp.int32, (c, c), 1) <= lax.broadcasted_iota(jnp.int32, (c, c), 0),
                    1.0, 0.0)

    xq = x[:, 0:GDN_QKV]
    pq = prev[:, 0:GDN_QKV]
    rid = lax.broadcasted_iota(jnp.int32, (c, GDN_QKV), 0)
    conv = _shift_rows(xq, pq, 3, rid) * convw_ref[0:1, :]
    conv = conv + _shift_rows(xq, pq, 2, rid) * convw_ref[1:2, :]
    conv = conv + _shift_rows(xq, pq, 1, rid) * convw_ref[2:3, :]
    conv = conv + xq * convw_ref[3:4, :]
    conv = conv * _sigmoid(conv)

    ba = x[:, COL_BA:COL_BA + LANES]
    beta_blk = _sigmoid(ba)
    g_blk = -jnp.exp(gvec_ref[0:1, :]) * _softplus(ba + gvec_ref[1:2, :])
    if masked:
        beta_blk = jnp.where(valid1, beta_blk, 0.0)
        g_blk = jnp.where(valid1, g_blk, 0.0)

    qn, kn, vv, kb, vb, beta_h = [], [], [], [], [], []
    for h in range(GDN_HEADS):
        sl = slice(h * GDN_HEAD_DIM, (h + 1) * GDN_HEAD_DIM)
        qh = conv[:, sl]
        kh = conv[:, GDN_WIDTH + h * GDN_HEAD_DIM:GDN_WIDTH + (h + 1) * GDN_HEAD_DIM]
        vh = conv[:, 2 * GDN_WIDTH + h * GDN_HEAD_DIM:2 * GDN_WIDTH + (h + 1) * GDN_HEAD_DIM]
        qh = qh * (lax.rsqrt(jnp.sum(qh * qh, axis=-1, keepdims=True) + 1e-6) * (GDN_HEAD_DIM ** -0.5))
        kh = vmask(kh * lax.rsqrt(jnp.sum(kh * kh, axis=-1, keepdims=True) + 1e-6))
        vh = vmask(vh)
        bh = beta_blk[:, h:h + 1]
        qn.append(qh), kn.append(kh), vv.append(vh), beta_h.append(bh)
        kb.append(kh * bh), vb.append(vh * bh)

    kn_all = jnp.concatenate(kn, axis=1)
    ma = _mm_nt(jnp.concatenate([jnp.concatenate(kb, axis=1), jnp.concatenate(qn, axis=1)], axis=0),
                _block_stack(kn_all, GDN_HEADS, GDN_HEAD_DIM), precise)
    gexp = jnp.zeros((c, gc4), F32)
    for h in range(GDN_HEADS):
        gexp = jnp.where(grp_p == h, g_blk[:, GDN_HEADS + h:GDN_HEADS + h + 1], gexp)
    gcol = _mm(tri, gexp, True)
    grow = jnp.sum(jnp.where(row_p == col_p, gcol, 0.0), axis=0, keepdims=True)
    decay = jnp.exp(jnp.where(lower_p, gcol - grow, -jnp.inf))
    m_p = jnp.where(strict_p, ma[0:c] * decay, 0.0)
    attn = ma[c:2 * c] * decay
    tinv = _inv_series(-m_p, c, GDN_HEADS, precise)

    egc, kdec, glast = [], [], []
    for h in range(GDN_HEADS):
        gch = gcol[:, h * c:h * c + 1]
        glh = gcol[c - 1:c, h * c:h * c + 1]
        egc.append(jnp.exp(gch))
        kdec.append(kn[h] * jnp.exp(glh - gch))
        glast.append(jnp.exp(glh))
    rhs2 = jnp.concatenate(
        [_block_stack(jnp.concatenate(vb, axis=1), GDN_HEADS, GDN_HEAD_DIM),
         _block_stack(jnp.concatenate([kb[h] * egc[h] for h in range(GDN_HEADS)], axis=1),
                      GDN_HEADS, GDN_HEAD_DIM)], axis=1)
    uw = _mm(tinv, rhs2, precise)
    vnew, o1 = [], []
    for h in range(GDN_HEADS):
        sl = slice(h * GDN_HEAD_DIM, (h + 1) * GDN_HEAD_DIM)
        s_h = sg_scr[h]
        ws = _mm(jnp.concatenate([uw[:, GDN_WIDTH + h * GDN_HEAD_DIM:GDN_WIDTH + (h + 1) * GDN_HEAD_DIM],
                                  qn[h] * egc[h]], axis=0), s_h, precise)
        vnew.append(uw[:, sl] - ws[0:c])
        o1.append(ws[c:2 * c])
    o2 = _mm(attn, _block_stack(jnp.concatenate(vnew, axis=1), GDN_HEADS, GDN_HEAD_DIM), precise)
    o_gdn = []
    for h in range(GDN_HEADS):
        sl = slice(h * GDN_HEAD_DIM, (h + 1) * GDN_HEAD_DIM)
        sg_scr[h] = sg_scr[h] * glast[h] + _mm_tn(kdec[h], vnew[h], precise)
        oh = o1[h] + o2[:, sl]
        zh = x[:, COL_Z + h * GDN_HEAD_DIM:COL_Z + (h + 1) * GDN_HEAD_DIM]
        oh = oh * lax.rsqrt(jnp.mean(oh * oh, axis=-1, keepdims=True) + 1e-6) * gvec_ref[2:3, :]
        o_gdn.append(oh * (zh * _sigmoid(zh)))

    rw = x[:, COL_RWKV:COL_BA]
    rid_r = lax.broadcasted_iota(jnp.int32, (c, RWKV_COLS), 0)
    prev_row = _shift_rows(rw, prev[:, COL_RWKV:COL_BA], 1, rid_r)
    rs = rw + (prev_row - rw) * mu_ref[...]
    o1_, o2_, o3_ = RWKV_WIDTH, 2 * RWKV_WIDTH, 3 * RWKV_WIDTH
    r = rs[:, 0:o1_]
    kr = rs[:, o1_:o2_]
    vr = rs[:, o2_:o3_]
    la = rs[:, o3_:o3_ + LANES]
    gl = rs[:, o3_ + LANES:o3_ + 2 * LANES]
    lane1 = lax.broadcasted_iota(jnp.int32, (c, LANES), 1)
    wa = _mm(jnp.where(lane1 < DECAY_LORA, jnp.tanh(la), la), loraw_ref[...])
    logw = -math.exp(-0.5) * _sigmoid(rvec_ref[0:1, :] + wa[:, 0:RWKV_WIDTH])
    a = _sigmoid(rvec_ref[1:2, :] + wa[:, RWKV_WIDTH:2 * RWKV_WIDTH])
    gate = _mm(_sigmoid(gl), g2_ref[...])
    kkr = kr * rvec_ref[2:3, :]
    kk = kkr * lax.rsqrt(_head_sum(kkr * kkr, RWKV_HEAD_DIM) + 1e-6)
    kr2 = kr * (1.0 + (a - 1.0) * rvec_ref[3:4, :])
    logw, kk, kr2 = vmask(logw), vmask(kk), vmask(kr2)
    gcum = _mm(tri, logw, True)
    e_pos = jnp.exp(gcum)
    e_neg = jnp.exp(-gcum)
    a_t = -kk * jnp.exp(gcum - logw)
    b_t = kk * a * e_neg
    k_t = kr2 * e_neg
    r_t = r * e_pos
    gw = GROUP * RWKV_HEAD_DIM
    bd_mask = (lax.broadcasted_iota(jnp.int32, (gw, gw), 0) // RWKV_HEAD_DIM
               == lax.broadcasted_iota(jnp.int32, (gw, gw), 1) // RWKV_HEAD_DIM)
    o_r = []
    for g in range(RWKV_HEADS // GROUP):
        sl = slice(g * gw, (g + 1) * gw)
        at_g, bt_g, kt_g, rt_g, v_g = a_t[:, sl], b_t[:, sl], k_t[:, sl], r_t[:, sl], vr[:, sl]
        aa = _mm_nt(jnp.concatenate([at_g, rt_g], axis=0),
                    jnp.concatenate([_block_stack(bt_g, GROUP, RWKV_HEAD_DIM),
                                     _block_stack(kt_g, GROUP, RWKV_HEAD_DIM)], axis=0), precise)
        a_ab = jnp.where(strict_p, aa[0:c, 0:gc4], 0.0)
        a_ak = jnp.where(strict_p, aa[0:c, gc4:2 * gc4], 0.0)
        a_rb = jnp.where(lower_p, aa[c:2 * c, 0:gc4], 0.0)
        a_rk = jnp.where(lower_p, aa[c:2 * c, gc4:2 * gc4], 0.0)
        tinv_r = _inv_series(a_ab, c, GROUP, precise)
        akv = _mm(a_ak, _block_stack(v_g, GROUP, RWKV_HEAD_DIM), precise)
        wu0 = _mm(tinv_r, jnp.concatenate([_block_stack(at_g, GROUP, RWKV_HEAD_DIM),
                                           _block_stack(akv, GROUP, RWKV_HEAD_DIM)], axis=1), precise)
        s_g = sr_scr[g]
        wr = _mm_nt(jnp.concatenate([wu0[:, 0:gw], rt_g], axis=0), s_g, precise)
        u = wr[0:c] + wu0[:, gw:2 * gw]
        o_g = wr[c:2 * c] + _mm(jnp.concatenate([a_rb, a_rk], axis=1),
                                jnp.concatenate([_block_stack(u, GROUP, RWKV_HEAD_DIM),
                                                 _block_stack(v_g, GROUP, RWKV_HEAD_DIM)], axis=0), precise)
        upd = _mm_tn(jnp.concatenate([u, v_g], axis=0), jnp.concatenate([bt_g, kt_g], axis=0), precise)
        sr_scr[g] = (s_g + jnp.where(bd_mask, upd, 0.0)) * e_pos[c - 1:c, sl]
        o_r.append(o_g)
    o_r = jnp.concatenate(o_r, axis=1)
    mean_o = _head_sum(o_r, RWKV_HEAD_DIM) * (1.0 / RWKV_HEAD_DIM)
    dev = o_r - mean_o
    var_o = _head_sum(dev * dev, RWKV_HEAD_DIM) * (1.0 / RWKV_HEAD_DIM)
    o_n = dev * lax.rsqrt(var_o + RWKV_GN_EPS) * rvec_ref[5:6, :] + rvec_ref[6:7, :]
    bonus = _head_sum(r * kr2 * rvec_ref[4:5, :], RWKV_HEAD_DIM) * vr
    o_rwkv = (o_n + bonus) * gate

    mix = jnp.concatenate(o_gdn + [o_rwkv], axis=1)
    mix_ref[0] = mix[0:rows_in].astype(mix_ref.dtype)
    prev_scr[...] = x[:, 0:COL_BA]

    @pl.when(ci == n_chunks - 1)
    def _():
        convo_ref[0] = x[t_last - (CONV_K - 1):t_last, 0:GDN_QKV]
        shifto_ref[0] = x[t_last - 1:t_last, COL_RWKV:COL_BA]
        sgo_ref[0] = sg_scr[...]
        sro_ref[0] = sr_scr[...]


def _mixer(proj3d, seq_len, conv_init8, shift_init, s_gdn, s_rwkv_bd, mp, precise):
    bsz, rows, _ = proj3d.shape
    c = CHUNK
    if rows >= c:
        assert rows == seq_len and seq_len % c == 0
        n_chunks, rows_in, t_last = seq_len // c, c, c
    else:
        n_chunks, rows_in, t_last = 1, rows, seq_len
    assert t_last >= CONV_K - 1
    gw = GROUP * RWKV_HEAD_DIM
    n_grp = RWKV_HEADS // GROUP
    const2 = lambda b, i: (0, 0)
    perb3 = lambda b, i: (b, 0, 0)
    perb4 = lambda b, i: (b, 0, 0, 0)
    kern = functools.partial(_mixer_kernel, c, rows_in, t_last, n_chunks, precise)
    return pl.pallas_call(
        kern,
        grid=(bsz, n_chunks),
        in_specs=[pl.BlockSpec((1, rows_in, PROJ_COLS), lambda b, i: (b, i, 0)),
                  pl.BlockSpec((1, SUBLANES, GDN_QKV), perb3),
                  pl.BlockSpec((1, 1, RWKV_COLS), perb3),
                  pl.BlockSpec((1, GDN_HEADS, GDN_HEAD_DIM, GDN_HEAD_DIM), perb4),
                  pl.BlockSpec((1, n_grp, gw, gw), perb4),
                  pl.BlockSpec((CONV_K, GDN_QKV), const2),
                  pl.BlockSpec((SUBLANES, LANES), const2),
                  pl.BlockSpec((1, RWKV_COLS), const2),
                  pl.BlockSpec((SUBLANES, RWKV_WIDTH), const2),
                  pl.BlockSpec((LANES, 2 * RWKV_WIDTH), const2),
                  pl.BlockSpec((GATE_LORA, RWKV_WIDTH), const2)],
        out_specs=[pl.BlockSpec((1, rows_in, D_MODEL), lambda b, i: (b, i, 0)),
                   pl.BlockSpec((1, CONV_K - 1, GDN_QKV), perb3),
                   pl.BlockSpec((1, 1, RWKV_COLS), perb3),
                   pl.BlockSpec((1, GDN_HEADS, GDN_HEAD_DIM, GDN_HEAD_DIM), perb4),
                   pl.BlockSpec((1, n_grp, gw, gw), perb4)],
        out_shape=[jax.ShapeDtypeStruct((bsz, rows, D_MODEL), BF16),
                   jax.ShapeDtypeStruct((bsz, CONV_K - 1, GDN_QKV), F32),
                   jax.ShapeDtypeStruct((bsz, 1, RWKV_COLS), F32),
                   jax.ShapeDtypeStruct((bsz, GDN_HEADS, GDN_HEAD_DIM, GDN_HEAD_DIM), F32),
                   jax.ShapeDtypeStruct((bsz, n_grp, gw, gw), F32)],
        scratch_shapes=[pltpu.VMEM((c, COL_BA), F32),
                        pltpu.VMEM((GDN_HEADS, GDN_HEAD_DIM, GDN_HEAD_DIM), F32),
                        pltpu.VMEM((n_grp, gw, gw), F32)],
        compiler_params=pltpu.CompilerParams(dimension_semantics=("parallel", "arbitrary"),
                                             vmem_limit_bytes=VMEM_LIMIT_BYTES),
        name="sequence_mixers",
    )(proj3d, conv_init8, shift_init, s_gdn, s_rwkv_bd,
      mp["conv_w"], mp["gvec"], mp["mu"], mp["rvec"], mp["lora_w"], mp["g2"])


def _rwkv_state_to_blockdiag(s):
    bsz = s.shape[0]
    n_grp = RWKV_HEADS // GROUP
    s5 = s.reshape(bsz, n_grp, GROUP, RWKV_HEAD_DIM, RWKV_HEAD_DIM)
    eye = jnp.eye(GROUP, dtype=s.dtype)
    bd = s5[:, :, :, :, None, :] * eye[None, None, :, None, :, None]
    return bd.reshape(bsz, n_grp, GROUP * RWKV_HEAD_DIM, GROUP * RWKV_HEAD_DIM)


def _rwkv_state_from_blockdiag(bd):
    bsz = bd.shape[0]
    n_grp = RWKV_HEADS // GROUP
    b6 = bd.reshape(bsz, n_grp, GROUP, RWKV_HEAD_DIM, GROUP, RWKV_HEAD_DIM)
    blocks = [b6[:, :, h, :, h, :] for h in range(GROUP)]
    return jnp.stack(blocks, axis=2).reshape(bsz, RWKV_HEADS, RWKV_HEAD_DIM, RWKV_HEAD_DIM)


def _layernorm(x, g, b):
    mu = jnp.mean(x, axis=-1, keepdims=True)
    d = x - mu
    var = jnp.mean(d * d, axis=-1, keepdims=True)
    return d * lax.rsqrt(var + LN_EPS) * g + b


def _post_mixer_kernel(mix_ref, x_ref, wo_ref, lnv_ref, wrt_ref, br_ref,
                       h_ref, idx_ref, rank_ref, gate_ref, cnt_ref, base_scr):
    i = pl.program_id(0)
    tt = mix_ref.shape[0]

    @pl.when(i == 0)
    def _():
        base_scr[...] = jnp.zeros(base_scr.shape, F32)

    hp = ALPHA * x_ref[...] + jnp.dot(mix_ref[...], wo_ref[...], preferred_element_type=F32)
    h = _layernorm(hp, lnv_ref[0:1, :], lnv_ref[1:2, :])
    h_ref[...] = h
    lt = lax.dot_general(wrt_ref[...], h, (((1,), (1,)), ((), ())),
                         precision=lax.Precision.HIGHEST, preferred_element_type=F32) + br_ref[...]
    eid = lax.broadcasted_iota(jnp.int32, lt.shape, 0)
    lt = jnp.where(eid < N_EXPERTS, lt, -jnp.inf)
    upper = jnp.where(lax.broadcasted_iota(jnp.int32, (tt, tt), 0) < lax.broadcasted_iota(jnp.int32, (tt, tt), 1),
                      1.0, 0.0).astype(BF16)
    base = base_scr[...]
    vals, idxs, ranks = [], [], []
    for _ in range(TOP_K):
        m = jnp.max(lt, axis=0, keepdims=True)
        sel = jnp.min(jnp.where(lt == m, eid, LANES), axis=0, keepdims=True)
        onehot = eid == sel
        lt = jnp.where(onehot, -jnp.inf, lt)
        oh = jnp.where(onehot, 1.0, 0.0)
        before = jnp.dot(oh.astype(BF16), upper, preferred_element_type=F32)
        ranks.append(jnp.sum(oh * (base + before), axis=0, keepdims=True))
        base = base + jnp.sum(oh, axis=1, keepdims=True)
        vals.append(m)
        idxs.append(sel)
    base_scr[...] = base
    ex = [jnp.exp(v - vals[0]) for v in vals]
    den = ex[0] + ex[1] + ex[2] + ex[3]
    pad_i = jnp.zeros((SUBLANES - TOP_K, tt), jnp.int32)
    idx_ref[...] = jnp.concatenate(idxs + [pad_i], axis=0)
    rank_ref[...] = jnp.concatenate([rk.astype(jnp.int32) for rk in ranks] + [pad_i], axis=0)
    gates = jnp.concatenate([e / den for e in ex] + [jnp.zeros((SUBLANES - TOP_K, tt), F32)], axis=0)
    gate_ref[...] = gates.T

    @pl.when(i == pl.num_programs(0) - 1)
    def _():
        cnt_ref[...] = base[:, 0:LANES].astype(jnp.int32)


def _post_mixer(mix2d, x2d, w_o_bf16, ln1, w_router_t, b_router_col):
    n_tok = x2d.shape[0]
    tt = TOK_TILE
    assert n_tok % tt == 0
    const2 = lambda i: (0, 0)
    return pl.pallas_call(
        _post_mixer_kernel,
        grid=(n_tok // tt,),
        in_specs=[pl.BlockSpec((tt, D_MODEL), lambda i: (i, 0)),
                  pl.BlockSpec((tt, D_MODEL), lambda i: (i, 0)),
                  pl.BlockSpec((D_MODEL, D_MODEL), const2),
                  pl.BlockSpec((SUBLANES, D_MODEL), const2),
                  pl.BlockSpec((LANES, D_MODEL), const2),
                  pl.BlockSpec((LANES, 1), const2)],
        out_specs=[pl.BlockSpec((tt, D_MODEL), lambda i: (i, 0)),
                   pl.BlockSpec((SUBLANES, tt), lambda i: (0, i)),
                   pl.BlockSpec((SUBLANES, tt), lambda i: (0, i)),
                   pl.BlockSpec((tt, SUBLANES), lambda i: (i, 0)),
                   pl.BlockSpec((LANES, LANES), const2)],
        out_shape=[jax.ShapeDtypeStruct((n_tok, D_MODEL), F32),
                   jax.ShapeDtypeStruct((SUBLANES, n_tok), jnp.int32),
                   jax.ShapeDtypeStruct((SUBLANES, n_tok), jnp.int32),
                   jax.ShapeDtypeStruct((n_tok, SUBLANES), F32),
                   jax.ShapeDtypeStruct((LANES, LANES), jnp.int32)],
        scratch_shapes=[pltpu.VMEM((LANES, tt), F32)],
        compiler_params=pltpu.CompilerParams(dimension_semantics=("arbitrary",),
                                             vmem_limit_bytes=VMEM_LIMIT_BYTES),
        name="outproj_norm_router",
    )(mix2d, x2d, w_o_bf16, ln1, w_router_t, b_router_col)


def _slot_kernel(pstart_ref, idx_ref, rank_ref, dest_ref):
    idx = idx_ref[...]
    dest = rank_ref[...]
    for e in range(N_EXPERTS):
        dest = dest + jnp.where(idx == e, pstart_ref[e], 0)
    dest_ref[...] = dest


def _slots(idx, rank, pstart):
    grid_spec = pltpu.PrefetchScalarGridSpec(
        num_scalar_prefetch=1,
        grid=(1,),
        in_specs=[pl.BlockSpec(idx.shape, lambda i, ps: (0, 0)),
                  pl.BlockSpec(idx.shape, lambda i, ps: (0, 0))],
        out_specs=pl.BlockSpec(idx.shape, lambda i, ps: (0, 0)),
    )
    return pl.pallas_call(
        _slot_kernel,
        grid_spec=grid_spec,
        out_shape=jax.ShapeDtypeStruct(idx.shape, jnp.int32),
        name="moe_slots",
    )(pstart, idx, rank)


DMA_UNROLL = 8


def _dispatch_kernel(dest_ref, h_ref, xb_in_hbm, xb_hbm, sem):
    del xb_in_hbm
    tt = h_ref.shape[0]

    def row_copy(t, k):
        return pltpu.make_async_copy(h_ref.at[pl.ds(t, 1), :], xb_hbm.at[pl.ds(dest_ref[k, t], 1), :], sem)

    def issue(t, carry):
        for k in range(TOP_K):
            row_copy(t, k).start()
        return carry

    lax.fori_loop(0, tt, issue, 0, unroll=DMA_UNROLL)

    def drain(t, carry):
        for k in range(TOP_K):
            row_copy(t, k).wait()
        return carry

    lax.fori_loop(0, tt, drain, 0, unroll=DMA_UNROLL)


def _dispatch(h2d, dest, n_rows):
    n_tok = h2d.shape[0]
    tt = DISPATCH_TILE
    assert n_tok % tt == 0
    xb0 = jnp.zeros((n_rows, D_MODEL), F32)
    return pl.pallas_call(
        _dispatch_kernel,
        grid=(n_tok // tt,),
        in_specs=[pl.BlockSpec((SUBLANES, tt), lambda i: (0, i), memory_space=pltpu.SMEM),
                  pl.BlockSpec((tt, D_MODEL), lambda i: (i, 0)),
                  pl.BlockSpec(memory_space=pl.ANY)],
        out_specs=pl.BlockSpec(memory_space=pl.ANY),
        out_shape=jax.ShapeDtypeStruct((n_rows, D_MODEL), F32),
        scratch_shapes=[pltpu.SemaphoreType.DMA(())],
        input_output_aliases={2: 0},
        compiler_params=pltpu.CompilerParams(dimension_semantics=("arbitrary",)),
        name="moe_dispatch",
    )(dest, h2d, xb0)


def _expert_kernel(be_ref, nused_ref, x_ref, wg_ref, wu_ref, wd_ref, bg_ref, bu_ref, bd_ref, y_ref,
                   wg16, wu16, wd16):
    i = pl.program_id(0)
    prev_e = be_ref[jnp.maximum(i - 1, 0)]
    fresh = jnp.logical_or(i == 0, be_ref[i] != prev_e)

    @pl.when(fresh)
    def _():
        wg16[...] = wg_ref[0].astype(BF16)
        wu16[...] = wu_ref[0].astype(BF16)
        wd16[...] = wd_ref[0].astype(BF16)

    @pl.when(i < nused_ref[0])
    def _():
        x16 = x_ref[...].astype(BF16)
        gt = jnp.dot(x16, wg16[...], preferred_element_type=F32) + bg_ref[0]
        up = jnp.dot(x16, wu16[...], preferred_element_type=F32) + bu_ref[0]
        gt = jnp.minimum(gt, SWIGLU_LIMIT)
        up = jnp.clip(up, -SWIGLU_LIMIT, SWIGLU_LIMIT)
        hid = (up + 1.0) * gt * _sigmoid(SWIGLU_ALPHA * gt)
        y_ref[...] = jnp.dot(hid.astype(BF16), wd16[...], preferred_element_type=F32) + bd_ref[0]

    @pl.when(i >= nused_ref[0])
    def _():
        y_ref[...] = jnp.zeros(y_ref.shape, F32)


def _experts(xb, block_expert, n_used, w_gate, b_gate, w_up, b_up, w_down, b_down):
    n_rows = xb.shape[0]
    tm = MOE_TILE
    n_blocks = n_rows // tm
    d_e = w_gate.shape[2]
    wspec = lambda shape: pl.BlockSpec(shape, lambda i, be, nu: (be[i], 0, 0))
    grid_spec = pltpu.PrefetchScalarGridSpec(
        num_scalar_prefetch=2,
        grid=(n_blocks,),
        in_specs=[pl.BlockSpec((tm, D_MODEL), lambda i, be, nu: (i, 0)),
                  wspec((1, D_MODEL, d_e)), wspec((1, D_MODEL, d_e)), wspec((1, d_e, D_MODEL)),
                  wspec((1, 1, d_e)), wspec((1, 1, d_e)), wspec((1, 1, D_MODEL))],
        out_specs=pl.BlockSpec((tm, D_MODEL), lambda i, be, nu: (i, 0)),
        scratch_shapes=[pltpu.VMEM((D_MODEL, d_e), BF16), pltpu.VMEM((D_MODEL, d_e), BF16),
                        pltpu.VMEM((d_e, D_MODEL), BF16)],
    )
    return pl.pallas_call(
        _expert_kernel,
        grid_spec=grid_spec,
        out_shape=jax.ShapeDtypeStruct((n_rows, D_MODEL), F32),
        compiler_params=pltpu.CompilerParams(dimension_semantics=("arbitrary",),
                                             vmem_limit_bytes=VMEM_LIMIT_BYTES),
        name="moe_experts",
    )(block_expert, n_used, xb, w_gate, w_up, w_down,
      b_gate[:, None, :], b_up[:, None, :], b_down[:, None, :])


def _combine_kernel(dest_ref, h_ref, gate_ref, lnv_ref, yb_hbm, y_ref, buf, sem):
    tt = h_ref.shape[0]

    def row_copy(t, k):
        return pltpu.make_async_copy(yb_hbm.at[pl.ds(dest_ref[k, t], 1), :], buf.at[k, pl.ds(t, 1), :], sem)

    def issue(t, carry):
        for k in range(TOP_K):
            row_copy(t, k).start()
        return carry

    lax.fori_loop(0, tt, issue, 0, unroll=DMA_UNROLL)

    def drain(t, carry):
        for k in range(TOP_K):
            row_copy(t, k).wait()
        return carry

    lax.fori_loop(0, tt, drain, 0, unroll=DMA_UNROLL)
    gates = gate_ref[...]
    f = buf[0] * gates[:, 0:1]
    for k in range(1, TOP_K):
        f = f + buf[k] * gates[:, k:k + 1]
    y_ref[...] = _layernorm(ALPHA * h_ref[...] + f, lnv_ref[0:1, :], lnv_ref[1:2, :])


def _combine(yb, h2d, dest, gates, ln2):
    n_tok = h2d.shape[0]
    tt = COMBINE_TILE
    assert n_tok % tt == 0
    return pl.pallas_call(
        _combine_kernel,
        grid=(n_tok // tt,),
        in_specs=[pl.BlockSpec((SUBLANES, tt), lambda i: (0, i), memory_space=pltpu.SMEM),
                  pl.BlockSpec((tt, D_MODEL), lambda i: (i, 0)),
                  pl.BlockSpec((tt, SUBLANES), lambda i: (i, 0)),
                  pl.BlockSpec((SUBLANES, D_MODEL), lambda i: (0, 0)),
                  pl.BlockSpec(memory_space=pl.ANY)],
        out_specs=pl.BlockSpec((tt, D_MODEL), lambda i: (i, 0)),
        out_shape=jax.ShapeDtypeStruct((n_tok, D_MODEL), F32),
        scratch_shapes=[pltpu.VMEM((TOP_K, tt, D_MODEL), F32), pltpu.SemaphoreType.DMA(())],
        compiler_params=pltpu.CompilerParams(dimension_semantics=("arbitrary",),
                                             vmem_limit_bytes=VMEM_LIMIT_BYTES),
        name="moe_combine_norm",
    )(dest, h2d, gates, ln2, yb)


def _pad_rows(v, rows):
    return jnp.concatenate([v, jnp.zeros((rows - v.shape[0],) + v.shape[1:], v.dtype)], axis=0)


def _mixer_params(conv_w, a_log, dt_bias, gdn_norm_w, mu_shift, w0, w2, a0, a2, g2, k_k, k_a, r_k, lnx_w, lnx_b):
    gvec = jnp.zeros((SUBLANES, LANES), F32)
    gvec = gvec.at[0, GDN_HEADS:2 * GDN_HEADS].set(a_log).at[1, GDN_HEADS:2 * GDN_HEADS].set(dt_bias)
    gvec = gvec.at[2, :].set(gdn_norm_w)
    rvec = _pad_rows(jnp.stack([w0, a0, k_k, k_a, r_k.reshape(-1), lnx_w, lnx_b]), SUBLANES)
    lora_w = jnp.zeros((LANES, 2 * RWKV_WIDTH), F32)
    lora_w = lora_w.at[0:DECAY_LORA, 0:RWKV_WIDTH].set(w2).at[DECAY_LORA:, RWKV_WIDTH:].set(a2)
    return dict(conv_w=conv_w, gvec=gvec, mu=mu_shift[None, :], rvec=rvec,
                lora_w=lora_w.astype(BF16), g2=g2.astype(BF16))


def _layer(x_prompt, x_sample, state_conv, state_shift, state_gdn, state_rwkv,
           w_in, mixer_params, w_o, ln1_g, ln1_b, w_router, b_router,
           w_gate, b_gate, w_up, b_up, w_down, b_down, ln2_g, ln2_b, precise):
    bp, tp, d = x_prompt.shape
    bs, ts, _ = x_sample.shape
    n_p, n_s = bp * tp, bs * ts
    n_tok = n_p + n_s
    x2d = jnp.concatenate([x_prompt.reshape(n_p, d), x_sample.reshape(n_s, d)], axis=0)

    in_cols = w_in.shape[1]
    off_ba = COL_Z + GDN_WIDTH
    w_in_r = jnp.concatenate([w_in[:, :off_ba], w_in[:, off_ba + 2 * GDN_HEADS:],
                              w_in[:, off_ba:off_ba + 2 * GDN_HEADS],
                              jnp.zeros((d, PROJ_COLS - in_cols), w_in.dtype)], axis=1).astype(BF16)

    rows_s = -(-ts // SUBLANES) * SUBLANES
    proj_p = _input_projection(x_prompt.reshape(n_p, d), w_in_r).reshape(bp, tp, PROJ_COLS)
    proj_s = _input_projection(x_sample.reshape(n_s, d), w_in_r).reshape(bs, ts, PROJ_COLS)
    if rows_s != ts:
        proj_s = jnp.pad(proj_s, ((0, 0), (0, rows_s - ts), (0, 0)))
    gw = GROUP * RWKV_HEAD_DIM
    zeros_p = (jnp.zeros((bp, SUBLANES, GDN_QKV), F32), jnp.zeros((bp, 1, RWKV_COLS), F32),
               jnp.zeros((bp, GDN_HEADS, GDN_HEAD_DIM, GDN_HEAD_DIM), F32),
               jnp.zeros((bp, RWKV_HEADS // GROUP, gw, gw), F32))
    mix_p, conv_p, shift_p, gdn_p, rwkv_p = _mixer(proj_p, tp, *zeros_p, mixer_params, precise)
    conv8 = jnp.pad(state_conv, ((0, 0), (SUBLANES - (CONV_K - 1), 0), (0, 0)))
    mix_s, conv_s, shift_s, gdn_s, rwkv_s = _mixer(
        proj_s, ts, conv8, state_shift[:, None, :], state_gdn, _rwkv_state_to_blockdiag(state_rwkv),
        mixer_params, precise)
    mix2d = jnp.concatenate([mix_p.reshape(n_p, d), mix_s[:, :ts].reshape(n_s, d)], axis=0)

    ln1 = _pad_rows(jnp.stack([ln1_g, ln1_b]), SUBLANES)
    ln2 = _pad_rows(jnp.stack([ln2_g, ln2_b]), SUBLANES)
    wrt = _pad_rows(w_router.T, LANES)
    brc = _pad_rows(b_router[:, None], LANES)
    h2d, idx, rank, gates, cnt = _post_mixer(mix2d, x2d, w_o.astype(BF16), ln1, wrt, brc)

    counts = cnt[:N_EXPERTS, 0]
    padded = ((counts + MOE_TILE - 1) // MOE_TILE) * MOE_TILE
    pend = jnp.cumsum(padded)
    pstart = (pend - padded).astype(jnp.int32)
    n_blocks = -(-(n_tok * TOP_K) // MOE_TILE) + N_EXPERTS
    n_rows = n_blocks * MOE_TILE
    block_expert = jnp.minimum(
        jnp.sum(pend[None, :] <= (jnp.arange(n_blocks) * MOE_TILE)[:, None], axis=1), N_EXPERTS - 1).astype(jnp.int32)
    n_used = (pend[-1:] // MOE_TILE).astype(jnp.int32)

    dest = _slots(idx, rank, pstart)
    xb = _dispatch(h2d, dest, n_rows)
    yb = _experts(xb, block_expert, n_used, w_gate, b_gate, w_up, b_up, w_down, b_down)
    y2d = _combine(yb, h2d, dest, gates, ln2)

    y_p = y2d[:n_p].reshape(bp, tp, d)
    y_s = y2d[n_p:].reshape(bs, ts, d)
    return (y_p, y_s, conv_p, shift_p[:, 0], gdn_p, _rwkv_state_from_blockdiag(rwkv_p),
            conv_s, shift_s[:, 0], gdn_s, _rwkv_state_from_blockdiag(rwkv_s))


def kernel(x_prompt, x_sample, state_conv, state_shift, state_gdn, state_rwkv, w_in, conv_w, a_log, dt_bias,
           gdn_norm_w, mu_shift, w0, w2, a0, a2, g2, k_k, k_a, r_k, lnx_w, lnx_b, w_o, ln1_g, ln1_b,
           w_router, b_router, w_gate, b_gate, w_up, b_up, w_down, b_down, ln2_g, ln2_b):
    mp = _mixer_params(conv_w, a_log, dt_bias, gdn_norm_w, mu_shift, w0, w2, a0, a2, g2, k_k, k_a, r_k,
                       lnx_w, lnx_b)
    return _layer(x_prompt, x_sample, state_conv, state_shift, state_gdn, state_rwkv,
                  w_in, mp, w_o, ln1_g, ln1_b, w_router, b_router,
                  w_gate, b_gate, w_up, b_up, w_down, b_down, ln2_g, ln2_b, precise=False)
```

```python
import functools
import math

import jax
import jax.numpy as jnp
from jax import lax
from jax.experimental import pallas as pl
from jax.experimental.pallas import tpu as pltpu

F32 = jnp.float32
BF16 = jnp.bfloat16

D_MODEL = 1024
GDN_HEADS = 4
GDN_HEAD_DIM = 128
GDN_WIDTH = GDN_HEADS * GDN_HEAD_DIM
GDN_QKV = 3 * GDN_WIDTH
CONV_K = 4
RWKV_HEADS = 8
RWKV_HEAD_DIM = 64
RWKV_WIDTH = RWKV_HEADS * RWKV_HEAD_DIM
DECAY_LORA = 64
AAA_LORA = 64
GATE_LORA = 128
RWKV_COLS = 3 * RWKV_WIDTH + DECAY_LORA + AAA_LORA + GATE_LORA
RWKV_GN_EPS = RWKV_HEAD_DIM * 1e-5
N_EXPERTS = 32
TOP_K = 4
SWIGLU_LIMIT = 7.0
SWIGLU_ALPHA = 1.702
DEPTH = 1
ALPHA = (2.0 * DEPTH) ** 0.25
LN_EPS = 1e-5

LANES = 128
SUBLANES = 8
VMEM_LIMIT_BYTES = 56 * 1024 * 1024

COL_Z = GDN_QKV
COL_RWKV = COL_Z + GDN_WIDTH
COL_BA = COL_RWKV + RWKV_COLS
PROJ_COLS = COL_BA + LANES

CHUNK = 64
GROUP = 4
RWKV_GROUPS = RWKV_HEADS // GROUP
GROUP_W = GROUP * RWKV_HEAD_DIM
SEQS_PER_STEP = 2
PROJ_TILE = 256
TOK_TILE = 256
MOE_TILE = 256
DISPATCH_TILE = 256
COMBINE_TILE = 128
DMA_UNROLL = 8


def _mm(a, b):
    return jnp.dot(a.astype(BF16), b.astype(BF16), preferred_element_type=F32)


def _mm_nt(a, b):
    return lax.dot_general(a.astype(BF16), b.astype(BF16), (((1,), (1,)), ((), ())), preferred_element_type=F32)


def _mm_tn(a, b):
    return _mm(a.T, b)


def _mm_f32(a, b):
    return jnp.dot(a, b, precision=lax.Precision.HIGHEST, preferred_element_type=F32)


def _sigmoid(x):
    return 0.5 * jnp.tanh(0.5 * x) + 0.5


def _softplus(x):
    return jnp.maximum(x, 0.0) + jnp.log1p(jnp.exp(-jnp.abs(x)))


def _proj_kernel(x_ref, w_ref, o_ref):
    o_ref[...] = jnp.dot(x_ref[...].astype(BF16), w_ref[...], preferred_element_type=F32)


def _input_projection(x2d, w_in_bf16):
    n_tok = x2d.shape[0]
    assert n_tok % PROJ_TILE == 0
    return pl.pallas_call(
        _proj_kernel,
        grid=(n_tok // PROJ_TILE,),
        in_specs=[pl.BlockSpec((PROJ_TILE, D_MODEL), lambda i: (i, 0)),
                  pl.BlockSpec((D_MODEL, PROJ_COLS), lambda i: (0, 0))],
        out_specs=pl.BlockSpec((PROJ_TILE, PROJ_COLS), lambda i: (i, 0)),
        out_shape=jax.ShapeDtypeStruct((n_tok, PROJ_COLS), F32),
        compiler_params=pltpu.CompilerParams(dimension_semantics=("parallel",),
                                             vmem_limit_bytes=VMEM_LIMIT_BYTES),
        name="input_projection",
    )(x2d, w_in_bf16)


def _shift_rows(cur, prev, s, row_ids):
    return jnp.where(row_ids < s, pltpu.roll(prev, s, axis=0), pltpu.roll(cur, s, axis=0))


def _block_stack(x, n, width):
    grp = lax.broadcasted_iota(jnp.int32, x.shape, 1) // width
    if x.shape[0] % (2 * SUBLANES) == 0:
        x16 = x.astype(BF16)
        return jnp.concatenate([x16 * jnp.where(grp == i, 1.0, 0.0).astype(BF16) for i in range(n)], axis=0)
    return jnp.concatenate([jnp.where(grp == i, x, 0.0) for i in range(n)], axis=0)


def _inv_series(nils, c, n):
    row = lax.broadcasted_iota(jnp.int32, nils[0].shape, 0)
    col = lax.broadcasted_iota(jnp.int32, nils[0].shape, 1) % c
    eye = jnp.where(row == col, 1.0, 0.0)
    xs = [eye + nil for nil in nils]
    qs = [_mm(nil, _block_stack(nil, n, c)) for nil in nils]
    levels = int(math.log2(c))
    for lvl in range(1, levels):
        nxt_q, nxt_x = [], []
        for q, x in zip(qs, xs):
            bd = _block_stack(q, n, c)
            if lvl == levels - 1:
                nxt_x.append(x + _mm(x, bd))
            else:
                both = _mm(jnp.concatenate([q, x], axis=0), bd)
                nxt_q.append(both[0:c])
                nxt_x.append(x + both[c:2 * c])
        qs, xs = nxt_q, nxt_x
    return xs


def _head_sum(x, width):
    pieces = []
    for j in range(x.shape[1] // LANES):
        xb = x[:, j * LANES:(j + 1) * LANES]
        if width == LANES:
            pieces.append(jnp.broadcast_to(jnp.sum(xb, axis=-1, keepdims=True), xb.shape))
        else:
            lo = lax.broadcasted_iota(jnp.int32, xb.shape, 1) < width
            s0 = jnp.sum(jnp.where(lo, xb, 0.0), axis=-1, keepdims=True)
            s1 = jnp.sum(jnp.where(lo, 0.0, xb), axis=-1, keepdims=True)
            pieces.append(jnp.where(lo, s0, s1))
    return jnp.concatenate(pieces, axis=1)


def _gdn_head(a, h):
    return a[:, h * GDN_HEAD_DIM:(h + 1) * GDN_HEAD_DIM]


def _gdn_prepare(x, prev, c, t_last, masks, convw_ref, gvec_ref):
    row_p, col_p, grp_p, lower_p, strict_p, tri = masks
    masked = t_last < c

    def vmask(a):
        if not masked:
            return a
        return jnp.where(lax.broadcasted_iota(jnp.int32, a.shape, 0) < t_last, a, 0.0)

    xq = x[:, 0:GDN_QKV]
    pq = prev[:, 0:GDN_QKV]
    rid = lax.broadcasted_iota(jnp.int32, (c, GDN_QKV), 0)
    conv = _shift_rows(xq, pq, 3, rid) * convw_ref[0:1, :]
    conv = conv + _shift_rows(xq, pq, 2, rid) * convw_ref[1:2, :]
    conv = conv + _shift_rows(xq, pq, 1, rid) * convw_ref[2:3, :]
    conv = conv + xq * convw_ref[3:4, :]
    conv = conv * _sigmoid(conv)

    ba = x[:, COL_BA:COL_BA + LANES]
    beta_blk = vmask(_sigmoid(ba))
    g_blk = vmask(-jnp.exp(gvec_ref[0:1, :]) * _softplus(ba + gvec_ref[1:2, :]))

    qn, kn, kb, vb = [], [], [], []
    for h in range(GDN_HEADS):
        qh = _gdn_head(conv[:, 0:GDN_WIDTH], h)
        kh = _gdn_head(conv[:, GDN_WIDTH:2 * GDN_WIDTH], h)
        vh = vmask(_gdn_head(conv[:, 2 * GDN_WIDTH:GDN_QKV], h))
        qh = qh * (lax.rsqrt(jnp.sum(qh * qh, axis=-1, keepdims=True) + 1e-6) * (GDN_HEAD_DIM ** -0.5))
        kh = vmask(kh * lax.rsqrt(jnp.sum(kh * kh, axis=-1, keepdims=True) + 1e-6))
        bh = beta_blk[:, h:h + 1]
        qn.append(qh), kn.append(kh), kb.append(kh * bh), vb.append(vh * bh)

    ma = _mm_nt(jnp.concatenate([jnp.concatenate(kb, axis=1), jnp.concatenate(qn, axis=1)], axis=0),
                _block_stack(jnp.concatenate(kn, axis=1), GDN_HEADS, GDN_HEAD_DIM))
    gexp = jnp.zeros(row_p.shape, F32)
    for h in range(GDN_HEADS):
        gexp = jnp.where(grp_p == h, g_blk[:, GDN_HEADS + h:GDN_HEADS + h + 1], gexp)
    gcol = _mm_f32(tri, gexp)
    grow = jnp.sum(jnp.where(row_p == col_p, gcol, 0.0), axis=0, keepdims=True)
    decay = jnp.exp(jnp.where(lower_p, gcol - grow, -jnp.inf))
    m_p = jnp.where(strict_p, ma[0:c] * decay, 0.0)
    attn = ma[c:2 * c] * decay
    egc, kdec, glast = [], [], []
    for h in range(GDN_HEADS):
        gch = gcol[:, h * c:h * c + 1]
        glh = gcol[c - 1:c, h * c:h * c + 1]
        egc.append(jnp.exp(gch))
        kdec.append(kn[h] * jnp.exp(glh - gch))
        glast.append(jnp.exp(glh))
    rhs2 = jnp.concatenate(
        [_block_stack(jnp.concatenate(vb, axis=1), GDN_HEADS, GDN_HEAD_DIM),
         _block_stack(jnp.concatenate([kb[h] * egc[h] for h in range(GDN_HEADS)], axis=1),
                      GDN_HEADS, GDN_HEAD_DIM)], axis=1)
    qdec = [qn[h] * egc[h] for h in range(GDN_HEADS)]
    return dict(nil=-m_p, attn=attn, rhs2=rhs2, qdec=qdec, kdec=kdec, glast=glast)


def _gdn_finish(p, tinv, x, c, sg_scr, j, gvec_ref):
    uw = _mm(tinv, p["rhs2"])
    vnew, o1 = [], []
    for h in range(GDN_HEADS):
        ws = _mm(jnp.concatenate([_gdn_head(uw[:, GDN_WIDTH:2 * GDN_WIDTH], h), p["qdec"][h]], axis=0),
                 sg_scr[j, h])
        vnew.append(_gdn_head(uw[:, 0:GDN_WIDTH], h) - ws[0:c])
        o1.append(ws[c:2 * c])
    o2 = _mm(p["attn"], _block_stack(jnp.concatenate(vnew, axis=1), GDN_HEADS, GDN_HEAD_DIM))
    out = []
    for h in range(GDN_HEADS):
        sg_scr[j, h] = sg_scr[j, h] * p["glast"][h] + _mm_tn(p["kdec"][h], vnew[h])
        oh = o1[h] + _gdn_head(o2, h)
        zh = _gdn_head(x[:, COL_Z:COL_Z + GDN_WIDTH], h)
        oh = oh * lax.rsqrt(jnp.mean(oh * oh, axis=-1, keepdims=True) + 1e-6) * gvec_ref[2:3, :]
        out.append(oh * (zh * _sigmoid(zh)))
    return out


def _rwkv_prepare(x, prev, c, t_last, masks, mu_ref, rvec_ref, loraw_ref, g2_ref):
    row_p, col_p, grp_p, lower_p, strict_p, tri = masks
    masked = t_last < c
    gc4 = GROUP * c

    def vmask(a):
        if not masked:
            return a
        return jnp.where(lax.broadcasted_iota(jnp.int32, a.shape, 0) < t_last, a, 0.0)

    rw = x[:, COL_RWKV:COL_BA]
    rid = lax.broadcasted_iota(jnp.int32, (c, RWKV_COLS), 0)
    prev_row = _shift_rows(rw, prev[:, COL_RWKV:COL_BA], 1, rid)
    rs = rw + (prev_row - rw) * mu_ref[...]
    o1_, o2_, o3_ = RWKV_WIDTH, 2 * RWKV_WIDTH, 3 * RWKV_WIDTH
    r = rs[:, 0:o1_]
    kr = rs[:, o1_:o2_]
    vr = rs[:, o2_:o3_]
    la = rs[:, o3_:o3_ + LANES]
    gl = rs[:, o3_ + LANES:o3_ + 2 * LANES]
    lane1 = lax.broadcasted_iota(jnp.int32, (c, LANES), 1)
    wa = _mm(jnp.where(lane1 < DECAY_LORA, jnp.tanh(la), la), loraw_ref[...])
    logw = vmask(-math.exp(-0.5) * _sigmoid(rvec_ref[0:1, :] + wa[:, 0:RWKV_WIDTH]))
    a = _sigmoid(rvec_ref[1:2, :] + wa[:, RWKV_WIDTH:2 * RWKV_WIDTH])
    gate = _mm(_sigmoid(gl), g2_ref[...])
    kkr = kr * rvec_ref[2:3, :]
    kk = vmask(kkr * lax.rsqrt(_head_sum(kkr * kkr, RWKV_HEAD_DIM) + 1e-6))
    kr2 = vmask(kr * (1.0 + (a - 1.0) * rvec_ref[3:4, :]))
    gcum = _mm_f32(tri, logw)
    e_pos = jnp.exp(gcum)
    e_neg = jnp.exp(-gcum)
    a_t = -kk * jnp.exp(gcum - logw)
    b_t = kk * a * e_neg
    k_t = kr2 * e_neg
    r_t = r * e_pos
    groups = []
    for g in range(RWKV_GROUPS):
        sl = slice(g * GROUP_W, (g + 1) * GROUP_W)
        at_g, bt_g, kt_g, rt_g, v_g = a_t[:, sl], b_t[:, sl], k_t[:, sl], r_t[:, sl], vr[:, sl]
        aa = _mm_nt(jnp.concatenate([at_g, rt_g], axis=0),
                    jnp.concatenate([_block_stack(bt_g, GROUP, RWKV_HEAD_DIM),
                                     _block_stack(kt_g, GROUP, RWKV_HEAD_DIM)], axis=0))
        a_ab = jnp.where(strict_p, aa[0:c, 0:gc4], 0.0)
        a_ak = jnp.where(strict_p, aa[0:c, gc4:2 * gc4], 0.0)
        a_rb = jnp.where(lower_p, aa[c:2 * c, 0:gc4], 0.0)
        a_rk = jnp.where(lower_p, aa[c:2 * c, gc4:2 * gc4], 0.0)
        akv = _mm(a_ak, _block_stack(v_g, GROUP, RWKV_HEAD_DIM))
        groups.append(dict(nil=a_ab, at=at_g, bt=bt_g, kt=kt_g, rt=rt_g, v=v_g, akv=akv,
                           a_r=jnp.concatenate([a_rb, a_rk], axis=1), e_last=e_pos[c - 1:c, sl]))
    bonus = _head_sum(r * kr2 * rvec_ref[4:5, :], RWKV_HEAD_DIM) * vr
    return dict(groups=groups, gate=gate, bonus=bonus)


def _rwkv_finish(p, tinvs, c, sr_scr, j, rvec_ref):
    bd_mask = (lax.broadcasted_iota(jnp.int32, (GROUP_W, GROUP_W), 0) // RWKV_HEAD_DIM
               == lax.broadcasted_iota(jnp.int32, (GROUP_W, GROUP_W), 1) // RWKV_HEAD_DIM)
    outs = []
    for g in range(RWKV_GROUPS):
        q = p["groups"][g]
        wu0 = _mm(tinvs[g], jnp.concatenate([_block_stack(q["at"], GROUP, RWKV_HEAD_DIM),
                                            _block_stack(q["akv"], GROUP, RWKV_HEAD_DIM)], axis=1))
        s_g = sr_scr[j, g]
        wr = _mm_nt(jnp.concatenate([wu0[:, 0:GROUP_W], q["rt"]], axis=0), s_g)
        u = wr[0:c] + wu0[:, GROUP_W:2 * GROUP_W]
        o_g = wr[c:2 * c] + _mm(q["a_r"], jnp.concatenate([_block_stack(u, GROUP, RWKV_HEAD_DIM),
                                                           _block_stack(q["v"], GROUP, RWKV_HEAD_DIM)], axis=0))
        upd = _mm_tn(jnp.concatenate([u, q["v"]], axis=0), jnp.concatenate([q["bt"], q["kt"]], axis=0))
        sr_scr[j, g] = (s_g + jnp.where(bd_mask, upd, 0.0)) * q["e_last"]
        outs.append(o_g)
    o_r = jnp.concatenate(outs, axis=1)
    mean_o = _head_sum(o_r, RWKV_HEAD_DIM) * (1.0 / RWKV_HEAD_DIM)
    dev = o_r - mean_o
    var_o = _head_sum(dev * dev, RWKV_HEAD_DIM) * (1.0 / RWKV_HEAD_DIM)
    o_n = dev * lax.rsqrt(var_o + RWKV_GN_EPS) * rvec_ref[5:6, :] + rvec_ref[6:7, :]
    return (o_n + p["bonus"]) * p["gate"]


def _mixer_kernel(c, rows_in, t_out, t_last, n_chunks, nb,
                  proj_ref, cinit_ref, sinit_ref, sg0_ref, sr0_ref,
                  convw_ref, gvec_ref, mu_ref, rvec_ref, loraw_ref, g2_ref,
                  mix_ref, convo_ref, shifto_ref, sgo_ref, sro_ref,
                  prev_scr, sg_scr, sr_scr):
    ci = pl.program_id(1)
    gc4 = GROUP * c
    hd = RWKV_HEAD_DIM

    @pl.when(ci == 0)
    def _():
        prev_scr[...] = jnp.zeros(prev_scr.shape, F32)
        for j in range(nb):
            prev_scr[j, c - SUBLANES:c, 0:GDN_QKV] = cinit_ref[j]
            prev_scr[j, c - 1:c, COL_RWKV:COL_BA] = sinit_ref[j]
            for g in range(RWKV_GROUPS):
                rows = []
                for h in range(GROUP):
                    parts = [sr0_ref[j, g * GROUP + h]]
                    if h > 0:
                        parts.insert(0, jnp.zeros((hd, h * hd), F32))
                    if h < GROUP - 1:
                        parts.append(jnp.zeros((hd, (GROUP - 1 - h) * hd), F32))
                    rows.append(jnp.concatenate(parts, axis=1))
                sr_scr[j, g] = jnp.concatenate(rows, axis=0)
        sg_scr[...] = sg0_ref[...]

    row_p = lax.broadcasted_iota(jnp.int32, (c, gc4), 0)
    col_p = lax.broadcasted_iota(jnp.int32, (c, gc4), 1) % c
    grp_p = lax.broadcasted_iota(jnp.int32, (c, gc4), 1) // c
    tri = jnp.where(lax.broadcasted_iota(jnp.int32, (c, c), 1) <= lax.broadcasted_iota(jnp.int32, (c, c), 0),
                    1.0, 0.0)
    masks = (row_p, col_p, grp_p, col_p <= row_p, col_p < row_p, tri)

    xs, gdn, rwkv = [], [], []
    for j in range(nb):
        x = proj_ref[j]
        if rows_in < c:
            x = jnp.concatenate([x, jnp.zeros((c - rows_in, PROJ_COLS), F32)], axis=0)
        prev = prev_scr[j]
        xs.append(x)
        gdn.append(_gdn_prepare(x, prev, c, t_last, masks, convw_ref, gvec_ref))
        rwkv.append(_rwkv_prepare(x, prev, c, t_last, masks, mu_ref, rvec_ref, loraw_ref, g2_ref))

    nils = []
    for j in range(nb):
        nils.append(gdn[j]["nil"])
        nils.extend(q["nil"] for q in rwkv[j]["groups"])
    tinvs = _inv_series(nils, c, GROUP)
    per = 1 + RWKV_GROUPS

    for j in range(nb):
        o_gdn = _gdn_finish(gdn[j], tinvs[j * per], xs[j], c, sg_scr, j, gvec_ref)
        o_rwkv = _rwkv_finish(rwkv[j], tinvs[j * per + 1:(j + 1) * per], c, sr_scr, j, rvec_ref)
        mix = jnp.concatenate(o_gdn + [o_rwkv], axis=1)
        mix_ref[j] = mix[0:t_out].astype(mix_ref.dtype)
        prev_scr[j] = xs[j][:, 0:COL_BA]

    @pl.when(ci == n_chunks - 1)
    def _():
        for j in range(nb):
            convo_ref[j] = xs[j][t_last - (CONV_K - 1):t_last, 0:GDN_QKV]
            shifto_ref[j] = xs[j][t_last - 1:t_last, COL_RWKV:COL_BA]
            for g in range(RWKV_GROUPS):
                s_g = sr_scr[j, g]
                for h in range(GROUP):
                    sro_ref[j, g * GROUP + h] = s_g[h * hd:(h + 1) * hd, h * hd:(h + 1) * hd]
        sgo_ref[...] = sg_scr[...]


def _mixer(proj3d, seq_len, conv_init8, shift_init, s_gdn, s_rwkv, mp, mix_dtype):
    bsz, rows, _ = proj3d.shape
    nb = SEQS_PER_STEP
    assert bsz % nb == 0
    if rows >= CHUNK:
        assert rows == seq_len and seq_len % CHUNK == 0
        c = CHUNK
        n_chunks, rows_in, t_out, t_last = seq_len // c, c, c, c
    else:
        c = rows
        n_chunks, rows_in, t_out, t_last = 1, rows, seq_len, seq_len
    assert t_last >= CONV_K - 1 and c % SUBLANES == 0
    const2 = lambda b, i: (0, 0)
    perb3 = lambda b, i: (b, 0, 0)
    perb4 = lambda b, i: (b, 0, 0, 0)
    kern = functools.partial(_mixer_kernel, c, rows_in, t_out, t_last, n_chunks, nb)
    state_g = (nb, GDN_HEADS, GDN_HEAD_DIM, GDN_HEAD_DIM)
    state_r = (nb, RWKV_HEADS, RWKV_HEAD_DIM, RWKV_HEAD_DIM)
    return pl.pallas_call(
        kern,
        grid=(bsz // nb, n_chunks),
        in_specs=[pl.BlockSpec((nb, rows_in, PROJ_COLS), lambda b, i: (b, i, 0)),
                  pl.BlockSpec((nb, SUBLANES, GDN_QKV), perb3),
                  pl.BlockSpec((nb, 1, RWKV_COLS), perb3),
                  pl.BlockSpec(state_g, perb4),
                  pl.BlockSpec(state_r, perb4),
                  pl.BlockSpec((CONV_K, GDN_QKV), const2),
                  pl.BlockSpec((SUBLANES, LANES), const2),
                  pl.BlockSpec((1, RWKV_COLS), const2),
                  pl.BlockSpec((SUBLANES, RWKV_WIDTH), const2),
                  pl.BlockSpec((LANES, 2 * RWKV_WIDTH), const2),
                  pl.BlockSpec((GATE_LORA, RWKV_WIDTH), const2)],
        out_specs=[pl.BlockSpec((nb, t_out, D_MODEL), lambda b, i: (b, i, 0)),
                   pl.BlockSpec((nb, CONV_K - 1, GDN_QKV), perb3),
                   pl.BlockSpec((nb, 1, RWKV_COLS), perb3),
                   pl.BlockSpec(state_g, perb4),
                   pl.BlockSpec(state_r, perb4)],
        out_shape=[jax.ShapeDtypeStruct((bsz, seq_len, D_MODEL), mix_dtype),
                   jax.ShapeDtypeStruct((bsz, CONV_K - 1, GDN_QKV), F32),
                   jax.ShapeDtypeStruct((bsz, 1, RWKV_COLS), F32),
                   jax.ShapeDtypeStruct((bsz,) + state_g[1:], F32),
                   jax.ShapeDtypeStruct((bsz,) + state_r[1:], F32)],
        scratch_shapes=[pltpu.VMEM((nb, c, COL_BA), F32),
                        pltpu.VMEM(state_g, F32),
                        pltpu.VMEM((nb, RWKV_GROUPS, GROUP_W, GROUP_W), F32)],
        compiler_params=pltpu.CompilerParams(dimension_semantics=("parallel", "arbitrary"),
                                             vmem_limit_bytes=VMEM_LIMIT_BYTES),
        name="sequence_mixers",
    )(proj3d, conv_init8, shift_init, s_gdn, s_rwkv,
      mp["conv_w"], mp["gvec"], mp["mu"], mp["rvec"], mp["lora_w"], mp["g2"])


def _layernorm(x, g, b):
    mu = jnp.mean(x, axis=-1, keepdims=True)
    d = x - mu
    var = jnp.mean(d * d, axis=-1, keepdims=True)
    return d * lax.rsqrt(var + LN_EPS) * g + b


def _post_mixer_kernel(n_first, mixp_ref, mixs_ref, xp_ref, xs_ref, wo_ref, lnv_ref, wrt_ref, br_ref,
                       h_ref, idx_ref, rank_ref, gate_ref, cnt_ref, base_scr, hp_scr):
    i = pl.program_id(0)
    tt = h_ref.shape[0]

    @pl.when(i == 0)
    def _():
        base_scr[...] = jnp.zeros(base_scr.shape, F32)

    @pl.when(i < n_first)
    def _():
        hp_scr[...] = ALPHA * xp_ref[...] + jnp.dot(mixp_ref[...].astype(BF16), wo_ref[...],
                                                    preferred_element_type=F32)

    @pl.when(i >= n_first)
    def _():
        hp_scr[...] = ALPHA * xs_ref[...] + jnp.dot(mixs_ref[...].astype(BF16), wo_ref[...],
                                                    preferred_element_type=F32)

    h = _layernorm(hp_scr[...], lnv_ref[0:1, :], lnv_ref[1:2, :])
    h_ref[...] = h
    lt = lax.dot_general(wrt_ref[...], h, (((1,), (1,)), ((), ())),
                         precision=lax.Precision.HIGHEST, preferred_element_type=F32) + br_ref[...]
    eid = lax.broadcasted_iota(jnp.int32, lt.shape, 0)
    lt = jnp.where(eid < N_EXPERTS, lt, -jnp.inf)
    upper = jnp.where(lax.broadcasted_iota(jnp.int32, (tt, tt), 0) < lax.broadcasted_iota(jnp.int32, (tt, tt), 1),
                      1.0, 0.0).astype(BF16)
    base = base_scr[...]
    vals, idxs, ranks = [], [], []
    for _ in range(TOP_K):
        m = jnp.max(lt, axis=0, keepdims=True)
        sel = jnp.min(jnp.where(lt == m, eid, LANES), axis=0, keepdims=True)
        onehot = eid == sel
        lt = jnp.where(onehot, -jnp.inf, lt)
        oh = jnp.where(onehot, 1.0, 0.0)
        before = jnp.dot(oh.astype(BF16), upper, preferred_element_type=F32)
        ranks.append(jnp.sum(oh * (base + before), axis=0, keepdims=True))
        base = base + jnp.sum(oh, axis=1, keepdims=True)
        vals.append(m)
        idxs.append(sel)
    base_scr[...] = base
    ex = [jnp.exp(v - vals[0]) for v in vals]
    den = ex[0] + ex[1] + ex[2] + ex[3]
    pad_i = jnp.zeros((SUBLANES - TOP_K, tt), jnp.int32)
    idx_ref[...] = jnp.concatenate(idxs + [pad_i], axis=0)
    rank_ref[...] = jnp.concatenate([rk.astype(jnp.int32) for rk in ranks] + [pad_i], axis=0)
    gates = jnp.concatenate([e / den for e in ex] + [jnp.zeros((SUBLANES - TOP_K, tt), F32)], axis=0)
    gate_ref[...] = gates.T

    @pl.when(i == pl.num_programs(0) - 1)
    def _():
        cnt_ref[...] = base[:, 0:LANES].astype(jnp.int32)


def _post_mixer(mix_p, mix_s, x_p, x_s, w_o_bf16, ln1, w_router_t, b_router_col):
    tt = TOK_TILE
    n_p, n_s = x_p.shape[0], x_s.shape[0]
    assert n_p % tt == 0 and n_s % tt == 0
    n1, n2 = n_p // tt, n_s // tt
    n_tok = n_p + n_s
    const2 = lambda i: (0, 0)
    first = lambda i: (jnp.minimum(i, n1 - 1), 0)
    second = lambda i: (jnp.maximum(i - n1, 0), 0)
    return pl.pallas_call(
        functools.partial(_post_mixer_kernel, n1),
        grid=(n1 + n2,),
        in_specs=[pl.BlockSpec((tt, D_MODEL), first),
                  pl.BlockSpec((tt, D_MODEL), second),
                  pl.BlockSpec((tt, D_MODEL), first),
                  pl.BlockSpec((tt, D_MODEL), second),
                  pl.BlockSpec((D_MODEL, D_MODEL), const2),
                  pl.BlockSpec((SUBLANES, D_MODEL), const2),
                  pl.BlockSpec((LANES, D_MODEL), const2),
                  pl.BlockSpec((LANES, 1), const2)],
        out_specs=[pl.BlockSpec((tt, D_MODEL), lambda i: (i, 0)),
                   pl.BlockSpec((SUBLANES, tt), lambda i: (0, i)),
                   pl.BlockSpec((SUBLANES, tt), lambda i: (0, i)),
                   pl.BlockSpec((tt, SUBLANES), lambda i: (i, 0)),
                   pl.BlockSpec((LANES, LANES), const2)],
        out_shape=[jax.ShapeDtypeStruct((n_tok, D_MODEL), F32),
                   jax.ShapeDtypeStruct((SUBLANES, n_tok), jnp.int32),
                   jax.ShapeDtypeStruct((SUBLANES, n_tok), jnp.int32),
                   jax.ShapeDtypeStruct((n_tok, SUBLANES), F32),
                   jax.ShapeDtypeStruct((LANES, LANES), jnp.int32)],
        scratch_shapes=[pltpu.VMEM((LANES, tt), F32), pltpu.VMEM((tt, D_MODEL), F32)],
        compiler_params=pltpu.CompilerParams(dimension_semantics=("arbitrary",),
                                             vmem_limit_bytes=VMEM_LIMIT_BYTES),
        name="outproj_norm_router",
    )(mix_p, mix_s, x_p, x_s, w_o_bf16, ln1, w_router_t, b_router_col)


def _slot_kernel(pstart_ref, idx_ref, rank_ref, dest_ref):
    idx = idx_ref[...]
    dest = rank_ref[...]
    for e in range(N_EXPERTS):
        dest = dest + jnp.where(idx == e, pstart_ref[e], 0)
    dest_ref[...] = dest


def _slots(idx, rank, pstart):
    grid_spec = pltpu.PrefetchScalarGridSpec(
        num_scalar_prefetch=1,
        grid=(1,),
        in_specs=[pl.BlockSpec(idx.shape, lambda i, ps: (0, 0)),
                  pl.BlockSpec(idx.shape, lambda i, ps: (0, 0))],
        out_specs=pl.BlockSpec(idx.shape, lambda i, ps: (0, 0)),
    )
    return pl.pallas_call(
        _slot_kernel,
        grid_spec=grid_spec,
        out_shape=jax.ShapeDtypeStruct(idx.shape, jnp.int32),
        name="moe_slots",
    )(pstart, idx, rank)


def _dispatch_kernel(pend_ref, padded_ref, dest_ref, h_ref, xb_hbm, zero_scr, sem):
    i = pl.program_id(0)
    tt = h_ref.shape[0]
    tm = zero_scr.shape[0]

    @pl.when(i == 0)
    def _():
        zero_scr[...] = jnp.zeros(zero_scr.shape, F32)

        def tail_copy(e):
            start = pl.multiple_of(pend_ref[e] - tm, tm)
            return pltpu.make_async_copy(zero_scr, xb_hbm.at[pl.ds(start, tm), :], sem)

        for e in range(N_EXPERTS):
            @pl.when(padded_ref[e] > 0)
            def _():
                tail_copy(e).start()
        for e in range(N_EXPERTS):
            @pl.when(padded_ref[e] > 0)
            def _():
                tail_copy(e).wait()

        def spare_copy(b):
            return pltpu.make_async_copy(zero_scr, xb_hbm.at[pl.ds(pl.multiple_of(b * tm, tm), tm), :], sem)

        def spare_start(b, carry):
            spare_copy(b).start()
            return carry

        def spare_wait(b, carry):
            spare_copy(b).wait()
            return carry

        first_spare = pend_ref[N_EXPERTS - 1] // tm
        lax.fori_loop(first_spare, xb_hbm.shape[0] // tm, spare_start, 0)
        lax.fori_loop(first_spare, xb_hbm.shape[0] // tm, spare_wait, 0)

    def row_copy(t, k):
        return pltpu.make_async_copy(h_ref.at[pl.ds(t, 1), :], xb_hbm.at[pl.ds(dest_ref[k, t], 1), :], sem)

    def issue(t, carry):
        for k in range(TOP_K):
            row_copy(t, k).start()
        return carry

    lax.fori_loop(0, tt, issue, 0, unroll=DMA_UNROLL)

    def drain(t, carry):
        for k in range(TOP_K):
            row_copy(t, k).wait()
        return carry

    lax.fori_loop(0, tt, drain, 0, unroll=DMA_UNROLL)


def _dispatch(h2d, dest, pend, padded, n_rows):
    n_tok = h2d.shape[0]
    tt = DISPATCH_TILE
    assert n_tok % tt == 0
    grid_spec = pltpu.PrefetchScalarGridSpec(
        num_scalar_prefetch=2,
        grid=(n_tok // tt,),
        in_specs=[pl.BlockSpec((SUBLANES, tt), lambda i, pe, pa: (0, i), memory_space=pltpu.SMEM),
                  pl.BlockSpec((tt, D_MODEL), lambda i, pe, pa: (i, 0))],
        out_specs=pl.BlockSpec(memory_space=pl.ANY),
        scratch_shapes=[pltpu.VMEM((MOE_TILE, D_MODEL), F32), pltpu.SemaphoreType.DMA(())],
    )
    return pl.pallas_call(
        _dispatch_kernel,
        grid_spec=grid_spec,
        out_shape=jax.ShapeDtypeStruct((n_rows, D_MODEL), F32),
        compiler_params=pltpu.CompilerParams(dimension_semantics=("arbitrary",)),
        name="moe_dispatch",
    )(pend, padded, dest, h2d)


def _expert_kernel(be_ref, nused_ref, x_ref, wg_ref, wu_ref, wd_ref, bg_ref, bu_ref, bd_ref, y_ref,
                   wg16, wu16, wd16):
    i = pl.program_id(0)
    prev_e = be_ref[jnp.maximum(i - 1, 0)]
    fresh = jnp.logical_or(i == 0, be_ref[i] != prev_e)
    used = i < nused_ref[0]

    @pl.when(jnp.logical_and(fresh, used))
    def _():
        wg16[...] = wg_ref[0].astype(BF16)
        wu16[...] = wu_ref[0].astype(BF16)
        wd16[...] = wd_ref[0].astype(BF16)

    @pl.when(used)
    def _():
        x16 = x_ref[...].astype(BF16)
        gt = jnp.dot(x16, wg16[...], preferred_element_type=F32) + bg_ref[0]
        up = jnp.dot(x16, wu16[...], preferred_element_type=F32) + bu_ref[0]
        gt = jnp.minimum(gt, SWIGLU_LIMIT)
        up = jnp.clip(up, -SWIGLU_LIMIT, SWIGLU_LIMIT)
        hid = (up + 1.0) * gt * _sigmoid(SWIGLU_ALPHA * gt)
        y_ref[...] = jnp.dot(hid.astype(BF16), wd16[...], preferred_element_type=F32) + bd_ref[0]

    @pl.when(jnp.logical_not(used))
    def _():
        y_ref[...] = jnp.zeros(y_ref.shape, F32)


def _experts(xb, block_expert, n_used, w_gate, b_gate, w_up, b_up, w_down, b_down):
    n_rows = xb.shape[0]
    tm = MOE_TILE
    n_blocks = n_rows // tm
    d_e = w_gate.shape[2]
    wspec = lambda shape: pl.BlockSpec(shape, lambda i, be, nu: (be[i], 0, 0))
    grid_spec = pltpu.PrefetchScalarGridSpec(
        num_scalar_prefetch=2,
        grid=(n_blocks,),
        in_specs=[pl.BlockSpec((tm, D_MODEL), lambda i, be, nu: (jnp.minimum(i, nu[0] - 1), 0)),
                  wspec((1, D_MODEL, d_e)), wspec((1, D_MODEL, d_e)), wspec((1, d_e, D_MODEL)),
                  wspec((1, 1, d_e)), wspec((1, 1, d_e)), wspec((1, 1, D_MODEL))],
        out_specs=pl.BlockSpec((tm, D_MODEL), lambda i, be, nu: (i, 0)),
        scratch_shapes=[pltpu.VMEM((D_MODEL, d_e), BF16), pltpu.VMEM((D_MODEL, d_e), BF16),
                        pltpu.VMEM((d_e, D_MODEL), BF16)],
    )
    return pl.pallas_call(
        _expert_kernel,
        grid_spec=grid_spec,
        out_shape=jax.ShapeDtypeStruct((n_rows, D_MODEL), F32),
        compiler_params=pltpu.CompilerParams(dimension_semantics=("arbitrary",),
                                             vmem_limit_bytes=VMEM_LIMIT_BYTES),
        name="moe_experts",
    )(block_expert, n_used, xb, w_gate, w_up, w_down,
      b_gate[:, None, :], b_up[:, None, :], b_down[:, None, :])


def _combine_kernel(n_first, dest_ref, h_ref, gate_ref, lnv_ref, yb_hbm, yp_ref, ys_ref, buf, sem):
    i = pl.program_id(0)
    tt = h_ref.shape[0]

    def row_copy(t, k):
        return pltpu.make_async_copy(yb_hbm.at[pl.ds(dest_ref[k, t], 1), :], buf.at[k, pl.ds(t, 1), :], sem)

    def issue(t, carry):
        for k in range(TOP_K):
            row_copy(t, k).start()
        return carry

    lax.fori_loop(0, tt, issue, 0, unroll=DMA_UNROLL)

    def drain(t, carry):
        for k in range(TOP_K):
            row_copy(t, k).wait()
        return carry

    lax.fori_loop(0, tt, drain, 0, unroll=DMA_UNROLL)
    gates = gate_ref[...]
    f = buf[0] * gates[:, 0:1]
    for k in range(1, TOP_K):
        f = f + buf[k] * gates[:, k:k + 1]
    y = _layernorm(ALPHA * h_ref[...] + f, lnv_ref[0:1, :], lnv_ref[1:2, :])

    @pl.when(i < n_first)
    def _():
        yp_ref[...] = y

    @pl.when(i >= n_first)
    def _():
        ys_ref[...] = y


def _combine(yb, h2d, dest, gates, ln2, n_p):
    n_tok = h2d.shape[0]
    tt = COMBINE_TILE
    n_s = n_tok - n_p
    assert n_p % tt == 0 and n_s % tt == 0
    n1, n2 = n_p // tt, n_s // tt
    return pl.pallas_call(
        functools.partial(_combine_kernel, n1),
        grid=(n1 + n2,),
        in_specs=[pl.BlockSpec((SUBLANES, tt), lambda i: (0, i), memory_space=pltpu.SMEM),
                  pl.BlockSpec((tt, D_MODEL), lambda i: (i, 0)),
                  pl.BlockSpec((tt, SUBLANES), lambda i: (i, 0)),
                  pl.BlockSpec((SUBLANES, D_MODEL), lambda i: (0, 0)),
                  pl.BlockSpec(memory_space=pl.ANY)],
        out_specs=[pl.BlockSpec((tt, D_MODEL), lambda i: (jnp.minimum(i, n1 - 1), 0)),
                   pl.BlockSpec((tt, D_MODEL), lambda i: (jnp.maximum(i - n1, 0), 0))],
        out_shape=[jax.ShapeDtypeStruct((n_p, D_MODEL), F32),
                   jax.ShapeDtypeStruct((n_s, D_MODEL), F32)],
        scratch_shapes=[pltpu.VMEM((TOP_K, tt, D_MODEL), F32), pltpu.SemaphoreType.DMA(())],
        compiler_params=pltpu.CompilerParams(dimension_semantics=("arbitrary",),
                                             vmem_limit_bytes=VMEM_LIMIT_BYTES),
        name="moe_combine_norm",
    )(dest, h2d, gates, ln2, yb)


def _pad_rows(v, rows):
    return jnp.concatenate([v, jnp.zeros((rows - v.shape[0],) + v.shape[1:], v.dtype)], axis=0)


def _mixer_params(conv_w, a_log, dt_bias, gdn_norm_w, mu_shift, w0, w2, a0, a2, g2, k_k, k_a, r_k, lnx_w, lnx_b):
    gvec = jnp.zeros((SUBLANES, LANES), F32)
    gvec = gvec.at[0, GDN_HEADS:2 * GDN_HEADS].set(a_log).at[1, GDN_HEADS:2 * GDN_HEADS].set(dt_bias)
    gvec = gvec.at[2, :].set(gdn_norm_w)
    rvec = _pad_rows(jnp.stack([w0, a0, k_k, k_a, r_k.reshape(-1), lnx_w, lnx_b]), SUBLANES)
    lora_w = jnp.zeros((LANES, 2 * RWKV_WIDTH), F32)
    lora_w = lora_w.at[0:DECAY_LORA, 0:RWKV_WIDTH].set(w2).at[DECAY_LORA:, RWKV_WIDTH:].set(a2)
    return dict(conv_w=conv_w, gvec=gvec, mu=mu_shift[None, :], rvec=rvec,
                lora_w=lora_w.astype(BF16), g2=g2.astype(BF16))


def _layer(x_prompt, x_sample, state_conv, state_shift, state_gdn, state_rwkv,
           w_in, mixer_params, w_o, ln1_g, ln1_b, w_router, b_router,
           w_gate, b_gate, w_up, b_up, w_down, b_down, ln2_g, ln2_b):
    bp, tp, d = x_prompt.shape
    bs, ts, _ = x_sample.shape
    n_p, n_s = bp * tp, bs * ts
    n_tok = n_p + n_s
    xp2d = x_prompt.reshape(n_p, d)
    xs2d = x_sample.reshape(n_s, d)

    in_cols = w_in.shape[1]
    off_ba = COL_Z + GDN_WIDTH
    w_in_r = jnp.concatenate([w_in[:, :off_ba], w_in[:, off_ba + 2 * GDN_HEADS:],
                              w_in[:, off_ba:off_ba + 2 * GDN_HEADS],
                              jnp.zeros((d, PROJ_COLS - in_cols), w_in.dtype)], axis=1).astype(BF16)

    rows_s = -(-ts // SUBLANES) * SUBLANES
    proj_p = _input_projection(xp2d, w_in_r).reshape(bp, tp, PROJ_COLS)
    proj_s = _input_projection(xs2d, w_in_r).reshape(bs, ts, PROJ_COLS)
    if rows_s != ts:
        proj_s = jnp.pad(proj_s, ((0, 0), (0, rows_s - ts), (0, 0)))
    zeros_p = (jnp.zeros((bp, SUBLANES, GDN_QKV), F32), jnp.zeros((bp, 1, RWKV_COLS), F32),
               jnp.zeros((bp, GDN_HEADS, GDN_HEAD_DIM, GDN_HEAD_DIM), F32),
               jnp.zeros((bp, RWKV_HEADS, RWKV_HEAD_DIM, RWKV_HEAD_DIM), F32))
    mix_p, conv_p, shift_p, gdn_p, rwkv_p = _mixer(proj_p, tp, *zeros_p, mixer_params, BF16)
    conv8 = jnp.pad(state_conv, ((0, 0), (SUBLANES - (CONV_K - 1), 0), (0, 0)))
    mix_s, conv_s, shift_s, gdn_s, rwkv_s = _mixer(
        proj_s, ts, conv8, state_shift[:, None, :], state_gdn, state_rwkv, mixer_params, F32)

    ln1 = _pad_rows(jnp.stack([ln1_g, ln1_b]), SUBLANES)
    ln2 = _pad_rows(jnp.stack([ln2_g, ln2_b]), SUBLANES)
    wrt = _pad_rows(w_router.T, LANES)
    brc = _pad_rows(b_router[:, None], LANES)
    h2d, idx, rank, gates, cnt = _post_mixer(mix_p.reshape(n_p, d), mix_s.reshape(n_s, d), xp2d, xs2d,
                                             w_o.astype(BF16), ln1, wrt, brc)

    counts = cnt[:N_EXPERTS, 0]
    padded = (((counts + MOE_TILE - 1) // MOE_TILE) * MOE_TILE).astype(jnp.int32)
    pend = jnp.cumsum(padded).astype(jnp.int32)
    pstart = pend - padded
    n_blocks = -(-(n_tok * TOP_K) // MOE_TILE) + N_EXPERTS
    n_rows = n_blocks * MOE_TILE
    block_expert = jnp.minimum(
        jnp.sum(pend[None, :] <= (jnp.arange(n_blocks) * MOE_TILE)[:, None], axis=1), N_EXPERTS - 1).astype(jnp.int32)
    n_used = pend[-1:] // MOE_TILE

    dest = _slots(idx, rank, pstart)
    xb = _dispatch(h2d, dest, pend, padded, n_rows)
    yb = _experts(xb, block_expert, n_used, w_gate, b_gate, w_up, b_up, w_down, b_down)
    y_p, y_s = _combine(yb, h2d, dest, gates, ln2, n_p)

    return (y_p.reshape(bp, tp, d), y_s.reshape(bs, ts, d), conv_p, shift_p[:, 0], gdn_p, rwkv_p,
            conv_s, shift_s[:, 0], gdn_s, rwkv_s)


def kernel(x_prompt, x_sample, state_conv, state_shift, state_gdn, state_rwkv, w_in, conv_w, a_log, dt_bias,
           gdn_norm_w, mu_shift, w0, w2, a0, a2, g2, k_k, k_a, r_k, lnx_w, lnx_b, w_o, ln1_g, ln1_b,
           w_router, b_router, w_gate, b_gate, w_up, b_up, w_down, b_down, ln2_g, ln2_b):
    mp = _mixer_params(conv_w, a_log, dt_bias, gdn_norm_w, mu_shift, w0, w2, a0, a2, g2, k_k, k_a, r_k,
                       lnx_w, lnx_b)
    return _layer(x_prompt, x_sample, state_conv, state_shift, state_gdn, state_rwkv,
                  w_in, mp, w_o, ln1_g, ln1_b, w_router, b_router,
                  w_gate, b_gate, w_up, b_up, w_down, b_down, ln2_g, ln2_b)
```

```python
import functools
import math

import jax
import jax.numpy as jnp
from jax import lax
from jax.experimental import pallas as pl
from jax.experimental.pallas import tpu as pltpu

F32 = jnp.float32
BF16 = jnp.bfloat16

D_MODEL = 1024
GDN_HEADS = 4
GDN_HEAD_DIM = 128
GDN_WIDTH = GDN_HEADS * GDN_HEAD_DIM
GDN_QKV = 3 * GDN_WIDTH
CONV_K = 4
RWKV_HEADS = 8
RWKV_HEAD_DIM = 64
RWKV_WIDTH = RWKV_HEADS * RWKV_HEAD_DIM
DECAY_LORA = 64
AAA_LORA = 64
GATE_LORA = 128
RWKV_COLS = 3 * RWKV_WIDTH + DECAY_LORA + AAA_LORA + GATE_LORA
RWKV_GN_EPS = RWKV_HEAD_DIM * 1e-5
N_EXPERTS = 32
TOP_K = 4
SWIGLU_LIMIT = 7.0
SWIGLU_ALPHA = 1.702
DEPTH = 1
ALPHA = (2.0 * DEPTH) ** 0.25
LN_EPS = 1e-5

LANES = 128
SUBLANES = 8
VMEM_LIMIT_BYTES = 56 * 1024 * 1024

COL_Z = GDN_QKV
COL_RWKV = COL_Z + GDN_WIDTH
COL_BA = COL_RWKV + RWKV_COLS
PROJ_COLS = COL_BA + LANES

CHUNK = 64
GROUP = 4
RWKV_GROUPS = RWKV_HEADS // GROUP
GROUP_W = GROUP * RWKV_HEAD_DIM
SEQS_PER_STEP_LONG = 4
SEQS_PER_STEP_SHORT = 8
PROJ_TILE = 256
TOK_TILE = 256
MOE_TILE = 512
DISPATCH_TILE = 256
COMBINE_TILE = 128
DMA_UNROLL = 8


class _Seqs:
    def __init__(self, vals):
        self.v = list(vals)

    def __getitem__(self, idx):
        return _Seqs([a[idx] for a in self.v])

    def __add__(self, o):
        return _lift(lambda a, b: a + b)(self, o)

    def __radd__(self, o):
        return _lift(lambda a, b: b + a)(self, o)

    def __sub__(self, o):
        return _lift(lambda a, b: a - b)(self, o)

    def __rsub__(self, o):
        return _lift(lambda a, b: b - a)(self, o)

    def __mul__(self, o):
        return _lift(lambda a, b: a * b)(self, o)

    def __rmul__(self, o):
        return _lift(lambda a, b: b * a)(self, o)

    def __neg__(self):
        return _Seqs([-a for a in self.v])

    @property
    def shape(self):
        return self.v[0].shape

    @property
    def T(self):
        return _Seqs([a.T for a in self.v])

    def astype(self, dt):
        return _Seqs([a.astype(dt) for a in self.v])


def _lift(f):
    def g(*args, **kw):
        n = next((len(a.v) for a in args if isinstance(a, _Seqs)), None)
        if n is None:
            return f(*args, **kw)
        return _Seqs([f(*[a.v[i] if isinstance(a, _Seqs) else a for a in args], **kw) for i in range(n)])
    return g


def _cat(parts, axis):
    n = next((len(a.v) for a in parts if isinstance(a, _Seqs)), None)
    if n is None:
        return jnp.concatenate(parts, axis=axis)
    return _Seqs([jnp.concatenate([a.v[i] if isinstance(a, _Seqs) else a for a in parts], axis=axis)
                  for i in range(n)])


_exp = _lift(jnp.exp)
_tanh = _lift(jnp.tanh)
_where = _lift(jnp.where)
_rsqrt = _lift(lax.rsqrt)
_sum = _lift(jnp.sum)
_roll = _lift(pltpu.roll)
_bcast = _lift(jnp.broadcast_to)


@_lift
def _mm(a, b):
    return jnp.dot(a.astype(BF16), b.astype(BF16), preferred_element_type=F32)


@_lift
def _mm_nt(a, b):
    return lax.dot_general(a.astype(BF16), b.astype(BF16), (((1,), (1,)), ((), ())), preferred_element_type=F32)


def _mm_tn(a, b):
    return _mm(a.T, b)


@_lift
def _mm_f32(a, b):
    return jnp.dot(a, b, precision=lax.Precision.HIGHEST, preferred_element_type=F32)


def _sigmoid(x):
    return 0.5 * _tanh(0.5 * x) + 0.5


@_lift
def _softplus(x):
    return jnp.maximum(x, 0.0) + jnp.log1p(jnp.exp(-jnp.abs(x)))


def _proj_kernel(x_ref, w_ref, o_ref):
    o_ref[...] = jnp.dot(x_ref[...].astype(BF16), w_ref[...], preferred_element_type=F32)


def _input_projection(x2d, w_in_bf16):
    n_tok = x2d.shape[0]
    assert n_tok % PROJ_TILE == 0
    return pl.pallas_call(
        _proj_kernel,
        grid=(n_tok // PROJ_TILE,),
        in_specs=[pl.BlockSpec((PROJ_TILE, D_MODEL), lambda i: (i, 0)),
                  pl.BlockSpec((D_MODEL, PROJ_COLS), lambda i: (0, 0))],
        out_specs=pl.BlockSpec((PROJ_TILE, PROJ_COLS), lambda i: (i, 0)),
        out_shape=jax.ShapeDtypeStruct((n_tok, PROJ_COLS), F32),
        compiler_params=pltpu.CompilerParams(dimension_semantics=("parallel",),
                                             vmem_limit_bytes=VMEM_LIMIT_BYTES),
        name="input_projection",
    )(x2d, w_in_bf16)


def _shift_rows(cur, prev, s, row_ids):
    return _where(row_ids < s, _roll(prev, s, axis=0), _roll(cur, s, axis=0))


@_lift
def _block_stack(x, n, width):
    grp = lax.broadcasted_iota(jnp.int32, x.shape, 1) // width
    if x.shape[0] % (2 * SUBLANES) == 0:
        x16 = x.astype(BF16)
        return jnp.concatenate([x16 * jnp.where(grp == i, 1.0, 0.0).astype(BF16) for i in range(n)], axis=0)
    return jnp.concatenate([jnp.where(grp == i, x, 0.0) for i in range(n)], axis=0)


def _inv_series(nils, c, n):
    row = lax.broadcasted_iota(jnp.int32, nils[0].shape, 0)
    col = lax.broadcasted_iota(jnp.int32, nils[0].shape, 1) % c
    eye = jnp.where(row == col, 1.0, 0.0)
    xs = [eye + nil for nil in nils]
    qs = [_mm(nil, _block_stack(nil, n, c)) for nil in nils]
    levels = int(math.log2(c))
    for lvl in range(1, levels):
        nxt_q, nxt_x = [], []
        for q, x in zip(qs, xs):
            bd = _block_stack(q, n, c)
            if lvl == levels - 1:
                nxt_x.append(x + _mm(x, bd))
            else:
                both = _mm(_cat([q, x], 0), bd)
                nxt_q.append(both[0:c])
                nxt_x.append(x + both[c:2 * c])
        qs, xs = nxt_q, nxt_x
    return xs


def _head_sum(x, width):
    pieces = []
    for j in range(x.shape[1] // LANES):
        xb = x[:, j * LANES:(j + 1) * LANES]
        if width == LANES:
            pieces.append(_bcast(_sum(xb, axis=-1, keepdims=True), xb.shape))
        else:
            lo = lax.broadcasted_iota(jnp.int32, xb.shape, 1) < width
            s0 = _sum(_where(lo, xb, 0.0), axis=-1, keepdims=True)
            s1 = _sum(_where(lo, 0.0, xb), axis=-1, keepdims=True)
            pieces.append(_where(lo, s0, s1))
    return _cat(pieces, 1)


def _gdn_head(a, h):
    return a[:, h * GDN_HEAD_DIM:(h + 1) * GDN_HEAD_DIM]


def _gdn_prepare(x, prev, c, t_last, masks, convw_ref, gvec_ref):
    row_p, col_p, grp_p, lower_p, strict_p, tri = masks
    masked = t_last < c

    def vmask(a):
        if not masked:
            return a
        return _where(lax.broadcasted_iota(jnp.int32, a.shape, 0) < t_last, a, 0.0)

    xq = x[:, 0:GDN_QKV]
    pq = prev[:, 0:GDN_QKV]
    rid = lax.broadcasted_iota(jnp.int32, (c, GDN_QKV), 0)
    conv = _shift_rows(xq, pq, 3, rid) * convw_ref[0:1, :]
    conv = conv + _shift_rows(xq, pq, 2, rid) * convw_ref[1:2, :]
    conv = conv + _shift_rows(xq, pq, 1, rid) * convw_ref[2:3, :]
    conv = conv + xq * convw_ref[3:4, :]
    conv = conv * _sigmoid(conv)

    ba = x[:, COL_BA:COL_BA + LANES]
    beta_blk = vmask(_sigmoid(ba))
    g_blk = vmask(-jnp.exp(gvec_ref[0:1, :]) * _softplus(ba + gvec_ref[1:2, :]))

    qn, kn, kb, vb = [], [], [], []
    for h in range(GDN_HEADS):
        qh = _gdn_head(conv[:, 0:GDN_WIDTH], h)
        kh = _gdn_head(conv[:, GDN_WIDTH:2 * GDN_WIDTH], h)
        vh = vmask(_gdn_head(conv[:, 2 * GDN_WIDTH:GDN_QKV], h))
        qh = qh * (_rsqrt(_sum(qh * qh, axis=-1, keepdims=True) + 1e-6) * (GDN_HEAD_DIM ** -0.5))
        kh = vmask(kh * _rsqrt(_sum(kh * kh, axis=-1, keepdims=True) + 1e-6))
        bh = beta_blk[:, h:h + 1]
        qn.append(qh), kn.append(kh), kb.append(kh * bh), vb.append(vh * bh)

    ma = _mm_nt(_cat([_cat(kb, 1), _cat(qn, 1)], 0),
                _block_stack(_cat(kn, 1), GDN_HEADS, GDN_HEAD_DIM))
    gexp = _where(grp_p == 0, g_blk[:, GDN_HEADS:GDN_HEADS + 1], 0.0)
    for h in range(1, GDN_HEADS):
        gexp = _where(grp_p == h, g_blk[:, GDN_HEADS + h:GDN_HEADS + h + 1], gexp)
    gcol = _mm_f32(tri, gexp)
    grow = _sum(_where(row_p == col_p, gcol, 0.0), axis=0, keepdims=True)
    decay = _exp(_where(lower_p, gcol - grow, -jnp.inf))
    m_p = _where(strict_p, ma[0:c] * decay, 0.0)
    attn = ma[c:2 * c] * decay
    egc, kdec, glast = [], [], []
    for h in range(GDN_HEADS):
        gch = gcol[:, h * c:h * c + 1]
        glh = gcol[c - 1:c, h * c:h * c + 1]
        egc.append(_exp(gch))
        kdec.append(kn[h] * _exp(glh - gch))
        glast.append(_exp(glh))
    rhs2 = _cat([_block_stack(_cat(vb, 1), GDN_HEADS, GDN_HEAD_DIM),
                 _block_stack(_cat([kb[h] * egc[h] for h in range(GDN_HEADS)], 1), GDN_HEADS, GDN_HEAD_DIM)], 1)
    qdec = [qn[h] * egc[h] for h in range(GDN_HEADS)]
    return dict(nil=-m_p, attn=attn, rhs2=rhs2, qdec=qdec, kdec=kdec, glast=glast)


def _gdn_finish(p, tinv, x, c, states, gvec_ref):
    uw = _mm(tinv, p["rhs2"])
    vnew, o1 = [], []
    for h in range(GDN_HEADS):
        ws = _mm(_cat([_gdn_head(uw[:, GDN_WIDTH:2 * GDN_WIDTH], h), p["qdec"][h]], 0), states[h])
        vnew.append(_gdn_head(uw[:, 0:GDN_WIDTH], h) - ws[0:c])
        o1.append(ws[c:2 * c])
    o2 = _mm(p["attn"], _block_stack(_cat(vnew, 1), GDN_HEADS, GDN_HEAD_DIM))
    out, new_states = [], []
    for h in range(GDN_HEADS):
        new_states.append(states[h] * p["glast"][h] + _mm_tn(p["kdec"][h], vnew[h]))
        oh = o1[h] + _gdn_head(o2, h)
        zh = _gdn_head(x[:, COL_Z:COL_Z + GDN_WIDTH], h)
        oh = oh * _rsqrt(_sum(oh * oh, axis=-1, keepdims=True) * (1.0 / GDN_HEAD_DIM) + 1e-6) * gvec_ref[2:3, :]
        out.append(oh * (zh * _sigmoid(zh)))
    return out, new_states


def _rwkv_prepare(x, prev, c, t_last, masks, mu_ref, rvec_ref, loraw_ref, g2_ref):
    row_p, col_p, grp_p, lower_p, strict_p, tri = masks
    masked = t_last < c
    gc4 = GROUP * c

    def vmask(a):
        if not masked:
            return a
        return _where(lax.broadcasted_iota(jnp.int32, a.shape, 0) < t_last, a, 0.0)

    rw = x[:, COL_RWKV:COL_BA]
    rid = lax.broadcasted_iota(jnp.int32, (c, RWKV_COLS), 0)
    prev_row = _shift_rows(rw, prev[:, COL_RWKV:COL_BA], 1, rid)
    rs = rw + (prev_row - rw) * mu_ref[...]
    o1_, o2_, o3_ = RWKV_WIDTH, 2 * RWKV_WIDTH, 3 * RWKV_WIDTH
    r = rs[:, 0:o1_]
    kr = rs[:, o1_:o2_]
    vr = rs[:, o2_:o3_]
    la = rs[:, o3_:o3_ + LANES]
    gl = rs[:, o3_ + LANES:o3_ + 2 * LANES]
    lane1 = lax.broadcasted_iota(jnp.int32, (c, LANES), 1)
    wa = _mm(_where(lane1 < DECAY_LORA, _tanh(la), la), loraw_ref[...])
    logw = vmask(-math.exp(-0.5) * _sigmoid(rvec_ref[0:1, :] + wa[:, 0:RWKV_WIDTH]))
    a = _sigmoid(rvec_ref[1:2, :] + wa[:, RWKV_WIDTH:2 * RWKV_WIDTH])
    gate = _mm(_sigmoid(gl), g2_ref[...])
    kkr = kr * rvec_ref[2:3, :]
    kk = vmask(kkr * _rsqrt(_head_sum(kkr * kkr, RWKV_HEAD_DIM) + 1e-6))
    kr2 = vmask(kr * (1.0 + (a - 1.0) * rvec_ref[3:4, :]))
    gcum = _mm_f32(tri, logw)
    e_pos = _exp(gcum)
    e_neg = _exp(-gcum)
    a_t = -kk * _exp(gcum - logw)
    b_t = kk * a * e_neg
    k_t = kr2 * e_neg
    r_t = r * e_pos
    groups = []
    for g in range(RWKV_GROUPS):
        sl = slice(g * GROUP_W, (g + 1) * GROUP_W)
        at_g, bt_g, kt_g, rt_g, v_g = a_t[:, sl], b_t[:, sl], k_t[:, sl], r_t[:, sl], vr[:, sl]
        aa = _mm_nt(_cat([at_g, rt_g], 0),
                    _cat([_block_stack(bt_g, GROUP, RWKV_HEAD_DIM), _block_stack(kt_g, GROUP, RWKV_HEAD_DIM)], 0))
        a_ab = _where(strict_p, aa[0:c, 0:gc4], 0.0)
        a_ak = _where(strict_p, aa[0:c, gc4:2 * gc4], 0.0)
        a_rb = _where(lower_p, aa[c:2 * c, 0:gc4], 0.0)
        a_rk = _where(lower_p, aa[c:2 * c, gc4:2 * gc4], 0.0)
        akv = _mm(a_ak, _block_stack(v_g, GROUP, RWKV_HEAD_DIM))
        groups.append(dict(nil=a_ab, at=at_g, bt=bt_g, kt=kt_g, rt=rt_g, v=v_g, akv=akv,
                           a_r=_cat([a_rb, a_rk], 1), e_last=e_pos[c - 1:c, sl]))
    bonus = _head_sum(r * kr2 * rvec_ref[4:5, :], RWKV_HEAD_DIM) * vr
    return dict(groups=groups, gate=gate, bonus=bonus)


def _rwkv_finish(p, tinvs, c, states, rvec_ref):
    bd_mask = (lax.broadcasted_iota(jnp.int32, (GROUP_W, GROUP_W), 0) // RWKV_HEAD_DIM
               == lax.broadcasted_iota(jnp.int32, (GROUP_W, GROUP_W), 1) // RWKV_HEAD_DIM)
    qs = p["groups"]
    rng = range(RWKV_GROUPS)
    wu0 = [_mm(tinvs[g], _cat([_block_stack(qs[g]["at"], GROUP, RWKV_HEAD_DIM),
                               _block_stack(qs[g]["akv"], GROUP, RWKV_HEAD_DIM)], 1)) for g in rng]
    wr = [_mm_nt(_cat([wu0[g][:, 0:GROUP_W], qs[g]["rt"]], 0), states[g]) for g in rng]
    u = [wr[g][0:c] + wu0[g][:, GROUP_W:2 * GROUP_W] for g in rng]
    o_g = [wr[g][c:2 * c] + _mm(qs[g]["a_r"], _cat([_block_stack(u[g], GROUP, RWKV_HEAD_DIM),
                                                    _block_stack(qs[g]["v"], GROUP, RWKV_HEAD_DIM)], 0)) for g in rng]
    upd = [_mm_tn(_cat([u[g], qs[g]["v"]], 0), _cat([qs[g]["bt"], qs[g]["kt"]], 0)) for g in rng]
    new_states = [(states[g] + _where(bd_mask, upd[g], 0.0)) * qs[g]["e_last"] for g in rng]
    o_r = _cat(o_g, 1)
    mean_o = _head_sum(o_r, RWKV_HEAD_DIM) * (1.0 / RWKV_HEAD_DIM)
    dev = o_r - mean_o
    var_o = _head_sum(dev * dev, RWKV_HEAD_DIM) * (1.0 / RWKV_HEAD_DIM)
    o_n = dev * _rsqrt(var_o + RWKV_GN_EPS) * rvec_ref[5:6, :] + rvec_ref[6:7, :]
    return (o_n + p["bonus"]) * p["gate"], new_states


def _mixer_kernel(c, rows_in, t_out, t_last, n_chunks, nb,
                  proj_ref, cinit_ref, sinit_ref, sg0_ref, sr0_ref,
                  convw_ref, gvec_ref, mu_ref, rvec_ref, loraw_ref, g2_ref,
                  mix_ref, convo_ref, shifto_ref, sgo_ref, sro_ref,
                  prev_scr, sg_scr, sr_scr):
    ci = pl.program_id(1)
    gc4 = GROUP * c
    hd = RWKV_HEAD_DIM

    @pl.when(ci == 0)
    def _():
        prev_scr[...] = jnp.zeros(prev_scr.shape, F32)
        for j in range(nb):
            prev_scr[j, c - SUBLANES:c, 0:GDN_QKV] = cinit_ref[j]
            prev_scr[j, c - 1:c, COL_RWKV:COL_BA] = sinit_ref[j]
            for g in range(RWKV_GROUPS):
                blocks = []
                for h in range(GROUP):
                    parts = [sr0_ref[j, g * GROUP + h]]
                    if h > 0:
                        parts.insert(0, jnp.zeros((hd, h * hd), F32))
                    if h < GROUP - 1:
                        parts.append(jnp.zeros((hd, (GROUP - 1 - h) * hd), F32))
                    blocks.append(jnp.concatenate(parts, axis=1))
                sr_scr[j, g] = jnp.concatenate(blocks, axis=0)
        sg_scr[...] = sg0_ref[...]

    row_p = lax.broadcasted_iota(jnp.int32, (c, gc4), 0)
    col_p = lax.broadcasted_iota(jnp.int32, (c, gc4), 1) % c
    grp_p = lax.broadcasted_iota(jnp.int32, (c, gc4), 1) // c
    tri = jnp.where(lax.broadcasted_iota(jnp.int32, (c, c), 1) <= lax.broadcasted_iota(jnp.int32, (c, c), 0),
                    1.0, 0.0)
    masks = (row_p, col_p, grp_p, col_p <= row_p, col_p < row_p, tri)

    def load_x(j):
        x = proj_ref[j]
        if rows_in < c:
            x = jnp.concatenate([x, jnp.zeros((c - rows_in, PROJ_COLS), F32)], axis=0)
        return x

    s_gdn = [_Seqs([sg_scr[j, h] for j in range(nb)]) for h in range(GDN_HEADS)]
    s_rwkv = [_Seqs([sr_scr[j, g] for j in range(nb)]) for g in range(RWKV_GROUPS)]
    x = _Seqs([load_x(j) for j in range(nb)])
    prev = _Seqs([prev_scr[j] for j in range(nb)])

    gdn = _gdn_prepare(x, prev, c, t_last, masks, convw_ref, gvec_ref)
    rwkv = _rwkv_prepare(x, prev, c, t_last, masks, mu_ref, rvec_ref, loraw_ref, g2_ref)
    tinvs = _inv_series([gdn["nil"]] + [q["nil"] for q in rwkv["groups"]], c, GROUP)
    o_gdn, new_g = _gdn_finish(gdn, tinvs[0], x, c, s_gdn, gvec_ref)
    o_rwkv, new_r = _rwkv_finish(rwkv, tinvs[1:], c, s_rwkv, rvec_ref)
    mix = _cat(o_gdn + [o_rwkv], 1)

    for j in range(nb):
        mix_ref[j] = mix.v[j][0:t_out].astype(mix_ref.dtype)
        prev_scr[j] = x.v[j][:, 0:COL_BA]
        for h in range(GDN_HEADS):
            sg_scr[j, h] = new_g[h].v[j]
        for g in range(RWKV_GROUPS):
            sr_scr[j, g] = new_r[g].v[j]

    @pl.when(ci == n_chunks - 1)
    def _():
        for j in range(nb):
            convo_ref[j] = x.v[j][t_last - (CONV_K - 1):t_last, 0:GDN_QKV]
            shifto_ref[j] = x.v[j][t_last - 1:t_last, COL_RWKV:COL_BA]
            for g in range(RWKV_GROUPS):
                s_g = new_r[g].v[j]
                for h in range(GROUP):
                    sro_ref[j, g * GROUP + h] = s_g[h * hd:(h + 1) * hd, h * hd:(h + 1) * hd]
            for h in range(GDN_HEADS):
                sgo_ref[j, h] = new_g[h].v[j]


def _mixer(proj3d, seq_len, conv_init8, shift_init, s_gdn, s_rwkv, mp, mix_dtype, nb):
    bsz, rows, _ = proj3d.shape
    nb = math.gcd(bsz, nb)
    if rows >= CHUNK:
        assert rows == seq_len and seq_len % CHUNK == 0
        c = CHUNK
        n_chunks, rows_in, t_out, t_last = seq_len // c, c, c, c
    else:
        c = rows
        n_chunks, rows_in, t_out, t_last = 1, rows, seq_len, seq_len
    assert t_last >= CONV_K - 1 and c % SUBLANES == 0
    const2 = lambda b, i: (0, 0)
    perb3 = lambda b, i: (b, 0, 0)
    perb4 = lambda b, i: (b, 0, 0, 0)
    kern = functools.partial(_mixer_kernel, c, rows_in, t_out, t_last, n_chunks, nb)
    state_g = (nb, GDN_HEADS, GDN_HEAD_DIM, GDN_HEAD_DIM)
    state_r = (nb, RWKV_HEADS, RWKV_HEAD_DIM, RWKV_HEAD_DIM)
    return pl.pallas_call(
        kern,
        grid=(bsz // nb, n_chunks),
        in_specs=[pl.BlockSpec((nb, rows_in, PROJ_COLS), lambda b, i: (b, i, 0)),
                  pl.BlockSpec((nb, SUBLANES, GDN_QKV), perb3),
                  pl.BlockSpec((nb, 1, RWKV_COLS), perb3),
                  pl.BlockSpec(state_g, perb4),
                  pl.BlockSpec(state_r, perb4),
                  pl.BlockSpec((CONV_K, GDN_QKV), const2),
                  pl.BlockSpec((SUBLANES, LANES), const2),
                  pl.BlockSpec((1, RWKV_COLS), const2),
                  pl.BlockSpec((SUBLANES, RWKV_WIDTH), const2),
                  pl.BlockSpec((LANES, 2 * RWKV_WIDTH), const2),
                  pl.BlockSpec((GATE_LORA, RWKV_WIDTH), const2)],
        out_specs=[pl.BlockSpec((nb, t_out, D_MODEL), lambda b, i: (b, i, 0)),
                   pl.BlockSpec((nb, CONV_K - 1, GDN_QKV), perb3),
                   pl.BlockSpec((nb, 1, RWKV_COLS), perb3),
                   pl.BlockSpec(state_g, perb4),
                   pl.BlockSpec(state_r, perb4)],
        out_shape=[jax.ShapeDtypeStruct((bsz, seq_len, D_MODEL), mix_dtype),
                   jax.ShapeDtypeStruct((bsz, CONV_K - 1, GDN_QKV), F32),
                   jax.ShapeDtypeStruct((bsz, 1, RWKV_COLS), F32),
                   jax.ShapeDtypeStruct((bsz,) + state_g[1:], F32),
                   jax.ShapeDtypeStruct((bsz,) + state_r[1:], F32)],
        scratch_shapes=[pltpu.VMEM((nb, c, COL_BA), F32),
                        pltpu.VMEM(state_g, F32),
                        pltpu.VMEM((nb, RWKV_GROUPS, GROUP_W, GROUP_W), F32)],
        compiler_params=pltpu.CompilerParams(dimension_semantics=("parallel", "arbitrary"),
                                             vmem_limit_bytes=VMEM_LIMIT_BYTES),
        name="sequence_mixers",
    )(proj3d, conv_init8, shift_init, s_gdn, s_rwkv,
      mp["conv_w"], mp["gvec"], mp["mu"], mp["rvec"], mp["lora_w"], mp["g2"])


def _layernorm(x, g, b):
    mu = jnp.mean(x, axis=-1, keepdims=True)
    d = x - mu
    var = jnp.mean(d * d, axis=-1, keepdims=True)
    return d * lax.rsqrt(var + LN_EPS) * g + b


def _post_mixer_kernel(n_first, mixp_ref, mixs_ref, xp_ref, xs_ref, wo_ref, lnv_ref, wrt_ref, br_ref,
                       h_ref, idx_ref, rank_ref, gate_ref, cnt_ref, base_scr, hp_scr):
    i = pl.program_id(0)
    tt = h_ref.shape[0]

    @pl.when(i == 0)
    def _():
        base_scr[...] = jnp.zeros(base_scr.shape, F32)

    @pl.when(i < n_first)
    def _():
        hp_scr[...] = ALPHA * xp_ref[...] + jnp.dot(mixp_ref[...].astype(BF16), wo_ref[...],
                                                    preferred_element_type=F32)

    @pl.when(i >= n_first)
    def _():
        hp_scr[...] = ALPHA * xs_ref[...] + jnp.dot(mixs_ref[...].astype(BF16), wo_ref[...],
                                                    preferred_element_type=F32)

    h = _layernorm(hp_scr[...], lnv_ref[0:1, :], lnv_ref[1:2, :])
    h_ref[...] = h
    lt = lax.dot_general(wrt_ref[...], h, (((1,), (1,)), ((), ())),
                         precision=lax.Precision.HIGHEST, preferred_element_type=F32) + br_ref[...]
    eid = lax.broadcasted_iota(jnp.int32, lt.shape, 0)
    lt = jnp.where(eid < N_EXPERTS, lt, -jnp.inf)
    upper = jnp.where(lax.broadcasted_iota(jnp.int32, (tt, tt), 0) < lax.broadcasted_iota(jnp.int32, (tt, tt), 1),
                      1.0, 0.0).astype(BF16)
    base = base_scr[...]
    vals, idxs, ranks = [], [], []
    for _ in range(TOP_K):
        m = jnp.max(lt, axis=0, keepdims=True)
        sel = jnp.min(jnp.where(lt == m, eid, LANES), axis=0, keepdims=True)
        onehot = eid == sel
        lt = jnp.where(onehot, -jnp.inf, lt)
        oh = jnp.where(onehot, 1.0, 0.0)
        before = jnp.dot(oh.astype(BF16), upper, preferred_element_type=F32)
        ranks.append(jnp.sum(oh * (base + before), axis=0, keepdims=True))
        base = base + jnp.sum(oh, axis=1, keepdims=True)
        vals.append(m)
        idxs.append(sel)
    base_scr[...] = base
    ex = [jnp.exp(v - vals[0]) for v in vals]
    den = ex[0] + ex[1] + ex[2] + ex[3]
    pad_i = jnp.zeros((SUBLANES - TOP_K, tt), jnp.int32)
    idx_ref[...] = jnp.concatenate(idxs + [pad_i], axis=0)
    rank_ref[...] = jnp.concatenate([rk.astype(jnp.int32) for rk in ranks] + [pad_i], axis=0)
    gates = jnp.concatenate([e / den for e in ex] + [jnp.zeros((SUBLANES - TOP_K, tt), F32)], axis=0)
    gate_ref[...] = gates.T

    @pl.when(i == pl.num_programs(0) - 1)
    def _():
        cnt_ref[...] = base[:, 0:LANES].astype(jnp.int32)


def _post_mixer(mix_p, mix_s, x_p, x_s, w_o_bf16, ln1, w_router_t, b_router_col):
    tt = TOK_TILE
    n_p, n_s = x_p.shape[0], x_s.shape[0]
    assert n_p % tt == 0 and n_s % tt == 0
    n1, n2 = n_p // tt, n_s // tt
    n_tok = n_p + n_s
    const2 = lambda i: (0, 0)
    first = lambda i: (jnp.minimum(i, n1 - 1), 0)
    second = lambda i: (jnp.maximum(i - n1, 0), 0)
    return pl.pallas_call(
        functools.partial(_post_mixer_kernel, n1),
        grid=(n1 + n2,),
        in_specs=[pl.BlockSpec((tt, D_MODEL), first),
                  pl.BlockSpec((tt, D_MODEL), second),
                  pl.BlockSpec((tt, D_MODEL), first),
                  pl.BlockSpec((tt, D_MODEL), second),
                  pl.BlockSpec((D_MODEL, D_MODEL), const2),
                  pl.BlockSpec((SUBLANES, D_MODEL), const2),
                  pl.BlockSpec((LANES, D_MODEL), const2),
                  pl.BlockSpec((LANES, 1), const2)],
        out_specs=[pl.BlockSpec((tt, D_MODEL), lambda i: (i, 0)),
                   pl.BlockSpec((SUBLANES, tt), lambda i: (0, i)),
                   pl.BlockSpec((SUBLANES, tt), lambda i: (0, i)),
                   pl.BlockSpec((tt, SUBLANES), lambda i: (i, 0)),
                   pl.BlockSpec((LANES, LANES), const2)],
        out_shape=[jax.ShapeDtypeStruct((n_tok, D_MODEL), F32),
                   jax.ShapeDtypeStruct((SUBLANES, n_tok), jnp.int32),
                   jax.ShapeDtypeStruct((SUBLANES, n_tok), jnp.int32),
                   jax.ShapeDtypeStruct((n_tok, SUBLANES), F32),
                   jax.ShapeDtypeStruct((LANES, LANES), jnp.int32)],
        scratch_shapes=[pltpu.VMEM((LANES, tt), F32), pltpu.VMEM((tt, D_MODEL), F32)],
        compiler_params=pltpu.CompilerParams(dimension_semantics=("arbitrary",),
                                             vmem_limit_bytes=VMEM_LIMIT_BYTES),
        name="outproj_norm_router",
    )(mix_p, mix_s, x_p, x_s, w_o_bf16, ln1, w_router_t, b_router_col)


def _slot_kernel(pstart_ref, idx_ref, rank_ref, dest_ref):
    idx = idx_ref[...]
    dest = rank_ref[...]
    for e in range(N_EXPERTS):
        dest = dest + jnp.where(idx == e, pstart_ref[e], 0)
    dest_ref[...] = dest


def _slots(idx, rank, pstart):
    grid_spec = pltpu.PrefetchScalarGridSpec(
        num_scalar_prefetch=1,
        grid=(1,),
        in_specs=[pl.BlockSpec(idx.shape, lambda i, ps: (0, 0)),
                  pl.BlockSpec(idx.shape, lambda i, ps: (0, 0))],
        out_specs=pl.BlockSpec(idx.shape, lambda i, ps: (0, 0)),
    )
    return pl.pallas_call(
        _slot_kernel,
        grid_spec=grid_spec,
        out_shape=jax.ShapeDtypeStruct(idx.shape, jnp.int32),
        name="moe_slots",
    )(pstart, idx, rank)


def _dispatch_kernel(pend_ref, padded_ref, dest_ref, h_ref, xb_hbm, zero_scr, sem):
    i = pl.program_id(0)
    tt = h_ref.shape[0]
    tm = zero_scr.shape[0]

    @pl.when(i == 0)
    def _():
        zero_scr[...] = jnp.zeros(zero_scr.shape, F32)

        def tail_copy(e):
            start = pl.multiple_of(pend_ref[e] - tm, tm)
            return pltpu.make_async_copy(zero_scr, xb_hbm.at[pl.ds(start, tm), :], sem)

        for e in range(N_EXPERTS):
            @pl.when(padded_ref[e] > 0)
            def _():
                tail_copy(e).start()
        for e in range(N_EXPERTS):
            @pl.when(padded_ref[e] > 0)
            def _():
                tail_copy(e).wait()

        def spare_copy(b):
            return pltpu.make_async_copy(zero_scr, xb_hbm.at[pl.ds(pl.multiple_of(b * tm, tm), tm), :], sem)

        def spare_start(b, carry):
            spare_copy(b).start()
            return carry

        def spare_wait(b, carry):
            spare_copy(b).wait()
            return carry

        first_spare = pend_ref[N_EXPERTS - 1] // tm
        lax.fori_loop(first_spare, xb_hbm.shape[0] // tm, spare_start, 0)
        lax.fori_loop(first_spare, xb_hbm.shape[0] // tm, spare_wait, 0)

    def row_copy(t, k):
        return pltpu.make_async_copy(h_ref.at[pl.ds(t, 1), :], xb_hbm.at[pl.ds(dest_ref[k, t], 1), :], sem)

    def issue(t, carry):
        for k in range(TOP_K):
            row_copy(t, k).start(priority=k % 2)
        return carry

    lax.fori_loop(0, tt, issue, 0, unroll=DMA_UNROLL)

    def drain(t, carry):
        for k in range(TOP_K):
            row_copy(t, k).wait()
        return carry

    lax.fori_loop(0, tt, drain, 0, unroll=DMA_UNROLL)


def _dispatch(h2d, dest, pend, padded, n_rows):
    n_tok = h2d.shape[0]
    tt = DISPATCH_TILE
    assert n_tok % tt == 0
    grid_spec = pltpu.PrefetchScalarGridSpec(
        num_scalar_prefetch=2,
        grid=(n_tok // tt,),
        in_specs=[pl.BlockSpec((SUBLANES, tt), lambda i, pe, pa: (0, i), memory_space=pltpu.SMEM),
                  pl.BlockSpec((tt, D_MODEL), lambda i, pe, pa: (i, 0))],
        out_specs=pl.BlockSpec(memory_space=pl.ANY),
        scratch_shapes=[pltpu.VMEM((MOE_TILE, D_MODEL), F32), pltpu.SemaphoreType.DMA(())],
    )
    return pl.pallas_call(
        _dispatch_kernel,
        grid_spec=grid_spec,
        out_shape=jax.ShapeDtypeStruct((n_rows, D_MODEL), F32),
        compiler_params=pltpu.CompilerParams(dimension_semantics=("arbitrary",)),
        name="moe_dispatch",
    )(pend, padded, dest, h2d)


def _expert_kernel(be_ref, nused_ref, x_ref, wg_ref, wu_ref, wd_ref, bg_ref, bu_ref, bd_ref, y_ref,
                   wg16, wu16, wd16):
    i = pl.program_id(0)
    prev_e = be_ref[jnp.maximum(i - 1, 0)]
    fresh = jnp.logical_or(i == 0, be_ref[i] != prev_e)
    used = i < nused_ref[0]

    @pl.when(jnp.logical_and(fresh, used))
    def _():
        wg16[...] = wg_ref[0].astype(BF16)
        wu16[...] = wu_ref[0].astype(BF16)
        wd16[...] = wd_ref[0].astype(BF16)

    @pl.when(used)
    def _():
        x16 = x_ref[...].astype(BF16)
        gt = jnp.dot(x16, wg16[...], preferred_element_type=F32) + bg_ref[0]
        up = jnp.dot(x16, wu16[...], preferred_element_type=F32) + bu_ref[0]
        gt = jnp.minimum(gt, SWIGLU_LIMIT)
        up = jnp.clip(up, -SWIGLU_LIMIT, SWIGLU_LIMIT)
        hid = (up + 1.0) * gt * _sigmoid(SWIGLU_ALPHA * gt)
        y_ref[...] = jnp.dot(hid.astype(BF16), wd16[...], preferred_element_type=F32) + bd_ref[0]

    @pl.when(jnp.logical_not(used))
    def _():
        y_ref[...] = jnp.zeros(y_ref.shape, F32)


def _experts(xb, block_expert, n_used, w_gate, b_gate, w_up, b_up, w_down, b_down):
    n_rows = xb.shape[0]
    tm = MOE_TILE
    n_blocks = n_rows // tm
    d_e = w_gate.shape[2]
    wspec = lambda shape: pl.BlockSpec(shape, lambda i, be, nu: (be[i], 0, 0))
    grid_spec = pltpu.PrefetchScalarGridSpec(
        num_scalar_prefetch=2,
        grid=(n_blocks,),
        in_specs=[pl.BlockSpec((tm, D_MODEL), lambda i, be, nu: (jnp.maximum(jnp.minimum(i, nu[0] - 1), 0), 0)),
                  wspec((1, D_MODEL, d_e)), wspec((1, D_MODEL, d_e)), wspec((1, d_e, D_MODEL)),
                  wspec((1, 1, d_e)), wspec((1, 1, d_e)), wspec((1, 1, D_MODEL))],
        out_specs=pl.BlockSpec((tm, D_MODEL), lambda i, be, nu: (i, 0)),
        scratch_shapes=[pltpu.VMEM((D_MODEL, d_e), BF16), pltpu.VMEM((D_MODEL, d_e), BF16),
                        pltpu.VMEM((d_e, D_MODEL), BF16)],
    )
    return pl.pallas_call(
        _expert_kernel,
        grid_spec=grid_spec,
        out_shape=jax.ShapeDtypeStruct((n_rows, D_MODEL), F32),
        compiler_params=pltpu.CompilerParams(dimension_semantics=("arbitrary",),
                                             vmem_limit_bytes=VMEM_LIMIT_BYTES),
        name="moe_experts",
    )(block_expert, n_used, xb, w_gate, w_up, w_down,
      b_gate[:, None, :], b_up[:, None, :], b_down[:, None, :])


def _combine_kernel(n_first, dest_ref, h_ref, gate_ref, lnv_ref, yb_hbm, yp_ref, ys_ref, buf, sem):
    i = pl.program_id(0)
    tt = h_ref.shape[0]

    def row_copy(t, k):
        return pltpu.make_async_copy(yb_hbm.at[pl.ds(dest_ref[k, t], 1), :], buf.at[k, pl.ds(t, 1), :], sem)

    def issue(t, carry):
        for k in range(TOP_K):
            row_copy(t, k).start(priority=k % 2)
        return carry

    lax.fori_loop(0, tt, issue, 0, unroll=DMA_UNROLL)

    def drain(t, carry):
        for k in range(TOP_K):
            row_copy(t, k).wait()
        return carry

    lax.fori_loop(0, tt, drain, 0, unroll=DMA_UNROLL)
    gates = gate_ref[...]
    f = buf[0] * gates[:, 0:1]
    for k in range(1, TOP_K):
        f = f + buf[k] * gates[:, k:k + 1]
    y = _layernorm(ALPHA * h_ref[...] + f, lnv_ref[0:1, :], lnv_ref[1:2, :])

    @pl.when(i < n_first)
    def _():
        yp_ref[...] = y

    @pl.when(i >= n_first)
    def _():
        ys_ref[...] = y


def _combine(yb, h2d, dest, gates, ln2, n_p):
    n_tok = h2d.shape[0]
    tt = COMBINE_TILE
    n_s = n_tok - n_p
    assert n_p % tt == 0 and n_s % tt == 0
    n1, n2 = n_p // tt, n_s // tt
    return pl.pallas_call(
        functools.partial(_combine_kernel, n1),
        grid=(n1 + n2,),
        in_specs=[pl.BlockSpec((SUBLANES, tt), lambda i: (0, i), memory_space=pltpu.SMEM),
                  pl.BlockSpec((tt, D_MODEL), lambda i: (i, 0)),
                  pl.BlockSpec((tt, SUBLANES), lambda i: (i, 0)),
                  pl.BlockSpec((SUBLANES, D_MODEL), lambda i: (0, 0)),
                  pl.BlockSpec(memory_space=pl.ANY)],
        out_specs=[pl.BlockSpec((tt, D_MODEL), lambda i: (jnp.minimum(i, n1 - 1), 0)),
                   pl.BlockSpec((tt, D_MODEL), lambda i: (jnp.maximum(i - n1, 0), 0))],
        out_shape=[jax.ShapeDtypeStruct((n_p, D_MODEL), F32),
                   jax.ShapeDtypeStruct((n_s, D_MODEL), F32)],
        scratch_shapes=[pltpu.VMEM((TOP_K, tt, D_MODEL), F32), pltpu.SemaphoreType.DMA(())],
        compiler_params=pltpu.CompilerParams(dimension_semantics=("arbitrary",),
                                             vmem_limit_bytes=VMEM_LIMIT_BYTES),
        name="moe_combine_norm",
    )(dest, h2d, gates, ln2, yb)


def _pad_rows(v, rows):
    return jnp.concatenate([v, jnp.zeros((rows - v.shape[0],) + v.shape[1:], v.dtype)], axis=0)


def _mixer_params(conv_w, a_log, dt_bias, gdn_norm_w, mu_shift, w0, w2, a0, a2, g2, k_k, k_a, r_k, lnx_w, lnx_b):
    gvec = jnp.zeros((SUBLANES, LANES), F32)
    gvec = gvec.at[0, GDN_HEADS:2 * GDN_HEADS].set(a_log).at[1, GDN_HEADS:2 * GDN_HEADS].set(dt_bias)
    gvec = gvec.at[2, :].set(gdn_norm_w)
    rvec = _pad_rows(jnp.stack([w0, a0, k_k, k_a, r_k.reshape(-1), lnx_w, lnx_b]), SUBLANES)
    lora_w = jnp.zeros((LANES, 2 * RWKV_WIDTH), F32)
    lora_w = lora_w.at[0:DECAY_LORA, 0:RWKV_WIDTH].set(w2).at[DECAY_LORA:, RWKV_WIDTH:].set(a2)
    return dict(conv_w=conv_w, gvec=gvec, mu=mu_shift[None, :], rvec=rvec,
                lora_w=lora_w.astype(BF16), g2=g2.astype(BF16))


def _layer(x_prompt, x_sample, state_conv, state_shift, state_gdn, state_rwkv,
           w_in, mixer_params, w_o, ln1_g, ln1_b, w_router, b_router,
           w_gate, b_gate, w_up, b_up, w_down, b_down, ln2_g, ln2_b):
    bp, tp, d = x_prompt.shape
    bs, ts, _ = x_sample.shape
    n_p, n_s = bp * tp, bs * ts
    n_tok = n_p + n_s
    xp2d = x_prompt.reshape(n_p, d)
    xs2d = x_sample.reshape(n_s, d)

    in_cols = w_in.shape[1]
    off_ba = COL_Z + GDN_WIDTH
    w_in_r = jnp.concatenate([w_in[:, :off_ba], w_in[:, off_ba + 2 * GDN_HEADS:],
                              w_in[:, off_ba:off_ba + 2 * GDN_HEADS],
                              jnp.zeros((d, PROJ_COLS - in_cols), w_in.dtype)], axis=1).astype(BF16)

    rows_s = -(-ts // SUBLANES) * SUBLANES
    proj_p = _input_projection(xp2d, w_in_r).reshape(bp, tp, PROJ_COLS)
    proj_s = _input_projection(xs2d, w_in_r).reshape(bs, ts, PROJ_COLS)
    if rows_s != ts:
        proj_s = jnp.pad(proj_s, ((0, 0), (0, rows_s - ts), (0, 0)))
    zeros_p = (jnp.zeros((bp, SUBLANES, GDN_QKV), F32), jnp.zeros((bp, 1, RWKV_COLS), F32),
               jnp.zeros((bp, GDN_HEADS, GDN_HEAD_DIM, GDN_HEAD_DIM), F32),
               jnp.zeros((bp, RWKV_HEADS, RWKV_HEAD_DIM, RWKV_HEAD_DIM), F32))
    mix_p, conv_p, shift_p, gdn_p, rwkv_p = _mixer(proj_p, tp, *zeros_p, mixer_params, BF16,
                                                   SEQS_PER_STEP_LONG)
    conv8 = jnp.pad(state_conv, ((0, 0), (SUBLANES - (CONV_K - 1), 0), (0, 0)))
    mix_s, conv_s, shift_s, gdn_s, rwkv_s = _mixer(
        proj_s, ts, conv8, state_shift[:, None, :], state_gdn, state_rwkv, mixer_params, F32, SEQS_PER_STEP_SHORT)

    ln1 = _pad_rows(jnp.stack([ln1_g, ln1_b]), SUBLANES)
    ln2 = _pad_rows(jnp.stack([ln2_g, ln2_b]), SUBLANES)
    wrt = _pad_rows(w_router.T, LANES)
    brc = _pad_rows(b_router[:, None], LANES)
    h2d, idx, rank, gates, cnt = _post_mixer(mix_p.reshape(n_p, d), mix_s.reshape(n_s, d), xp2d, xs2d,
                                             w_o.astype(BF16), ln1, wrt, brc)

    counts = cnt[:N_EXPERTS, 0]
    padded = (((counts + MOE_TILE - 1) // MOE_TILE) * MOE_TILE).astype(jnp.int32)
    pend = jnp.cumsum(padded).astype(jnp.int32)
    pstart = pend - padded
    n_blocks = -(-(n_tok * TOP_K) // MOE_TILE) + N_EXPERTS
    n_rows = n_blocks * MOE_TILE
    block_expert = jnp.minimum(
        jnp.sum(pend[None, :] <= (jnp.arange(n_blocks) * MOE_TILE)[:, None], axis=1), N_EXPERTS - 1).astype(jnp.int32)
    n_used = pend[-1:] // MOE_TILE

    dest = _slots(idx, rank, pstart)
    xb = _dispatch(h2d, dest, pend, padded, n_rows)
    yb = _experts(xb, block_expert, n_used, w_gate, b_gate, w_up, b_up, w_down, b_down)
    y_p, y_s = _combine(yb, h2d, dest, gates, ln2, n_p)

    return (y_p.reshape(bp, tp, d), y_s.reshape(bs, ts, d), conv_p, shift_p[:, 0], gdn_p, rwkv_p,
            conv_s, shift_s[:, 0], gdn_s, rwkv_s)


def kernel(x_prompt, x_sample, state_conv, state_shift, state_gdn, state_rwkv, w_in, conv_w, a_log, dt_bias,
           gdn_norm_w, mu_shift, w0, w2, a0, a2, g2, k_k, k_a, r_k, lnx_w, lnx_b, w_o, ln1_g, ln1_b,
           w_router, b_router, w_gate, b_gate, w_up, b_up, w_down, b_down, ln2_g, ln2_b):
    mp = _mixer_params(conv_w, a_log, dt_bias, gdn_norm_w, mu_shift, w0, w2, a0, a2, g2, k_k, k_a, r_k,
                       lnx_w, lnx_b)
    return _layer(x_prompt, x_sample, state_conv, state_shift, state_gdn, state_rwkv,
                  w_in, mp, w_o, ln1_g, ln1_b, w_router, b_router,
                  w_gate, b_gate, w_up, b_up, w_down, b_down, ln2_g, ln2_b)
```

```python
import functools
import math

import jax
import jax.numpy as jnp
from jax import lax
from jax.experimental import pallas as pl
from jax.experimental.pallas import tpu as pltpu

F32 = jnp.float32
BF16 = jnp.bfloat16

D_MODEL = 1024
GDN_HEADS = 4
GDN_HEAD_DIM = 128
GDN_WIDTH = GDN_HEADS * GDN_HEAD_DIM
GDN_QKV = 3 * GDN_WIDTH
CONV_K = 4
RWKV_HEADS = 8
RWKV_HEAD_DIM = 64
RWKV_WIDTH = RWKV_HEADS * RWKV_HEAD_DIM
DECAY_LORA = 64
AAA_LORA = 64
GATE_LORA = 128
RWKV_COLS = 3 * RWKV_WIDTH + DECAY_LORA + AAA_LORA + GATE_LORA
RWKV_GN_EPS = RWKV_HEAD_DIM * 1e-5
N_EXPERTS = 32
TOP_K = 4
SWIGLU_LIMIT = 7.0
SWIGLU_ALPHA = 1.702
DEPTH = 1
ALPHA = (2.0 * DEPTH) ** 0.25
LN_EPS = 1e-5

LANES = 128
SUBLANES = 8
VMEM_LIMIT_BYTES = 56 * 1024 * 1024

COL_Z = GDN_QKV
COL_RWKV = COL_Z + GDN_WIDTH
COL_BA = COL_RWKV + RWKV_COLS
PROJ_COLS = COL_BA + LANES

CHUNK = 64
GROUP = 4
RWKV_GROUPS = RWKV_HEADS // GROUP
GROUP_W = GROUP * RWKV_HEAD_DIM
SEQS_PER_STEP_LONG = 4
SEQS_PER_STEP_SHORT = 8
PROJ_TILE = 256
TOK_TILE = 256
MOE_TILE = 512
DISPATCH_TILE = 256
COMBINE_TILE = 128
DMA_UNROLL = 8


class _Seqs:
    def __init__(self, vals):
        self.v = list(vals)

    def __getitem__(self, idx):
        return _Seqs([a[idx] for a in self.v])

    def __add__(self, o):
        return _lift(lambda a, b: a + b)(self, o)

    def __radd__(self, o):
        return _lift(lambda a, b: b + a)(self, o)

    def __sub__(self, o):
        return _lift(lambda a, b: a - b)(self, o)

    def __rsub__(self, o):
        return _lift(lambda a, b: b - a)(self, o)

    def __mul__(self, o):
        return _lift(lambda a, b: a * b)(self, o)

    def __rmul__(self, o):
        return _lift(lambda a, b: b * a)(self, o)

    def __neg__(self):
        return _Seqs([-a for a in self.v])

    @property
    def shape(self):
        return self.v[0].shape

    @property
    def T(self):
        return _Seqs([a.T for a in self.v])

    def astype(self, dt):
        return _Seqs([a.astype(dt) for a in self.v])


def _lift(f):
    def g(*args, **kw):
        n = next((len(a.v) for a in args if isinstance(a, _Seqs)), None)
        if n is None:
            return f(*args, **kw)
        return _Seqs([f(*[a.v[i] if isinstance(a, _Seqs) else a for a in args], **kw) for i in range(n)])
    return g


def _cat(parts, axis):
    n = next((len(a.v) for a in parts if isinstance(a, _Seqs)), None)
    if n is None:
        return jnp.concatenate(parts, axis=axis)
    return _Seqs([jnp.concatenate([a.v[i] if isinstance(a, _Seqs) else a for a in parts], axis=axis)
                  for i in range(n)])


_exp = _lift(jnp.exp)
_tanh = _lift(jnp.tanh)
_where = _lift(jnp.where)
_rsqrt = _lift(lax.rsqrt)
_sum = _lift(jnp.sum)
_roll = _lift(pltpu.roll)
_bcast = _lift(jnp.broadcast_to)


@_lift
def _mm(a, b):
    return jnp.dot(a.astype(BF16), b.astype(BF16), preferred_element_type=F32)


@_lift
def _mm_nt(a, b):
    return lax.dot_general(a.astype(BF16), b.astype(BF16), (((1,), (1,)), ((), ())), preferred_element_type=F32)


def _mm_tn(a, b):
    return _mm(a.T, b)


@_lift
def _mm_f32(a, b):
    return jnp.dot(a, b, precision=lax.Precision.HIGHEST, preferred_element_type=F32)


def _sigmoid(x):
    return 0.5 * _tanh(0.5 * x) + 0.5


@_lift
def _softplus(x):
    return jnp.maximum(x, 0.0) + jnp.log1p(jnp.exp(-jnp.abs(x)))


def _proj_kernel(x_ref, w_ref, o_ref):
    o_ref[...] = jnp.dot(x_ref[...].astype(BF16), w_ref[...], preferred_element_type=F32)


def _input_projection(x2d, w_in_bf16):
    n_tok = x2d.shape[0]
    assert n_tok % PROJ_TILE == 0
    return pl.pallas_call(
        _proj_kernel,
        grid=(n_tok // PROJ_TILE,),
        in_specs=[pl.BlockSpec((PROJ_TILE, D_MODEL), lambda i: (i, 0)),
                  pl.BlockSpec((D_MODEL, PROJ_COLS), lambda i: (0, 0))],
        out_specs=pl.BlockSpec((PROJ_TILE, PROJ_COLS), lambda i: (i, 0)),
        out_shape=jax.ShapeDtypeStruct((n_tok, PROJ_COLS), F32),
        compiler_params=pltpu.CompilerParams(dimension_semantics=("parallel",),
                                             vmem_limit_bytes=VMEM_LIMIT_BYTES),
        name="input_projection",
    )(x2d, w_in_bf16)


def _shift_rows(cur, prev, s, row_ids):
    return _where(row_ids < s, _roll(prev, s, axis=0), _roll(cur, s, axis=0))


@_lift
def _block_stack(x, n, width):
    grp = lax.broadcasted_iota(jnp.int32, x.shape, 1) // width
    if x.shape[0] % (2 * SUBLANES) == 0:
        x16 = x.astype(BF16)
        return jnp.concatenate([x16 * jnp.where(grp == i, 1.0, 0.0).astype(BF16) for i in range(n)], axis=0)
    return jnp.concatenate([jnp.where(grp == i, x, 0.0) for i in range(n)], axis=0)


def _inv_series(nils, c, n):
    row = lax.broadcasted_iota(jnp.int32, nils[0].shape, 0)
    col = lax.broadcasted_iota(jnp.int32, nils[0].shape, 1) % c
    eye = jnp.where(row == col, 1.0, 0.0)
    xs = [eye + nil for nil in nils]
    qs = [_mm(nil, _block_stack(nil, n, c)) for nil in nils]
    levels = int(math.log2(c))
    for lvl in range(1, levels):
        nxt_q, nxt_x = [], []
        for q, x in zip(qs, xs):
            bd = _block_stack(q, n, c)
            if lvl == levels - 1:
                nxt_x.append(x + _mm(x, bd))
            else:
                both = _mm(_cat([q, x], 0), bd)
                nxt_q.append(both[0:c])
                nxt_x.append(x + both[c:2 * c])
        qs, xs = nxt_q, nxt_x
    return xs


def _head_sum(x, width):
    pieces = []
    for j in range(x.shape[1] // LANES):
        xb = x[:, j * LANES:(j + 1) * LANES]
        if width == LANES:
            pieces.append(_bcast(_sum(xb, axis=-1, keepdims=True), xb.shape))
        else:
            lo = lax.broadcasted_iota(jnp.int32, xb.shape, 1) < width
            s0 = _sum(_where(lo, xb, 0.0), axis=-1, keepdims=True)
            s1 = _sum(_where(lo, 0.0, xb), axis=-1, keepdims=True)
            pieces.append(_where(lo, s0, s1))
    return _cat(pieces, 1)


def _gdn_head(a, h):
    return a[:, h * GDN_HEAD_DIM:(h + 1) * GDN_HEAD_DIM]


def _gdn_prepare(x, prev, c, t_last, masks, convw_ref, gvec_ref):
    row_p, col_p, grp_p, lower_p, strict_p, tri = masks
    masked = t_last < c

    def vmask(a):
        if not masked:
            return a
        return _where(lax.broadcasted_iota(jnp.int32, a.shape, 0) < t_last, a, 0.0)

    xq = x[:, 0:GDN_QKV]
    pq = prev[:, 0:GDN_QKV]
    rid = lax.broadcasted_iota(jnp.int32, (c, GDN_QKV), 0)
    conv = _shift_rows(xq, pq, 3, rid) * convw_ref[0:1, :]
    conv = conv + _shift_rows(xq, pq, 2, rid) * convw_ref[1:2, :]
    conv = conv + _shift_rows(xq, pq, 1, rid) * convw_ref[2:3, :]
    conv = conv + xq * convw_ref[3:4, :]
    conv = conv * _sigmoid(conv)

    ba = x[:, COL_BA:COL_BA + LANES]
    beta_blk = vmask(_sigmoid(ba))
    g_blk = vmask(-jnp.exp(gvec_ref[0:1, :]) * _softplus(ba + gvec_ref[1:2, :]))

    qn, kn, kb, vb = [], [], [], []
    for h in range(GDN_HEADS):
        qh = _gdn_head(conv[:, 0:GDN_WIDTH], h)
        kh = _gdn_head(conv[:, GDN_WIDTH:2 * GDN_WIDTH], h)
        vh = vmask(_gdn_head(conv[:, 2 * GDN_WIDTH:GDN_QKV], h))
        qh = qh * (_rsqrt(_sum(qh * qh, axis=-1, keepdims=True) + 1e-6) * (GDN_HEAD_DIM ** -0.5))
        kh = vmask(kh * _rsqrt(_sum(kh * kh, axis=-1, keepdims=True) + 1e-6))
        bh = beta_blk[:, h:h + 1]
        qn.append(qh), kn.append(kh), kb.append(kh * bh), vb.append(vh * bh)

    ma = _mm_nt(_cat([_cat(kb, 1), _cat(qn, 1)], 0),
                _block_stack(_cat(kn, 1), GDN_HEADS, GDN_HEAD_DIM))
    gexp = _where(grp_p == 0, g_blk[:, GDN_HEADS:GDN_HEADS + 1], 0.0)
    for h in range(1, GDN_HEADS):
        gexp = _where(grp_p == h, g_blk[:, GDN_HEADS + h:GDN_HEADS + h + 1], gexp)
    gcol = _mm_f32(tri, gexp)
    grow = _sum(_where(row_p == col_p, gcol, 0.0), axis=0, keepdims=True)
    decay = _exp(_where(lower_p, gcol - grow, -jnp.inf))
    m_p = _where(strict_p, ma[0:c] * decay, 0.0)
    attn = ma[c:2 * c] * decay
    egc, kdec, glast = [], [], []
    for h in range(GDN_HEADS):
        gch = gcol[:, h * c:h * c + 1]
        glh = gcol[c - 1:c, h * c:h * c + 1]
        egc.append(_exp(gch))
        kdec.append(kn[h] * _exp(glh - gch))
        glast.append(_exp(glh))
    rhs2 = _cat([_block_stack(_cat(vb, 1), GDN_HEADS, GDN_HEAD_DIM),
                 _block_stack(_cat([kb[h] * egc[h] for h in range(GDN_HEADS)], 1), GDN_HEADS, GDN_HEAD_DIM)], 1)
    qdec = [qn[h] * egc[h] for h in range(GDN_HEADS)]
    return dict(nil=-m_p, attn=attn, rhs2=rhs2, qdec=qdec, kdec=kdec, glast=glast)


def _gdn_finish(p, tinv, x, c, states, gvec_ref):
    uw = _mm(tinv, p["rhs2"])
    vnew, o1 = [], []
    for h in range(GDN_HEADS):
        ws = _mm(_cat([_gdn_head(uw[:, GDN_WIDTH:2 * GDN_WIDTH], h), p["qdec"][h]], 0), states[h])
        vnew.append(_gdn_head(uw[:, 0:GDN_WIDTH], h) - ws[0:c])
        o1.append(ws[c:2 * c])
    o2 = _mm(p["attn"], _block_stack(_cat(vnew, 1), GDN_HEADS, GDN_HEAD_DIM))
    out, new_states = [], []
    for h in range(GDN_HEADS):
        new_states.append(states[h] * p["glast"][h] + _mm_tn(p["kdec"][h], vnew[h]))
        oh = o1[h] + _gdn_head(o2, h)
        zh = _gdn_head(x[:, COL_Z:COL_Z + GDN_WIDTH], h)
        oh = oh * _rsqrt(_sum(oh * oh, axis=-1, keepdims=True) * (1.0 / GDN_HEAD_DIM) + 1e-6) * gvec_ref[2:3, :]
        out.append(oh * (zh * _sigmoid(zh)))
    return out, new_states


def _rwkv_prepare(x, prev, c, t_last, masks, mu_ref, rvec_ref, loraw_ref, g2_ref):
    row_p, col_p, grp_p, lower_p, strict_p, tri = masks
    masked = t_last < c
    gc4 = GROUP * c

    def vmask(a):
        if not masked:
            return a
        return _where(lax.broadcasted_iota(jnp.int32, a.shape, 0) < t_last, a, 0.0)

    rw = x[:, COL_RWKV:COL_BA]
    rid = lax.broadcasted_iota(jnp.int32, (c, RWKV_COLS), 0)
    prev_row = _shift_rows(rw, prev[:, COL_RWKV:COL_BA], 1, rid)
    rs = rw + (prev_row - rw) * mu_ref[...]
    o1_, o2_, o3_ = RWKV_WIDTH, 2 * RWKV_WIDTH, 3 * RWKV_WIDTH
    r = rs[:, 0:o1_]
    kr = rs[:, o1_:o2_]
    vr = rs[:, o2_:o3_]
    la = rs[:, o3_:o3_ + LANES]
    gl = rs[:, o3_ + LANES:o3_ + 2 * LANES]
    lane1 = lax.broadcasted_iota(jnp.int32, (c, LANES), 1)
    wa = _mm(_where(lane1 < DECAY_LORA, _tanh(la), la), loraw_ref[...])
    logw = vmask(-math.exp(-0.5) * _sigmoid(rvec_ref[0:1, :] + wa[:, 0:RWKV_WIDTH]))
    a = _sigmoid(rvec_ref[1:2, :] + wa[:, RWKV_WIDTH:2 * RWKV_WIDTH])
    gate = _mm(_sigmoid(gl), g2_ref[...])
    kkr = kr * rvec_ref[2:3, :]
    kk = vmask(kkr * _rsqrt(_head_sum(kkr * kkr, RWKV_HEAD_DIM) + 1e-6))
    kr2 = vmask(kr * (1.0 + (a - 1.0) * rvec_ref[3:4, :]))
    gcum = _mm_f32(tri, logw)
    e_pos = _exp(gcum)
    e_neg = _exp(-gcum)
    a_t = -kk * _exp(gcum - logw)
    b_t = kk * a * e_neg
    k_t = kr2 * e_neg
    r_t = r * e_pos
    groups = []
    for g in range(RWKV_GROUPS):
        sl = slice(g * GROUP_W, (g + 1) * GROUP_W)
        at_g, bt_g, kt_g, rt_g, v_g = a_t[:, sl], b_t[:, sl], k_t[:, sl], r_t[:, sl], vr[:, sl]
        aa = _mm_nt(_cat([at_g, rt_g], 0),
                    _cat([_block_stack(bt_g, GROUP, RWKV_HEAD_DIM), _block_stack(kt_g, GROUP, RWKV_HEAD_DIM)], 0))
        a_ab = _where(strict_p, aa[0:c, 0:gc4], 0.0)
        a_ak = _where(strict_p, aa[0:c, gc4:2 * gc4], 0.0)
        a_rb = _where(lower_p, aa[c:2 * c, 0:gc4], 0.0)
        a_rk = _where(lower_p, aa[c:2 * c, gc4:2 * gc4], 0.0)
        akv = _mm(a_ak, _block_stack(v_g, GROUP, RWKV_HEAD_DIM))
        groups.append(dict(nil=a_ab, at=at_g, bt=bt_g, kt=kt_g, rt=rt_g, v=v_g, akv=akv,
                           a_r=_cat([a_rb, a_rk], 1), e_last=e_pos[c - 1:c, sl]))
    bonus = _head_sum(r * kr2 * rvec_ref[4:5, :], RWKV_HEAD_DIM) * vr
    return dict(groups=groups, gate=gate, bonus=bonus)


def _rwkv_finish(p, tinvs, c, states, rvec_ref):
    bd_mask = (lax.broadcasted_iota(jnp.int32, (GROUP_W, GROUP_W), 0) // RWKV_HEAD_DIM
               == lax.broadcasted_iota(jnp.int32, (GROUP_W, GROUP_W), 1) // RWKV_HEAD_DIM)
    qs = p["groups"]
    rng = range(RWKV_GROUPS)
    wu0 = [_mm(tinvs[g], _cat([_block_stack(qs[g]["at"], GROUP, RWKV_HEAD_DIM),
                               _block_stack(qs[g]["akv"], GROUP, RWKV_HEAD_DIM)], 1)) for g in rng]
    wr = [_mm_nt(_cat([wu0[g][:, 0:GROUP_W], qs[g]["rt"]], 0), states[g]) for g in rng]
    u = [wr[g][0:c] + wu0[g][:, GROUP_W:2 * GROUP_W] for g in rng]
    o_g = [wr[g][c:2 * c] + _mm(qs[g]["a_r"], _cat([_block_stack(u[g], GROUP, RWKV_HEAD_DIM),
                                                    _block_stack(qs[g]["v"], GROUP, RWKV_HEAD_DIM)], 0)) for g in rng]
    upd = [_mm_tn(_cat([u[g], qs[g]["v"]], 0), _cat([qs[g]["bt"], qs[g]["kt"]], 0)) for g in rng]
    new_states = [(states[g] + _where(bd_mask, upd[g], 0.0)) * qs[g]["e_last"] for g in rng]
    o_r = _cat(o_g, 1)
    mean_o = _head_sum(o_r, RWKV_HEAD_DIM) * (1.0 / RWKV_HEAD_DIM)
    dev = o_r - mean_o
    var_o = _head_sum(dev * dev, RWKV_HEAD_DIM) * (1.0 / RWKV_HEAD_DIM)
    o_n = dev * _rsqrt(var_o + RWKV_GN_EPS) * rvec_ref[5:6, :] + rvec_ref[6:7, :]
    return (o_n + p["bonus"]) * p["gate"], new_states


def _mixer_kernel(c, rows_in, t_out, t_last, n_chunks, nb,
                  proj_ref, cinit_ref, sinit_ref, sg0_ref, sr0_ref,
                  convw_ref, gvec_ref, mu_ref, rvec_ref, loraw_ref, g2_ref,
                  mix_ref, convo_ref, shifto_ref, sgo_ref, sro_ref,
                  prev_scr, sg_scr, sr_scr):
    ci = pl.program_id(1)
    gc4 = GROUP * c
    hd = RWKV_HEAD_DIM

    @pl.when(ci == 0)
    def _():
        prev_scr[...] = jnp.zeros(prev_scr.shape, F32)
        for j in range(nb):
            prev_scr[j, c - SUBLANES:c, 0:GDN_QKV] = cinit_ref[j]
            prev_scr[j, c - 1:c, COL_RWKV:COL_BA] = sinit_ref[j]
            for g in range(RWKV_GROUPS):
                blocks = []
                for h in range(GROUP):
                    parts = [sr0_ref[j, g * GROUP + h]]
                    if h > 0:
                        parts.insert(0, jnp.zeros((hd, h * hd), F32))
                    if h < GROUP - 1:
                        parts.append(jnp.zeros((hd, (GROUP - 1 - h) * hd), F32))
                    blocks.append(jnp.concatenate(parts, axis=1))
                sr_scr[j, g] = jnp.concatenate(blocks, axis=0)
        sg_scr[...] = sg0_ref[...]

    row_p = lax.broadcasted_iota(jnp.int32, (c, gc4), 0)
    col_p = lax.broadcasted_iota(jnp.int32, (c, gc4), 1) % c
    grp_p = lax.broadcasted_iota(jnp.int32, (c, gc4), 1) // c
    tri = jnp.where(lax.broadcasted_iota(jnp.int32, (c, c), 1) <= lax.broadcasted_iota(jnp.int32, (c, c), 0),
                    1.0, 0.0)
    masks = (row_p, col_p, grp_p, col_p <= row_p, col_p < row_p, tri)

    def load_x(j):
        x = proj_ref[j]
        if rows_in < c:
            x = jnp.concatenate([x, jnp.zeros((c - rows_in, PROJ_COLS), F32)], axis=0)
        return x

    s_gdn = [_Seqs([sg_scr[j, h] for j in range(nb)]) for h in range(GDN_HEADS)]
    s_rwkv = [_Seqs([sr_scr[j, g] for j in range(nb)]) for g in range(RWKV_GROUPS)]
    x = _Seqs([load_x(j) for j in range(nb)])
    prev = _Seqs([prev_scr[j] for j in range(nb)])

    gdn = _gdn_prepare(x, prev, c, t_last, masks, convw_ref, gvec_ref)
    rwkv = _rwkv_prepare(x, prev, c, t_last, masks, mu_ref, rvec_ref, loraw_ref, g2_ref)
    tinvs = _inv_series([gdn["nil"]] + [q["nil"] for q in rwkv["groups"]], c, GROUP)
    o_gdn, new_g = _gdn_finish(gdn, tinvs[0], x, c, s_gdn, gvec_ref)
    o_rwkv, new_r = _rwkv_finish(rwkv, tinvs[1:], c, s_rwkv, rvec_ref)
    mix = _cat(o_gdn + [o_rwkv], 1)

    for j in range(nb):
        mix_ref[j] = mix.v[j][0:t_out].astype(mix_ref.dtype)
        prev_scr[j] = x.v[j][:, 0:COL_BA]
        for h in range(GDN_HEADS):
            sg_scr[j, h] = new_g[h].v[j]
        for g in range(RWKV_GROUPS):
            sr_scr[j, g] = new_r[g].v[j]

    @pl.when(ci == n_chunks - 1)
    def _():
        for j in range(nb):
            convo_ref[j] = x.v[j][t_last - (CONV_K - 1):t_last, 0:GDN_QKV]
            shifto_ref[j] = x.v[j][t_last - 1:t_last, COL_RWKV:COL_BA]
            for g in range(RWKV_GROUPS):
                s_g = new_r[g].v[j]
                for h in range(GROUP):
                    sro_ref[j, g * GROUP + h] = s_g[h * hd:(h + 1) * hd, h * hd:(h + 1) * hd]
            for h in range(GDN_HEADS):
                sgo_ref[j, h] = new_g[h].v[j]


def _mixer(proj3d, seq_len, conv_init8, shift_init, s_gdn, s_rwkv, mp, mix_dtype, nb):
    bsz, rows, _ = proj3d.shape
    nb = math.gcd(bsz, nb)
    if rows >= CHUNK:
        assert rows == seq_len and seq_len % CHUNK == 0
        c = CHUNK
        n_chunks, rows_in, t_out, t_last = seq_len // c, c, c, c
    else:
        c = rows
        n_chunks, rows_in, t_out, t_last = 1, rows, seq_len, seq_len
    assert t_last >= CONV_K - 1 and c % SUBLANES == 0
    const2 = lambda b, i: (0, 0)
    perb3 = lambda b, i: (b, 0, 0)
    perb4 = lambda b, i: (b, 0, 0, 0)
    kern = functools.partial(_mixer_kernel, c, rows_in, t_out, t_last, n_chunks, nb)
    state_g = (nb, GDN_HEADS, GDN_HEAD_DIM, GDN_HEAD_DIM)
    state_r = (nb, RWKV_HEADS, RWKV_HEAD_DIM, RWKV_HEAD_DIM)
    return pl.pallas_call(
        kern,
        grid=(bsz // nb, n_chunks),
        in_specs=[pl.BlockSpec((nb, rows_in, PROJ_COLS), lambda b, i: (b, i, 0)),
                  pl.BlockSpec((nb, SUBLANES, GDN_QKV), perb3),
                  pl.BlockSpec((nb, 1, RWKV_COLS), perb3),
                  pl.BlockSpec(state_g, perb4),
                  pl.BlockSpec(state_r, perb4),
                  pl.BlockSpec((CONV_K, GDN_QKV), const2),
                  pl.BlockSpec((SUBLANES, LANES), const2),
                  pl.BlockSpec((1, RWKV_COLS), const2),
                  pl.BlockSpec((SUBLANES, RWKV_WIDTH), const2),
                  pl.BlockSpec((LANES, 2 * RWKV_WIDTH), const2),
                  pl.BlockSpec((GATE_LORA, RWKV_WIDTH), const2)],
        out_specs=[pl.BlockSpec((nb, t_out, D_MODEL), lambda b, i: (b, i, 0)),
                   pl.BlockSpec((nb, CONV_K - 1, GDN_QKV), perb3),
                   pl.BlockSpec((nb, 1, RWKV_COLS), perb3),
                   pl.BlockSpec(state_g, perb4),
                   pl.BlockSpec(state_r, perb4)],
        out_shape=[jax.ShapeDtypeStruct((bsz, seq_len, D_MODEL), mix_dtype),
                   jax.ShapeDtypeStruct((bsz, CONV_K - 1, GDN_QKV), F32),
                   jax.ShapeDtypeStruct((bsz, 1, RWKV_COLS), F32),
                   jax.ShapeDtypeStruct((bsz,) + state_g[1:], F32),
                   jax.ShapeDtypeStruct((bsz,) + state_r[1:], F32)],
        scratch_shapes=[pltpu.VMEM((nb, c, COL_BA), F32),
                        pltpu.VMEM(state_g, F32),
                        pltpu.VMEM((nb, RWKV_GROUPS, GROUP_W, GROUP_W), F32)],
        compiler_params=pltpu.CompilerParams(dimension_semantics=("parallel", "arbitrary"),
                                             vmem_limit_bytes=VMEM_LIMIT_BYTES),
        name="sequence_mixers",
    )(proj3d, conv_init8, shift_init, s_gdn, s_rwkv,
      mp["conv_w"], mp["gvec"], mp["mu"], mp["rvec"], mp["lora_w"], mp["g2"])


def _layernorm(x, g, b):
    mu = jnp.mean(x, axis=-1, keepdims=True)
    d = x - mu
    var = jnp.mean(d * d, axis=-1, keepdims=True)
    return d * lax.rsqrt(var + LN_EPS) * g + b


def _post_mixer_kernel(n_first, mixp_ref, mixs_ref, xp_ref, xs_ref, wo_ref, lnv_ref, wrt_ref, br_ref,
                       h_ref, idx_ref, rank_ref, gate_ref, cnt_ref, base_scr, hp_scr):
    i = pl.program_id(0)
    tt = h_ref.shape[0]

    @pl.when(i == 0)
    def _():
        base_scr[...] = jnp.zeros(base_scr.shape, F32)

    @pl.when(i < n_first)
    def _():
        hp_scr[...] = ALPHA * xp_ref[...] + jnp.dot(mixp_ref[...].astype(BF16), wo_ref[...],
                                                    preferred_element_type=F32)

    @pl.when(i >= n_first)
    def _():
        hp_scr[...] = ALPHA * xs_ref[...] + jnp.dot(mixs_ref[...].astype(BF16), wo_ref[...],
                                                    preferred_element_type=F32)

    h = _layernorm(hp_scr[...], lnv_ref[0:1, :], lnv_ref[1:2, :])
    h_ref[...] = h
    lt = lax.dot_general(wrt_ref[...], h, (((1,), (1,)), ((), ())),
                         precision=lax.Precision.HIGHEST, preferred_element_type=F32) + br_ref[...]
    eid = lax.broadcasted_iota(jnp.int32, lt.shape, 0)
    lt = jnp.where(eid < N_EXPERTS, lt, -jnp.inf)
    upper = jnp.where(lax.broadcasted_iota(jnp.int32, (tt, tt), 0) < lax.broadcasted_iota(jnp.int32, (tt, tt), 1),
                      1.0, 0.0).astype(BF16)
    base = base_scr[...]
    vals, idxs, ranks = [], [], []
    for _ in range(TOP_K):
        m = jnp.max(lt, axis=0, keepdims=True)
        sel = jnp.min(jnp.where(lt == m, eid, LANES), axis=0, keepdims=True)
        onehot = eid == sel
        lt = jnp.where(onehot, -jnp.inf, lt)
        oh = jnp.where(onehot, 1.0, 0.0)
        before = jnp.dot(oh.astype(BF16), upper, preferred_element_type=F32)
        ranks.append(jnp.sum(oh * (base + before), axis=0, keepdims=True))
        base = base + jnp.sum(oh, axis=1, keepdims=True)
        vals.append(m)
        idxs.append(sel)
    base_scr[...] = base
    ex = [jnp.exp(v - vals[0]) for v in vals]
    den = ex[0] + ex[1] + ex[2] + ex[3]
    pad_i = jnp.zeros((SUBLANES - TOP_K, tt), jnp.int32)
    idx_ref[...] = jnp.concatenate(idxs + [pad_i], axis=0)
    rank_ref[...] = jnp.concatenate([rk.astype(jnp.int32) for rk in ranks] + [pad_i], axis=0)
    gates = jnp.concatenate([e / den for e in ex] + [jnp.zeros((SUBLANES - TOP_K, tt), F32)], axis=0)
    gate_ref[...] = gates.T

    @pl.when(i == pl.num_programs(0) - 1)
    def _():
        cnt_ref[...] = base[:, 0:LANES].astype(jnp.int32)


def _post_mixer(mix_p, mix_s, x_p, x_s, w_o_bf16, ln1, w_router_t, b_router_col):
    tt = TOK_TILE
    n_p, n_s = x_p.shape[0], x_s.shape[0]
    assert n_p % tt == 0 and n_s % tt == 0
    n1, n2 = n_p // tt, n_s // tt
    n_tok = n_p + n_s
    const2 = lambda i: (0, 0)
    first = lambda i: (jnp.minimum(i, n1 - 1), 0)
    second = lambda i: (jnp.maximum(i - n1, 0), 0)
    return pl.pallas_call(
        functools.partial(_post_mixer_kernel, n1),
        grid=(n1 + n2,),
        in_specs=[pl.BlockSpec((tt, D_MODEL), first),
                  pl.BlockSpec((tt, D_MODEL), second),
                  pl.BlockSpec((tt, D_MODEL), first),
                  pl.BlockSpec((tt, D_MODEL), second),
                  pl.BlockSpec((D_MODEL, D_MODEL), const2),
                  pl.BlockSpec((SUBLANES, D_MODEL), const2),
                  pl.BlockSpec((LANES, D_MODEL), const2),
                  pl.BlockSpec((LANES, 1), const2)],
        out_specs=[pl.BlockSpec((tt, D_MODEL), lambda i: (i, 0)),
                   pl.BlockSpec((SUBLANES, tt), lambda i: (0, i)),
                   pl.BlockSpec((SUBLANES, tt), lambda i: (0, i)),
                   pl.BlockSpec((tt, SUBLANES), lambda i: (i, 0)),
                   pl.BlockSpec((LANES, LANES), const2)],
        out_shape=[jax.ShapeDtypeStruct((n_tok, D_MODEL), F32),
                   jax.ShapeDtypeStruct((SUBLANES, n_tok), jnp.int32),
                   jax.ShapeDtypeStruct((SUBLANES, n_tok), jnp.int32),
                   jax.ShapeDtypeStruct((n_tok, SUBLANES), F32),
                   jax.ShapeDtypeStruct((LANES, LANES), jnp.int32)],
        scratch_shapes=[pltpu.VMEM((LANES, tt), F32), pltpu.VMEM((tt, D_MODEL), F32)],
        compiler_params=pltpu.CompilerParams(dimension_semantics=("arbitrary",),
                                             vmem_limit_bytes=VMEM_LIMIT_BYTES),
        name="outproj_norm_router",
    )(mix_p, mix_s, x_p, x_s, w_o_bf16, ln1, w_router_t, b_router_col)


def _slot_kernel(pstart_ref, idx_ref, rank_ref, dest_ref):
    idx = idx_ref[...]
    dest = rank_ref[...]
    for e in range(N_EXPERTS):
        dest = dest + jnp.where(idx == e, pstart_ref[e], 0)
    dest_ref[...] = dest


def _slots(idx, rank, pstart):
    grid_spec = pltpu.PrefetchScalarGridSpec(
        num_scalar_prefetch=1,
        grid=(1,),
        in_specs=[pl.BlockSpec(idx.shape, lambda i, ps: (0, 0)),
                  pl.BlockSpec(idx.shape, lambda i, ps: (0, 0))],
        out_specs=pl.BlockSpec(idx.shape, lambda i, ps: (0, 0)),
    )
    return pl.pallas_call(
        _slot_kernel,
        grid_spec=grid_spec,
        out_shape=jax.ShapeDtypeStruct(idx.shape, jnp.int32),
        name="moe_slots",
    )(pstart, idx, rank)


def _dispatch_kernel(pend_ref, padded_ref, count_ref, dest_ref, h_ref, xb_hbm, slot_ref, zero_scr, sem):
    i = pl.program_id(0)
    tt = h_ref.shape[0]
    tm = zero_scr.shape[0]
    n_rows = xb_hbm.shape[0]
    dump = pl.num_programs(0) * tt * TOP_K

    @pl.when(i == 0)
    def _():
        zero_scr[...] = jnp.zeros(zero_scr.shape, F32)

        def tail_copy(e):
            start = pl.multiple_of(pend_ref[e] - tm, tm)
            return pltpu.make_async_copy(zero_scr, xb_hbm.at[pl.ds(start, tm), :], sem)

        for e in range(N_EXPERTS):
            @pl.when(padded_ref[e] > 0)
            def _():
                tail_copy(e).start()
        for e in range(N_EXPERTS):
            @pl.when(padded_ref[e] > 0)
            def _():
                tail_copy(e).wait()

        def spare_copy(b):
            return pltpu.make_async_copy(zero_scr, xb_hbm.at[pl.ds(pl.multiple_of(b * tm, tm), tm), :], sem)

        def spare_start(b, carry):
            spare_copy(b).start()
            return carry

        def spare_wait(b, carry):
            spare_copy(b).wait()
            return carry

        first_spare = pend_ref[N_EXPERTS - 1] // tm
        lax.fori_loop(first_spare, n_rows // tm, spare_start, 0)
        lax.fori_loop(first_spare, n_rows // tm, spare_wait, 0)

        def to_dump(r, carry):
            slot_ref[r] = dump + r % tm
            return carry

        for e in range(N_EXPERTS):
            lax.fori_loop(pend_ref[e] - padded_ref[e] + count_ref[e], pend_ref[e], to_dump, 0)
        lax.fori_loop(pend_ref[N_EXPERTS - 1], n_rows, to_dump, 0)

    def row_copy(t, k):
        return pltpu.make_async_copy(h_ref.at[pl.ds(t, 1), :], xb_hbm.at[pl.ds(dest_ref[k, t], 1), :], sem)

    def issue(t, carry):
        for k in range(TOP_K):
            row_copy(t, k).start(priority=k % 2)
            slot_ref[dest_ref[k, t]] = k * (pl.num_programs(0) * tt) + i * tt + t
        return carry

    lax.fori_loop(0, tt, issue, 0, unroll=DMA_UNROLL)

    def drain(t, carry):
        for k in range(TOP_K):
            row_copy(t, k).wait()
        return carry

    lax.fori_loop(0, tt, drain, 0, unroll=DMA_UNROLL)


def _dispatch(h2d, dest, pend, padded, counts, n_rows):
    n_tok = h2d.shape[0]
    tt = DISPATCH_TILE
    assert n_tok % tt == 0
    grid_spec = pltpu.PrefetchScalarGridSpec(
        num_scalar_prefetch=3,
        grid=(n_tok // tt,),
        in_specs=[pl.BlockSpec((SUBLANES, tt), lambda i, pe, pa, co: (0, i), memory_space=pltpu.SMEM),
                  pl.BlockSpec((tt, D_MODEL), lambda i, pe, pa, co: (i, 0))],
        out_specs=[pl.BlockSpec(memory_space=pl.ANY), pl.BlockSpec(memory_space=pltpu.SMEM)],
        scratch_shapes=[pltpu.VMEM((MOE_TILE, D_MODEL), F32), pltpu.SemaphoreType.DMA(())],
    )
    return pl.pallas_call(
        _dispatch_kernel,
        grid_spec=grid_spec,
        out_shape=[jax.ShapeDtypeStruct((n_rows, D_MODEL), F32), jax.ShapeDtypeStruct((n_rows,), jnp.int32)],
        compiler_params=pltpu.CompilerParams(dimension_semantics=("arbitrary",)),
        name="moe_dispatch",
    )(pend, padded, counts, dest, h2d)


def _expert_kernel(be_ref, nused_ref, slot_ref, x_ref, wg_ref, wu_ref, wd_ref, bg_ref, bu_ref, bd_ref, ys_hbm,
                   wg16, wu16, wd16, ybuf, sem):
    i = pl.program_id(0)
    tm = x_ref.shape[0]
    n_used = nused_ref[0]
    dump = ys_hbm.shape[0] - tm
    prev_e = be_ref[jnp.maximum(i - 1, 0)]
    fresh = jnp.logical_or(i == 0, be_ref[i] != prev_e)
    used = i < n_used

    @pl.when(i == 0)
    def _():
        ybuf[1] = jnp.zeros((tm, D_MODEL), F32)
        clear = pltpu.make_async_copy(ybuf.at[1], ys_hbm.at[pl.ds(dump, tm), :], sem)
        clear.start()
        clear.wait()

    @pl.when(jnp.logical_and(fresh, used))
    def _():
        wg16[...] = wg_ref[0].astype(BF16)
        wu16[...] = wu_ref[0].astype(BF16)
        wd16[...] = wd_ref[0].astype(BF16)

    def row_out(buf, block, r):
        return pltpu.make_async_copy(ybuf.at[buf, pl.ds(r, 1), :],
                                     ys_hbm.at[pl.ds(slot_ref[block * tm + r], 1), :], sem)

    @pl.when(used)
    def _():
        cur = i % 2
        quarter = tm // 4

        prev = jnp.maximum(i - 1, 0)

        def send(lo, hi):
            for r in range(lo, hi):
                row_out(1 - cur, prev, r).start(priority=r % 2)

        send(0, quarter)
        x16 = x_ref[...].astype(BF16)
        gt = jnp.dot(x16, wg16[...], preferred_element_type=F32) + bg_ref[0]
        send(quarter, 2 * quarter)
        up = jnp.dot(x16, wu16[...], preferred_element_type=F32) + bu_ref[0]
        send(2 * quarter, 3 * quarter)
        gt = jnp.minimum(gt, SWIGLU_LIMIT)
        up = jnp.clip(up, -SWIGLU_LIMIT, SWIGLU_LIMIT)
        hid = (up + 1.0) * gt * _sigmoid(SWIGLU_ALPHA * gt)
        send(3 * quarter, tm)
        ybuf[cur] = jnp.dot(hid.astype(BF16), wd16[...], preferred_element_type=F32) + bd_ref[0]
        for r in range(tm):
            row_out(1 - cur, prev, r).wait()

    @pl.when(i == n_used - 1)
    def _():
        cur = i % 2
        for r in range(tm):
            row_out(cur, i, r).start(priority=r % 2)
        for r in range(tm):
            row_out(cur, i, r).wait()


def _experts(xb, row_slot, block_expert, n_used, n_slots, w_gate, b_gate, w_up, b_up, w_down, b_down):
    n_rows = xb.shape[0]
    tm = MOE_TILE
    n_blocks = n_rows // tm
    d_e = w_gate.shape[2]
    wspec = lambda shape: pl.BlockSpec(shape, lambda i, be, nu, sl: (be[i], 0, 0))
    grid_spec = pltpu.PrefetchScalarGridSpec(
        num_scalar_prefetch=3,
        grid=(n_blocks,),
        in_specs=[pl.BlockSpec((tm, D_MODEL), lambda i, be, nu, sl: (jnp.maximum(jnp.minimum(i, nu[0] - 1), 0), 0)),
                  wspec((1, D_MODEL, d_e)), wspec((1, D_MODEL, d_e)), wspec((1, d_e, D_MODEL)),
                  wspec((1, 1, d_e)), wspec((1, 1, d_e)), wspec((1, 1, D_MODEL))],
        out_specs=pl.BlockSpec(memory_space=pl.ANY),
        scratch_shapes=[pltpu.VMEM((D_MODEL, d_e), BF16), pltpu.VMEM((D_MODEL, d_e), BF16),
                        pltpu.VMEM((d_e, D_MODEL), BF16), pltpu.VMEM((2, tm, D_MODEL), F32),
                        pltpu.SemaphoreType.DMA(())],
    )
    return pl.pallas_call(
        _expert_kernel,
        grid_spec=grid_spec,
        out_shape=jax.ShapeDtypeStruct((n_slots + tm, D_MODEL), F32),
        compiler_params=pltpu.CompilerParams(dimension_semantics=("arbitrary",),
                                             vmem_limit_bytes=VMEM_LIMIT_BYTES),
        name="moe_experts",
    )(block_expert, n_used, row_slot, xb, w_gate, w_up, w_down,
      b_gate[:, None, :], b_up[:, None, :], b_down[:, None, :])


def _combine_kernel(n_first, y0_ref, y1_ref, y2_ref, y3_ref, h_ref, gate_ref, lnv_ref, yp_ref, ys_out_ref):
    i = pl.program_id(0)
    gates = gate_ref[...]
    f = y0_ref[...] * gates[:, 0:1]
    for k, y_ref in enumerate((y1_ref, y2_ref, y3_ref), start=1):
        f = f + y_ref[...] * gates[:, k:k + 1]
    y = _layernorm(ALPHA * h_ref[...] + f, lnv_ref[0:1, :], lnv_ref[1:2, :])

    @pl.when(i < n_first)
    def _():
        yp_ref[...] = y

    @pl.when(i >= n_first)
    def _():
        ys_out_ref[...] = y


def _combine(y_slots, h2d, gates, ln2, n_p):
    n_tok = h2d.shape[0]
    tt = COMBINE_TILE
    n_s = n_tok - n_p
    assert n_p % tt == 0 and n_s % tt == 0
    n1, n2 = n_p // tt, n_s // tt
    return pl.pallas_call(
        functools.partial(_combine_kernel, n1),
        grid=(n1 + n2,),
        in_specs=[pl.BlockSpec((tt, D_MODEL), lambda i, k=k: (k * (n1 + n2) + i, 0)) for k in range(TOP_K)] + [
                  pl.BlockSpec((tt, D_MODEL), lambda i: (i, 0)),
                  pl.BlockSpec((tt, SUBLANES), lambda i: (i, 0)),
                  pl.BlockSpec((SUBLANES, D_MODEL), lambda i: (0, 0))],
        out_specs=[pl.BlockSpec((tt, D_MODEL), lambda i: (jnp.minimum(i, n1 - 1), 0)),
                   pl.BlockSpec((tt, D_MODEL), lambda i: (jnp.maximum(i - n1, 0), 0))],
        out_shape=[jax.ShapeDtypeStruct((n_p, D_MODEL), F32),
                   jax.ShapeDtypeStruct((n_s, D_MODEL), F32)],
        compiler_params=pltpu.CompilerParams(dimension_semantics=("arbitrary",),
                                             vmem_limit_bytes=VMEM_LIMIT_BYTES),
        name="moe_combine_norm",
    )(*([y_slots] * TOP_K), h2d, gates, ln2)


def _pad_rows(v, rows):
    return jnp.concatenate([v, jnp.zeros((rows - v.shape[0],) + v.shape[1:], v.dtype)], axis=0)


def _mixer_params(conv_w, a_log, dt_bias, gdn_norm_w, mu_shift, w0, w2, a0, a2, g2, k_k, k_a, r_k, lnx_w, lnx_b):
    gvec = jnp.zeros((SUBLANES, LANES), F32)
    gvec = gvec.at[0, GDN_HEADS:2 * GDN_HEADS].set(a_log).at[1, GDN_HEADS:2 * GDN_HEADS].set(dt_bias)
    gvec = gvec.at[2, :].set(gdn_norm_w)
    rvec = _pad_rows(jnp.stack([w0, a0, k_k, k_a, r_k.reshape(-1), lnx_w, lnx_b]), SUBLANES)
    lora_w = jnp.zeros((LANES, 2 * RWKV_WIDTH), F32)
    lora_w = lora_w.at[0:DECAY_LORA, 0:RWKV_WIDTH].set(w2).at[DECAY_LORA:, RWKV_WIDTH:].set(a2)
    return dict(conv_w=conv_w, gvec=gvec, mu=mu_shift[None, :], rvec=rvec,
                lora_w=lora_w.astype(BF16), g2=g2.astype(BF16))


def _layer(x_prompt, x_sample, state_conv, state_shift, state_gdn, state_rwkv,
           w_in, mixer_params, w_o, ln1_g, ln1_b, w_router, b_router,
           w_gate, b_gate, w_up, b_up, w_down, b_down, ln2_g, ln2_b):
    bp, tp, d = x_prompt.shape
    bs, ts, _ = x_sample.shape
    n_p, n_s = bp * tp, bs * ts
    n_tok = n_p + n_s
    xp2d = x_prompt.reshape(n_p, d)
    xs2d = x_sample.reshape(n_s, d)

    in_cols = w_in.shape[1]
    off_ba = COL_Z + GDN_WIDTH
    w_in_r = jnp.concatenate([w_in[:, :off_ba], w_in[:, off_ba + 2 * GDN_HEADS:],
                              w_in[:, off_ba:off_ba + 2 * GDN_HEADS],
                              jnp.zeros((d, PROJ_COLS - in_cols), w_in.dtype)], axis=1).astype(BF16)

    rows_s = -(-ts // SUBLANES) * SUBLANES
    proj_p = _input_projection(xp2d, w_in_r).reshape(bp, tp, PROJ_COLS)
    proj_s = _input_projection(xs2d, w_in_r).reshape(bs, ts, PROJ_COLS)
    if rows_s != ts:
        proj_s = jnp.pad(proj_s, ((0, 0), (0, rows_s - ts), (0, 0)))
    zeros_p = (jnp.zeros((bp, SUBLANES, GDN_QKV), F32), jnp.zeros((bp, 1, RWKV_COLS), F32),
               jnp.zeros((bp, GDN_HEADS, GDN_HEAD_DIM, GDN_HEAD_DIM), F32),
               jnp.zeros((bp, RWKV_HEADS, RWKV_HEAD_DIM, RWKV_HEAD_DIM), F32))
    mix_p, conv_p, shift_p, gdn_p, rwkv_p = _mixer(proj_p, tp, *zeros_p, mixer_params, BF16,
                                                   SEQS_PER_STEP_LONG)
    conv8 = jnp.pad(state_conv, ((0, 0), (SUBLANES - (CONV_K - 1), 0), (0, 0)))
    mix_s, conv_s, shift_s, gdn_s, rwkv_s = _mixer(
        proj_s, ts, conv8, state_shift[:, None, :], state_gdn, state_rwkv, mixer_params, F32, SEQS_PER_STEP_SHORT)

    ln1 = _pad_rows(jnp.stack([ln1_g, ln1_b]), SUBLANES)
    ln2 = _pad_rows(jnp.stack([ln2_g, ln2_b]), SUBLANES)
    wrt = _pad_rows(w_router.T, LANES)
    brc = _pad_rows(b_router[:, None], LANES)
    h2d, idx, rank, gates, cnt = _post_mixer(mix_p.reshape(n_p, d), mix_s.reshape(n_s, d), xp2d, xs2d,
                                             w_o.astype(BF16), ln1, wrt, brc)

    counts = cnt[:N_EXPERTS, 0]
    padded = (((counts + MOE_TILE - 1) // MOE_TILE) * MOE_TILE).astype(jnp.int32)
    pend = jnp.cumsum(padded).astype(jnp.int32)
    pstart = pend - padded
    n_blocks = -(-(n_tok * TOP_K) // MOE_TILE) + N_EXPERTS
    n_rows = n_blocks * MOE_TILE
    block_expert = jnp.minimum(
        jnp.sum(pend[None, :] <= (jnp.arange(n_blocks) * MOE_TILE)[:, None], axis=1), N_EXPERTS - 1).astype(jnp.int32)
    n_used = pend[-1:] // MOE_TILE

    dest = _slots(idx, rank, pstart)
    xb, row_slot = _dispatch(h2d, dest, pend, padded, counts.astype(jnp.int32), n_rows)
    y_slots = _experts(xb, row_slot, block_expert, n_used, n_tok * TOP_K,
                       w_gate, b_gate, w_up, b_up, w_down, b_down)
    y_p, y_s = _combine(y_slots, h2d, gates, ln2, n_p)

    return (y_p.reshape(bp, tp, d), y_s.reshape(bs, ts, d), conv_p, shift_p[:, 0], gdn_p, rwkv_p,
            conv_s, shift_s[:, 0], gdn_s, rwkv_s)


def kernel(x_prompt, x_sample, state_conv, state_shift, state_gdn, state_rwkv, w_in, conv_w, a_log, dt_bias,
           gdn_norm_w, mu_shift, w0, w2, a0, a2, g2, k_k, k_a, r_k, lnx_w, lnx_b, w_o, ln1_g, ln1_b,
           w_router, b_router, w_gate, b_gate, w_up, b_up, w_down, b_down, ln2_g, ln2_b):
    mp = _mixer_params(conv_w, a_log, dt_bias, gdn_norm_w, mu_shift, w0, w2, a0, a2, g2, k_k, k_a, r_k,
                       lnx_w, lnx_b)
    return _layer(x_prompt, x_sample, state_conv, state_shift, state_gdn, state_rwkv,
                  w_in, mp, w_o, ln1_g, ln1_b, w_router, b_router,
                  w_gate, b_gate, w_up, b_up, w_down, b_down, ln2_g, ln2_b)
```

```python
import functools
import math

import jax
import jax.numpy as jnp
from jax import lax
from jax.experimental import pallas as pl
from jax.experimental.pallas import tpu as pltpu

F32 = jnp.float32
BF16 = jnp.bfloat16

D_MODEL = 1024
GDN_HEADS = 4
GDN_HEAD_DIM = 128
GDN_WIDTH = GDN_HEADS * GDN_HEAD_DIM
GDN_QKV = 3 * GDN_WIDTH
CONV_K = 4
RWKV_HEADS = 8
RWKV_HEAD_DIM = 64
RWKV_WIDTH = RWKV_HEADS * RWKV_HEAD_DIM
DECAY_LORA = 64
AAA_LORA = 64
GATE_LORA = 128
RWKV_COLS = 3 * RWKV_WIDTH + DECAY_LORA + AAA_LORA + GATE_LORA
RWKV_GN_EPS = RWKV_HEAD_DIM * 1e-5
N_EXPERTS = 32
TOP_K = 4
SWIGLU_LIMIT = 7.0
SWIGLU_ALPHA = 1.702
DEPTH = 1
ALPHA = (2.0 * DEPTH) ** 0.25
LN_EPS = 1e-5

LANES = 128
SUBLANES = 8
MXU_COLS = 256
VMEM_LIMIT_BYTES = 56 * 1024 * 1024

COL_Z = GDN_QKV
COL_RWKV = COL_Z + GDN_WIDTH
COL_BA = COL_RWKV + RWKV_COLS
PROJ_COLS = COL_BA + LANES

CHUNK = 64
GROUP = 4
RWKV_GROUPS = RWKV_HEADS // GROUP
GROUP_W = GROUP * RWKV_HEAD_DIM
SEQS_PER_STEP_LONG = 4
SEQS_PER_STEP_SHORT = 8
PROJ_TILE = 512
TOK_TILE = 256
MOE_TILE = 512
DISPATCH_TILE = 512
COMBINE_TILE = 256
DMA_UNROLL = 8


class _Seqs:
    def __init__(self, vals):
        self.v = list(vals)

    def __getitem__(self, idx):
        return _Seqs([a[idx] for a in self.v])

    def __add__(self, o):
        return _lift(lambda a, b: a + b)(self, o)

    def __radd__(self, o):
        return _lift(lambda a, b: b + a)(self, o)

    def __sub__(self, o):
        return _lift(lambda a, b: a - b)(self, o)

    def __rsub__(self, o):
        return _lift(lambda a, b: b - a)(self, o)

    def __mul__(self, o):
        return _lift(lambda a, b: a * b)(self, o)

    def __rmul__(self, o):
        return _lift(lambda a, b: b * a)(self, o)

    def __neg__(self):
        return _Seqs([-a for a in self.v])

    @property
    def shape(self):
        return self.v[0].shape

    @property
    def T(self):
        return _Seqs([a.T for a in self.v])

    def astype(self, dt):
        return _Seqs([a.astype(dt) for a in self.v])


def _lift(f):
    def g(*args, **kw):
        n = next((len(a.v) for a in args if isinstance(a, _Seqs)), None)
        if n is None:
            return f(*args, **kw)
        return _Seqs([f(*[a.v[i] if isinstance(a, _Seqs) else a for a in args], **kw) for i in range(n)])
    return g


def _cat(parts, axis):
    n = next((len(a.v) for a in parts if isinstance(a, _Seqs)), None)
    if n is None:
        return jnp.concatenate(parts, axis=axis)
    return _Seqs([jnp.concatenate([a.v[i] if isinstance(a, _Seqs) else a for a in parts], axis=axis)
                  for i in range(n)])


_exp = _lift(jnp.exp)
_tanh = _lift(jnp.tanh)
_where = _lift(jnp.where)
_rsqrt = _lift(lax.rsqrt)
_sum = _lift(jnp.sum)
_roll = _lift(pltpu.roll)
_bcast = _lift(jnp.broadcast_to)


@_lift
def _mm(a, b):
    return jnp.dot(a.astype(BF16), b.astype(BF16), preferred_element_type=F32)


@_lift
def _mm_nt(a, b):
    return lax.dot_general(a.astype(BF16), b.astype(BF16), (((1,), (1,)), ((), ())), preferred_element_type=F32)


def _mm_tn(a, b):
    return _mm(a.T, b)


@_lift
def _mm_f32(a, b):
    return jnp.dot(a, b, precision=lax.Precision.HIGHEST, preferred_element_type=F32)


def _sigmoid(x):
    return 0.5 * _tanh(0.5 * x) + 0.5


@_lift
def _softplus(x):
    return jnp.maximum(x, 0.0) + jnp.log1p(jnp.exp(-jnp.abs(x)))


def _proj_kernel(x_ref, w_ref, o_ref):
    o_ref[...] = jnp.dot(x_ref[...].astype(BF16), w_ref[...], preferred_element_type=F32)


def _input_projection(x2d, w_in_bf16):
    n_tok = x2d.shape[0]
    tile = math.gcd(n_tok, PROJ_TILE)
    assert tile % SUBLANES == 0
    return pl.pallas_call(
        _proj_kernel,
        grid=(n_tok // tile,),
        in_specs=[pl.BlockSpec((tile, D_MODEL), lambda i: (i, 0)),
                  pl.BlockSpec((D_MODEL, PROJ_COLS), lambda i: (0, 0))],
        out_specs=pl.BlockSpec((tile, PROJ_COLS), lambda i: (i, 0)),
        out_shape=jax.ShapeDtypeStruct((n_tok, PROJ_COLS), F32),
        compiler_params=pltpu.CompilerParams(dimension_semantics=("parallel",),
                                             vmem_limit_bytes=VMEM_LIMIT_BYTES),
        name="input_projection",
    )(x2d, w_in_bf16)


def _shift_rows(cur, prev, s, row_ids):
    return _where(row_ids < s, _roll(prev, s, axis=0), _roll(cur, s, axis=0))


@_lift
def _block_stack(x, n, width):
    grp = lax.broadcasted_iota(jnp.int32, x.shape, 1) // width
    if x.shape[0] % (2 * SUBLANES) == 0:
        x16 = x.astype(BF16)
        return jnp.concatenate([x16 * jnp.where(grp == i, 1.0, 0.0).astype(BF16) for i in range(n)], axis=0)
    return jnp.concatenate([jnp.where(grp == i, x, 0.0) for i in range(n)], axis=0)


def _inv_series(nils, c, n):
    row = lax.broadcasted_iota(jnp.int32, nils[0].shape, 0)
    col = lax.broadcasted_iota(jnp.int32, nils[0].shape, 1) % c
    eye = jnp.where(row == col, 1.0, 0.0)
    xs = [eye + nil for nil in nils]
    qs = [_mm(nil, _block_stack(nil, n, c)) for nil in nils]
    levels = int(math.log2(c))
    for lvl in range(1, levels):
        nxt_q, nxt_x = [], []
        for q, x in zip(qs, xs):
            bd = _block_stack(q, n, c)
            if lvl == levels - 1:
                nxt_x.append(x + _mm(x, bd))
            else:
                both = _mm(_cat([q, x], 0), bd)
                nxt_q.append(both[0:c])
                nxt_x.append(x + both[c:2 * c])
        qs, xs = nxt_q, nxt_x
    return xs


def _head_sum(x, width):
    pieces = []
    for j in range(x.shape[1] // LANES):
        xb = x[:, j * LANES:(j + 1) * LANES]
        if width == LANES:
            pieces.append(_bcast(_sum(xb, axis=-1, keepdims=True), xb.shape))
        else:
            lo = lax.broadcasted_iota(jnp.int32, xb.shape, 1) < width
            s0 = _sum(_where(lo, xb, 0.0), axis=-1, keepdims=True)
            s1 = _sum(_where(lo, 0.0, xb), axis=-1, keepdims=True)
            pieces.append(_where(lo, s0, s1))
    return _cat(pieces, 1)


def _gdn_head(a, h):
    return a[:, h * GDN_HEAD_DIM:(h + 1) * GDN_HEAD_DIM]


def _gdn_prepare(x, prev, c, t_last, masks, convw_ref, gvec_ref):
    row_p, col_p, grp_p, lower_p, strict_p, tri = masks
    masked = t_last < c

    def vmask(a):
        if not masked:
            return a
        return _where(lax.broadcasted_iota(jnp.int32, a.shape, 0) < t_last, a, 0.0)

    xq = x[:, 0:GDN_QKV]
    pq = prev[:, 0:GDN_QKV]
    rid = lax.broadcasted_iota(jnp.int32, (c, GDN_QKV), 0)
    conv = _shift_rows(xq, pq, 3, rid) * convw_ref[0:1, :]
    conv = conv + _shift_rows(xq, pq, 2, rid) * convw_ref[1:2, :]
    conv = conv + _shift_rows(xq, pq, 1, rid) * convw_ref[2:3, :]
    conv = conv + xq * convw_ref[3:4, :]
    conv = conv * _sigmoid(conv)

    ba = x[:, COL_BA:COL_BA + LANES]
    beta_blk = vmask(_sigmoid(ba))
    g_blk = vmask(-jnp.exp(gvec_ref[0:1, :]) * _softplus(ba + gvec_ref[1:2, :]))

    qn, kn, kb, vb = [], [], [], []
    for h in range(GDN_HEADS):
        qh = _gdn_head(conv[:, 0:GDN_WIDTH], h)
        kh = _gdn_head(conv[:, GDN_WIDTH:2 * GDN_WIDTH], h)
        vh = vmask(_gdn_head(conv[:, 2 * GDN_WIDTH:GDN_QKV], h))
        qh = qh * (_rsqrt(_sum(qh * qh, axis=-1, keepdims=True) + 1e-6) * (GDN_HEAD_DIM ** -0.5))
        kh = vmask(kh * _rsqrt(_sum(kh * kh, axis=-1, keepdims=True) + 1e-6))
        bh = beta_blk[:, h:h + 1]
        qn.append(qh), kn.append(kh), kb.append(kh * bh), vb.append(vh * bh)

    ma = _mm_nt(_cat([_cat(kb, 1), _cat(qn, 1)], 0),
                _block_stack(_cat(kn, 1), GDN_HEADS, GDN_HEAD_DIM))
    gexp = _where(grp_p == 0, g_blk[:, GDN_HEADS:GDN_HEADS + 1], 0.0)
    for h in range(1, GDN_HEADS):
        gexp = _where(grp_p == h, g_blk[:, GDN_HEADS + h:GDN_HEADS + h + 1], gexp)
    gcol = _mm_f32(tri, gexp)
    grow = _sum(_where(row_p == col_p, gcol, 0.0), axis=0, keepdims=True)
    decay = _exp(_where(lower_p, gcol - grow, -jnp.inf))
    m_p = _where(strict_p, ma[0:c] * decay, 0.0)
    attn = ma[c:2 * c] * decay
    egc, kdec, glast = [], [], []
    for h in range(GDN_HEADS):
        gch = gcol[:, h * c:h * c + 1]
        glh = gcol[c - 1:c, h * c:h * c + 1]
        egc.append(_exp(gch))
        kdec.append(kn[h] * _exp(glh - gch))
        glast.append(_exp(glh))
    rhs2 = _cat([_block_stack(_cat(vb, 1), GDN_HEADS, GDN_HEAD_DIM),
                 _block_stack(_cat([kb[h] * egc[h] for h in range(GDN_HEADS)], 1), GDN_HEADS, GDN_HEAD_DIM)], 1)
    qdec = [qn[h] * egc[h] for h in range(GDN_HEADS)]
    return dict(nil=-m_p, attn=attn, rhs2=rhs2, qdec=qdec, kdec=kdec, glast=glast)


def _gdn_finish(p, tinv, x, c, states, gvec_ref):
    uw = _mm(tinv, p["rhs2"])
    vnew, o1 = [], []
    for h in range(GDN_HEADS):
        ws = _mm(_cat([_gdn_head(uw[:, GDN_WIDTH:2 * GDN_WIDTH], h), p["qdec"][h]], 0), states[h])
        vnew.append(_gdn_head(uw[:, 0:GDN_WIDTH], h) - ws[0:c])
        o1.append(ws[c:2 * c])
    o2 = _mm(p["attn"], _block_stack(_cat(vnew, 1), GDN_HEADS, GDN_HEAD_DIM))
    out, new_states = [], []
    for h in range(GDN_HEADS):
        new_states.append(states[h] * p["glast"][h] + _mm_tn(p["kdec"][h], vnew[h]))
        oh = o1[h] + _gdn_head(o2, h)
        zh = _gdn_head(x[:, COL_Z:COL_Z + GDN_WIDTH], h)
        oh = oh * _rsqrt(_sum(oh * oh, axis=-1, keepdims=True) * (1.0 / GDN_HEAD_DIM) + 1e-6) * gvec_ref[2:3, :]
        out.append(oh * (zh * _sigmoid(zh)))
    return out, new_states


def _rwkv_prepare(x, prev, c, t_last, masks, mu_ref, rvec_ref, loraw_ref, g2_ref):
    row_p, col_p, grp_p, lower_p, strict_p, tri = masks
    masked = t_last < c
    gc4 = GROUP * c

    def vmask(a):
        if not masked:
            return a
        return _where(lax.broadcasted_iota(jnp.int32, a.shape, 0) < t_last, a, 0.0)

    rw = x[:, COL_RWKV:COL_BA]
    rid = lax.broadcasted_iota(jnp.int32, (c, RWKV_COLS), 0)
    prev_row = _shift_rows(rw, prev[:, COL_RWKV:COL_BA], 1, rid)
    rs = rw + (prev_row - rw) * mu_ref[...]
    o1_, o2_, o3_ = RWKV_WIDTH, 2 * RWKV_WIDTH, 3 * RWKV_WIDTH
    r = rs[:, 0:o1_]
    kr = rs[:, o1_:o2_]
    vr = rs[:, o2_:o3_]
    la = rs[:, o3_:o3_ + LANES]
    gl = rs[:, o3_ + LANES:o3_ + 2 * LANES]
    lane1 = lax.broadcasted_iota(jnp.int32, (c, LANES), 1)
    wa = _mm(_where(lane1 < DECAY_LORA, _tanh(la), la), loraw_ref[...])
    logw = vmask(-math.exp(-0.5) * _sigmoid(rvec_ref[0:1, :] + wa[:, 0:RWKV_WIDTH]))
    a = _sigmoid(rvec_ref[1:2, :] + wa[:, RWKV_WIDTH:2 * RWKV_WIDTH])
    gate = _mm(_sigmoid(gl), g2_ref[...])
    kkr = kr * rvec_ref[2:3, :]
    kk = vmask(kkr * _rsqrt(_head_sum(kkr * kkr, RWKV_HEAD_DIM) + 1e-6))
    kr2 = vmask(kr * (1.0 + (a - 1.0) * rvec_ref[3:4, :]))
    gcum = _mm_f32(tri, logw)
    e_pos = _exp(gcum)
    e_neg = _exp(-gcum)
    a_t = -kk * _exp(gcum - logw)
    b_t = kk * a * e_neg
    k_t = kr2 * e_neg
    r_t = r * e_pos
    groups = []
    for g in range(RWKV_GROUPS):
        sl = slice(g * GROUP_W, (g + 1) * GROUP_W)
        at_g, bt_g, kt_g, rt_g, v_g = a_t[:, sl], b_t[:, sl], k_t[:, sl], r_t[:, sl], vr[:, sl]
        aa = _mm_nt(_cat([at_g, rt_g], 0),
                    _cat([_block_stack(bt_g, GROUP, RWKV_HEAD_DIM), _block_stack(kt_g, GROUP, RWKV_HEAD_DIM)], 0))
        a_ab = _where(strict_p, aa[0:c, 0:gc4], 0.0)
        a_ak = _where(strict_p, aa[0:c, gc4:2 * gc4], 0.0)
        a_rb = _where(lower_p, aa[c:2 * c, 0:gc4], 0.0)
        a_rk = _where(lower_p, aa[c:2 * c, gc4:2 * gc4], 0.0)
        akv = _mm(a_ak, _block_stack(v_g, GROUP, RWKV_HEAD_DIM))
        groups.append(dict(nil=a_ab, at=at_g, bt=bt_g, kt=kt_g, rt=rt_g, v=v_g, akv=akv,
                           a_r=_cat([a_rb, a_rk], 1), e_last=e_pos[c - 1:c, sl]))
    bonus = _head_sum(r * kr2 * rvec_ref[4:5, :], RWKV_HEAD_DIM) * vr
    return dict(groups=groups, gate=gate, bonus=bonus)


def _rwkv_finish(p, tinvs, c, states, rvec_ref):
    bd_mask = (lax.broadcasted_iota(jnp.int32, (GROUP_W, GROUP_W), 0) // RWKV_HEAD_DIM
               == lax.broadcasted_iota(jnp.int32, (GROUP_W, GROUP_W), 1) // RWKV_HEAD_DIM)
    qs = p["groups"]
    rng = range(RWKV_GROUPS)
    wu0 = [_mm(tinvs[g], _cat([_block_stack(qs[g]["at"], GROUP, RWKV_HEAD_DIM),
                               _block_stack(qs[g]["akv"], GROUP, RWKV_HEAD_DIM)], 1)) for g in rng]
    wr = [_mm_nt(_cat([wu0[g][:, 0:GROUP_W], qs[g]["rt"]], 0), states[g]) for g in rng]
    u = [wr[g][0:c] + wu0[g][:, GROUP_W:2 * GROUP_W] for g in rng]
    o_g = [wr[g][c:2 * c] + _mm(qs[g]["a_r"], _cat([_block_stack(u[g], GROUP, RWKV_HEAD_DIM),
                                                    _block_stack(qs[g]["v"], GROUP, RWKV_HEAD_DIM)], 0)) for g in rng]
    upd = [_mm_tn(_cat([u[g], qs[g]["v"]], 0), _cat([qs[g]["bt"], qs[g]["kt"]], 0)) for g in rng]
    new_states = [(states[g] + _where(bd_mask, upd[g], 0.0)) * qs[g]["e_last"] for g in rng]
    o_r = _cat(o_g, 1)
    mean_o = _head_sum(o_r, RWKV_HEAD_DIM) * (1.0 / RWKV_HEAD_DIM)
    dev = o_r - mean_o
    var_o = _head_sum(dev * dev, RWKV_HEAD_DIM) * (1.0 / RWKV_HEAD_DIM)
    o_n = dev * _rsqrt(var_o + RWKV_GN_EPS) * rvec_ref[5:6, :] + rvec_ref[6:7, :]
    return (o_n + p["bonus"]) * p["gate"], new_states


def _mixer_kernel(c, rows_in, t_out, t_last, n_chunks, nb,
                  proj_ref, cinit_ref, sinit_ref, sg0_ref, sr0_ref,
                  convw_ref, gvec_ref, mu_ref, rvec_ref, loraw_ref, g2_ref,
                  mix_ref, convo_ref, shifto_ref, sgo_ref, sro_ref,
                  prev_scr, sg_scr, sr_scr):
    ci = pl.program_id(1)
    gc4 = GROUP * c
    hd = RWKV_HEAD_DIM

    @pl.when(ci == 0)
    def _():
        prev_scr[...] = jnp.zeros(prev_scr.shape, F32)
        for j in range(nb):
            prev_scr[j, c - SUBLANES:c, 0:GDN_QKV] = cinit_ref[j]
            prev_scr[j, c - 1:c, COL_RWKV:COL_BA] = sinit_ref[j]
            for g in range(RWKV_GROUPS):
                blocks = []
                for h in range(GROUP):
                    parts = [sr0_ref[j, g * GROUP + h]]
                    if h > 0:
                        parts.insert(0, jnp.zeros((hd, h * hd), F32))
                    if h < GROUP - 1:
                        parts.append(jnp.zeros((hd, (GROUP - 1 - h) * hd), F32))
                    blocks.append(jnp.concatenate(parts, axis=1))
                sr_scr[j, g] = jnp.concatenate(blocks, axis=0)
        sg_scr[...] = sg0_ref[...]

    row_p = lax.broadcasted_iota(jnp.int32, (c, gc4), 0)
    col_p = lax.broadcasted_iota(jnp.int32, (c, gc4), 1) % c
    grp_p = lax.broadcasted_iota(jnp.int32, (c, gc4), 1) // c
    tri = jnp.where(lax.broadcasted_iota(jnp.int32, (c, c), 1) <= lax.broadcasted_iota(jnp.int32, (c, c), 0),
                    1.0, 0.0)
    masks = (row_p, col_p, grp_p, col_p <= row_p, col_p < row_p, tri)

    def load_x(j):
        x = proj_ref[j]
        if rows_in < c:
            x = jnp.concatenate([x, jnp.zeros((c - rows_in, PROJ_COLS), F32)], axis=0)
        return x

    s_gdn = [_Seqs([sg_scr[j, h] for j in range(nb)]) for h in range(GDN_HEADS)]
    s_rwkv = [_Seqs([sr_scr[j, g] for j in range(nb)]) for g in range(RWKV_GROUPS)]
    x = _Seqs([load_x(j) for j in range(nb)])
    prev = _Seqs([prev_scr[j] for j in range(nb)])

    gdn = _gdn_prepare(x, prev, c, t_last, masks, convw_ref, gvec_ref)
    rwkv = _rwkv_prepare(x, prev, c, t_last, masks, mu_ref, rvec_ref, loraw_ref, g2_ref)
    tinvs = _inv_series([gdn["nil"]] + [q["nil"] for q in rwkv["groups"]], c, GROUP)
    o_gdn, new_g = _gdn_finish(gdn, tinvs[0], x, c, s_gdn, gvec_ref)
    o_rwkv, new_r = _rwkv_finish(rwkv, tinvs[1:], c, s_rwkv, rvec_ref)
    mix = _cat(o_gdn + [o_rwkv], 1)

    for j in range(nb):
        mix_ref[j] = mix.v[j][0:t_out].astype(mix_ref.dtype)
        prev_scr[j] = x.v[j][:, 0:COL_BA]
        for h in range(GDN_HEADS):
            sg_scr[j, h] = new_g[h].v[j]
        for g in range(RWKV_GROUPS):
            sr_scr[j, g] = new_r[g].v[j]

    @pl.when(ci == n_chunks - 1)
    def _():
        for j in range(nb):
            convo_ref[j] = x.v[j][t_last - (CONV_K - 1):t_last, 0:GDN_QKV]
            shifto_ref[j] = x.v[j][t_last - 1:t_last, COL_RWKV:COL_BA]
            for g in range(RWKV_GROUPS):
                s_g = new_r[g].v[j]
                for h in range(GROUP):
                    sro_ref[j, g * GROUP + h] = s_g[h * hd:(h + 1) * hd, h * hd:(h + 1) * hd]
            for h in range(GDN_HEADS):
                sgo_ref[j, h] = new_g[h].v[j]


def _mixer(proj3d, seq_len, conv_init8, shift_init, s_gdn, s_rwkv, mp, mix_dtype, nb):
    bsz, rows, _ = proj3d.shape
    nb = math.gcd(bsz, nb)
    if rows >= CHUNK:
        assert rows == seq_len and seq_len % CHUNK == 0
        c = CHUNK
        n_chunks, rows_in, t_out, t_last = seq_len // c, c, c, c
    else:
        c = rows
        n_chunks, rows_in, t_out, t_last = 1, rows, seq_len, seq_len
    assert t_last >= CONV_K - 1 and c % SUBLANES == 0
    const2 = lambda b, i: (0, 0)
    perb3 = lambda b, i: (b, 0, 0)
    perb4 = lambda b, i: (b, 0, 0, 0)
    kern = functools.partial(_mixer_kernel, c, rows_in, t_out, t_last, n_chunks, nb)
    state_g = (nb, GDN_HEADS, GDN_HEAD_DIM, GDN_HEAD_DIM)
    state_r = (nb, RWKV_HEADS, RWKV_HEAD_DIM, RWKV_HEAD_DIM)
    return pl.pallas_call(
        kern,
        grid=(bsz // nb, n_chunks),
        in_specs=[pl.BlockSpec((nb, rows_in, PROJ_COLS), lambda b, i: (b, i, 0)),
                  pl.BlockSpec((nb, SUBLANES, GDN_QKV), perb3),
                  pl.BlockSpec((nb, 1, RWKV_COLS), perb3),
                  pl.BlockSpec(state_g, perb4),
                  pl.BlockSpec(state_r, perb4),
                  pl.BlockSpec((CONV_K, GDN_QKV), const2),
                  pl.BlockSpec((SUBLANES, LANES), const2),
                  pl.BlockSpec((1, RWKV_COLS), const2),
                  pl.BlockSpec((SUBLANES, RWKV_WIDTH), const2),
                  pl.BlockSpec((LANES, 2 * RWKV_WIDTH), const2),
                  pl.BlockSpec((GATE_LORA, RWKV_WIDTH), const2)],
        out_specs=[pl.BlockSpec((nb, t_out, D_MODEL), lambda b, i: (b, i, 0)),
                   pl.BlockSpec((nb, CONV_K - 1, GDN_QKV), perb3),
                   pl.BlockSpec((nb, 1, RWKV_COLS), perb3),
                   pl.BlockSpec(state_g, perb4),
                   pl.BlockSpec(state_r, perb4)],
        out_shape=[jax.ShapeDtypeStruct((bsz, seq_len, D_MODEL), mix_dtype),
                   jax.ShapeDtypeStruct((bsz, CONV_K - 1, GDN_QKV), F32),
                   jax.ShapeDtypeStruct((bsz, 1, RWKV_COLS), F32),
                   jax.ShapeDtypeStruct((bsz,) + state_g[1:], F32),
                   jax.ShapeDtypeStruct((bsz,) + state_r[1:], F32)],
        scratch_shapes=[pltpu.VMEM((nb, c, COL_BA), F32),
                        pltpu.VMEM(state_g, F32),
                        pltpu.VMEM((nb, RWKV_GROUPS, GROUP_W, GROUP_W), F32)],
        compiler_params=pltpu.CompilerParams(dimension_semantics=("parallel", "arbitrary"),
                                             vmem_limit_bytes=VMEM_LIMIT_BYTES),
        name="sequence_mixers",
    )(proj3d, conv_init8, shift_init, s_gdn, s_rwkv,
      mp["conv_w"], mp["gvec"], mp["mu"], mp["rvec"], mp["lora_w"], mp["g2"])


def _layernorm(x, g, b):
    mu = jnp.mean(x, axis=-1, keepdims=True)
    d = x - mu
    var = jnp.mean(d * d, axis=-1, keepdims=True)
    return d * lax.rsqrt(var + LN_EPS) * g + b


def _post_mixer_kernel(n_first, mixp_ref, mixs_ref, xp_ref, xs_ref, wo_ref, lnv_ref, wrt_ref, br_ref,
                       h_ref, idx_ref, rank_ref, gate_ref, cnt_ref, base_scr, hp_scr):
    i = pl.program_id(0)
    tt = h_ref.shape[0]

    @pl.when(i == 0)
    def _():
        base_scr[...] = jnp.zeros(base_scr.shape, F32)

    @pl.when(i < n_first)
    def _():
        hp_scr[...] = ALPHA * xp_ref[...] + jnp.dot(mixp_ref[...].astype(BF16), wo_ref[...],
                                                    preferred_element_type=F32)

    @pl.when(i >= n_first)
    def _():
        hp_scr[...] = ALPHA * xs_ref[...] + jnp.dot(mixs_ref[...].astype(BF16), wo_ref[...],
                                                    preferred_element_type=F32)

    h = _layernorm(hp_scr[...], lnv_ref[0:1, :], lnv_ref[1:2, :])
    h_ref[...] = h
    lt = lax.dot_general(wrt_ref[...], h, (((1,), (1,)), ((), ())),
                         precision=lax.Precision.HIGHEST, preferred_element_type=F32) + br_ref[...]
    eid = lax.broadcasted_iota(jnp.int32, lt.shape, 0)
    lt = jnp.where(eid < N_EXPERTS, lt, -jnp.inf)
    upper = jnp.where(lax.broadcasted_iota(jnp.int32, (tt, tt), 0) < lax.broadcasted_iota(jnp.int32, (tt, tt), 1),
                      1.0, 0.0).astype(BF16)
    base = base_scr[...]
    vals, idxs, ranks = [], [], []
    for _ in range(TOP_K):
        m = jnp.max(lt, axis=0, keepdims=True)
        sel = jnp.min(jnp.where(lt == m, eid, LANES), axis=0, keepdims=True)
        onehot = eid == sel
        lt = jnp.where(onehot, -jnp.inf, lt)
        oh = jnp.where(onehot, 1.0, 0.0)
        before = jnp.dot(oh.astype(BF16), upper, preferred_element_type=F32)
        ranks.append(jnp.sum(oh * (base + before), axis=0, keepdims=True))
        base = base + jnp.sum(oh, axis=1, keepdims=True)
        vals.append(m)
        idxs.append(sel)
    base_scr[...] = base
    ex = [jnp.exp(v - vals[0]) for v in vals]
    den = ex[0] + ex[1] + ex[2] + ex[3]
    pad_i = jnp.zeros((SUBLANES - TOP_K, tt), jnp.int32)
    idx_ref[...] = jnp.concatenate(idxs + [pad_i], axis=0)
    rank_ref[...] = jnp.concatenate([rk.astype(jnp.int32) for rk in ranks] + [pad_i], axis=0)
    gates = jnp.concatenate([e / den for e in ex] + [jnp.zeros((SUBLANES - TOP_K, tt), F32)], axis=0)
    gate_ref[...] = gates.T

    @pl.when(i == pl.num_programs(0) - 1)
    def _():
        cnt_ref[...] = base[:, 0:LANES].astype(jnp.int32)


def _post_mixer(mix_p, mix_s, x_p, x_s, w_o_bf16, ln1, w_router_t, b_router_col):
    tt = TOK_TILE
    n_p, n_s = x_p.shape[0], x_s.shape[0]
    assert n_p % tt == 0 and n_s % tt == 0
    n1, n2 = n_p // tt, n_s // tt
    n_tok = n_p + n_s
    const2 = lambda i: (0, 0)
    first = lambda i: (jnp.minimum(i, n1 - 1), 0)
    second = lambda i: (jnp.maximum(i - n1, 0), 0)
    return pl.pallas_call(
        functools.partial(_post_mixer_kernel, n1),
        grid=(n1 + n2,),
        in_specs=[pl.BlockSpec((tt, D_MODEL), first),
                  pl.BlockSpec((tt, D_MODEL), second),
                  pl.BlockSpec((tt, D_MODEL), first),
                  pl.BlockSpec((tt, D_MODEL), second),
                  pl.BlockSpec((D_MODEL, D_MODEL), const2),
                  pl.BlockSpec((SUBLANES, D_MODEL), const2),
                  pl.BlockSpec((LANES, D_MODEL), const2),
                  pl.BlockSpec((LANES, 1), const2)],
        out_specs=[pl.BlockSpec((tt, D_MODEL), lambda i: (i, 0)),
                   pl.BlockSpec((SUBLANES, tt), lambda i: (0, i)),
                   pl.BlockSpec((SUBLANES, tt), lambda i: (0, i)),
                   pl.BlockSpec((tt, SUBLANES), lambda i: (i, 0)),
                   pl.BlockSpec((LANES, LANES), const2)],
        out_shape=[jax.ShapeDtypeStruct((n_tok, D_MODEL), F32),
                   jax.ShapeDtypeStruct((SUBLANES, n_tok), jnp.int32),
                   jax.ShapeDtypeStruct((SUBLANES, n_tok), jnp.int32),
                   jax.ShapeDtypeStruct((n_tok, SUBLANES), F32),
                   jax.ShapeDtypeStruct((LANES, LANES), jnp.int32)],
        scratch_shapes=[pltpu.VMEM((LANES, tt), F32), pltpu.VMEM((tt, D_MODEL), F32)],
        compiler_params=pltpu.CompilerParams(dimension_semantics=("arbitrary",),
                                             vmem_limit_bytes=VMEM_LIMIT_BYTES),
        name="outproj_norm_router",
    )(mix_p, mix_s, x_p, x_s, w_o_bf16, ln1, w_router_t, b_router_col)


def _slot_kernel(pstart_ref, idx_ref, rank_ref, dest_ref):
    idx = idx_ref[...]
    dest = rank_ref[...]
    for e in range(N_EXPERTS):
        dest = dest + jnp.where(idx == e, pstart_ref[e], 0)
    dest_ref[...] = dest


def _slots(idx, rank, pstart):
    grid_spec = pltpu.PrefetchScalarGridSpec(
        num_scalar_prefetch=1,
        grid=(1,),
        in_specs=[pl.BlockSpec(idx.shape, lambda i, ps: (0, 0)),
                  pl.BlockSpec(idx.shape, lambda i, ps: (0, 0))],
        out_specs=pl.BlockSpec(idx.shape, lambda i, ps: (0, 0)),
    )
    return pl.pallas_call(
        _slot_kernel,
        grid_spec=grid_spec,
        out_shape=jax.ShapeDtypeStruct(idx.shape, jnp.int32),
        name="moe_slots",
    )(pstart, idx, rank)


def _dispatch_kernel(pend_ref, padded_ref, dest_ref, h_ref, xb_hbm, zero_scr, sem):
    i = pl.program_id(0)
    tt = h_ref.shape[0]
    tm = zero_scr.shape[0]

    @pl.when(i == 0)
    def _():
        zero_scr[...] = jnp.zeros(zero_scr.shape, F32)

        def tail_copy(e):
            start = pl.multiple_of(pend_ref[e] - tm, tm)
            return pltpu.make_async_copy(zero_scr, xb_hbm.at[pl.ds(start, tm), :], sem)

        for e in range(N_EXPERTS):
            @pl.when(padded_ref[e] > 0)
            def _():
                tail_copy(e).start()
        for e in range(N_EXPERTS):
            @pl.when(padded_ref[e] > 0)
            def _():
                tail_copy(e).wait()

        def spare_copy(b):
            return pltpu.make_async_copy(zero_scr, xb_hbm.at[pl.ds(pl.multiple_of(b * tm, tm), tm), :], sem)

        def spare_start(b, carry):
            spare_copy(b).start()
            return carry

        def spare_wait(b, carry):
            spare_copy(b).wait()
            return carry

        first_spare = pend_ref[N_EXPERTS - 1] // tm
        lax.fori_loop(first_spare, xb_hbm.shape[0] // tm, spare_start, 0)
        lax.fori_loop(first_spare, xb_hbm.shape[0] // tm, spare_wait, 0)

    def row_copy(t, k):
        return pltpu.make_async_copy(h_ref.at[pl.ds(t, 1), :], xb_hbm.at[pl.ds(dest_ref[k, t], 1), :], sem)

    def issue(t, carry):
        for k in range(TOP_K):
            row_copy(t, k).start(priority=k % 2)
        return carry

    lax.fori_loop(0, tt, issue, 0, unroll=DMA_UNROLL)

    def drain(t, carry):
        for k in range(TOP_K):
            row_copy(t, k).wait()
        return carry

    lax.fori_loop(0, tt, drain, 0, unroll=DMA_UNROLL)


def _dispatch(h2d, dest, pend, padded, n_rows):
    n_tok = h2d.shape[0]
    tt = math.gcd(n_tok, DISPATCH_TILE)
    assert tt % LANES == 0
    grid_spec = pltpu.PrefetchScalarGridSpec(
        num_scalar_prefetch=2,
        grid=(n_tok // tt,),
        in_specs=[pl.BlockSpec((SUBLANES, tt), lambda i, pe, pa: (0, i), memory_space=pltpu.SMEM),
                  pl.BlockSpec((tt, D_MODEL), lambda i, pe, pa: (i, 0))],
        out_specs=pl.BlockSpec(memory_space=pl.ANY),
        scratch_shapes=[pltpu.VMEM((MOE_TILE, D_MODEL), F32), pltpu.SemaphoreType.DMA(())],
    )
    return pl.pallas_call(
        _dispatch_kernel,
        grid_spec=grid_spec,
        out_shape=jax.ShapeDtypeStruct((n_rows, D_MODEL), F32),
        compiler_params=pltpu.CompilerParams(dimension_semantics=("arbitrary",)),
        name="moe_dispatch",
    )(pend, padded, dest, h2d)


def _expert_kernel(be_ref, nused_ref, x_ref, wg_ref, wu_ref, wd_ref, bg_ref, bu_ref, bd_ref, y_ref,
                   wg16, wu16, wd16):
    i = pl.program_id(0)
    prev_e = be_ref[jnp.maximum(i - 1, 0)]
    fresh = jnp.logical_or(i == 0, be_ref[i] != prev_e)
    used = i < nused_ref[0]

    @pl.when(jnp.logical_and(fresh, used))
    def _():
        wg16[...] = wg_ref[0].astype(BF16)
        wu16[...] = wu_ref[0].astype(BF16)
        wd16[...] = wd_ref[0].astype(BF16)

    @pl.when(used)
    def _():
        x16 = x_ref[...].astype(BF16)
        hid = []
        for cc in range(wg16.shape[1] // MXU_COLS):
            sl = slice(cc * MXU_COLS, (cc + 1) * MXU_COLS)
            gt = jnp.dot(x16, wg16[:, sl], preferred_element_type=F32) + bg_ref[0, :, sl]
            up = jnp.dot(x16, wu16[:, sl], preferred_element_type=F32) + bu_ref[0, :, sl]
            gt = jnp.minimum(gt, SWIGLU_LIMIT)
            up = jnp.clip(up, -SWIGLU_LIMIT, SWIGLU_LIMIT)
            hid.append(((up + 1.0) * gt * _sigmoid(SWIGLU_ALPHA * gt)).astype(BF16))
        y_ref[...] = jnp.dot(jnp.concatenate(hid, axis=1), wd16[...], preferred_element_type=F32) + bd_ref[0]

    @pl.when(jnp.logical_not(used))
    def _():
        y_ref[...] = jnp.zeros(y_ref.shape, F32)


def _experts(xb, block_expert, n_used, w_gate, b_gate, w_up, b_up, w_down, b_down):
    n_rows = xb.shape[0]
    tm = MOE_TILE
    n_blocks = n_rows // tm
    d_e = w_gate.shape[2]
    wspec = lambda shape: pl.BlockSpec(shape, lambda i, be, nu: (be[i], 0, 0))
    grid_spec = pltpu.PrefetchScalarGridSpec(
        num_scalar_prefetch=2,
        grid=(n_blocks,),
        in_specs=[pl.BlockSpec((tm, D_MODEL), lambda i, be, nu: (jnp.maximum(jnp.minimum(i, nu[0] - 1), 0), 0)),
                  wspec((1, D_MODEL, d_e)), wspec((1, D_MODEL, d_e)), wspec((1, d_e, D_MODEL)),
                  wspec((1, 1, d_e)), wspec((1, 1, d_e)), wspec((1, 1, D_MODEL))],
        out_specs=pl.BlockSpec((tm, D_MODEL), lambda i, be, nu: (i, 0)),
        scratch_shapes=[pltpu.VMEM((D_MODEL, d_e), BF16), pltpu.VMEM((D_MODEL, d_e), BF16),
                        pltpu.VMEM((d_e, D_MODEL), BF16)],
    )
    return pl.pallas_call(
        _expert_kernel,
        grid_spec=grid_spec,
        out_shape=jax.ShapeDtypeStruct((n_rows, D_MODEL), F32),
        compiler_params=pltpu.CompilerParams(dimension_semantics=("arbitrary",),
                                             vmem_limit_bytes=VMEM_LIMIT_BYTES),
        name="moe_experts",
    )(block_expert, n_used, xb, w_gate, w_up, w_down,
      b_gate[:, None, :], b_up[:, None, :], b_down[:, None, :])


def _combine_kernel(n_first, dest_ref, h_ref, gate_ref, lnv_ref, yb_hbm, yp_ref, ys_ref, buf, sem):
    i = pl.program_id(0)
    tt = h_ref.shape[0]

    def row_copy(t, k):
        return pltpu.make_async_copy(yb_hbm.at[pl.ds(dest_ref[k, t], 1), :], buf.at[k, pl.ds(t, 1), :], sem)

    def issue(t, carry):
        for k in range(TOP_K):
            row_copy(t, k).start(priority=k % 2)
        return carry

    lax.fori_loop(0, tt, issue, 0, unroll=DMA_UNROLL)

    def drain(t, carry):
        for k in range(TOP_K):
            row_copy(t, k).wait()
        return carry

    lax.fori_loop(0, tt, drain, 0, unroll=DMA_UNROLL)
    gates = gate_ref[...]
    f = buf[0] * gates[:, 0:1]
    for k in range(1, TOP_K):
        f = f + buf[k] * gates[:, k:k + 1]
    y = _layernorm(ALPHA * h_ref[...] + f, lnv_ref[0:1, :], lnv_ref[1:2, :])

    @pl.when(i < n_first)
    def _():
        yp_ref[...] = y

    @pl.when(i >= n_first)
    def _():
        ys_ref[...] = y


def _combine(yb, h2d, dest, gates, ln2, n_p):
    n_tok = h2d.shape[0]
    tt = COMBINE_TILE
    n_s = n_tok - n_p
    assert n_p % tt == 0 and n_s % tt == 0
    n1, n2 = n_p // tt, n_s // tt
    return pl.pallas_call(
        functools.partial(_combine_kernel, n1),
        grid=(n1 + n2,),
        in_specs=[pl.BlockSpec((SUBLANES, tt), lambda i: (0, i), memory_space=pltpu.SMEM),
                  pl.BlockSpec((tt, D_MODEL), lambda i: (i, 0)),
                  pl.BlockSpec((tt, SUBLANES), lambda i: (i, 0)),
                  pl.BlockSpec((SUBLANES, D_MODEL), lambda i: (0, 0)),
                  pl.BlockSpec(memory_space=pl.ANY)],
        out_specs=[pl.BlockSpec((tt, D_MODEL), lambda i: (jnp.minimum(i, n1 - 1), 0)),
                   pl.BlockSpec((tt, D_MODEL), lambda i: (jnp.maximum(i - n1, 0), 0))],
        out_shape=[jax.ShapeDtypeStruct((n_p, D_MODEL), F32),
                   jax.ShapeDtypeStruct((n_s, D_MODEL), F32)],
        scratch_shapes=[pltpu.VMEM((TOP_K, tt, D_MODEL), F32), pltpu.SemaphoreType.DMA(())],
        compiler_params=pltpu.CompilerParams(dimension_semantics=("arbitrary",),
                                             vmem_limit_bytes=VMEM_LIMIT_BYTES),
        name="moe_combine_norm",
    )(dest, h2d, gates, ln2, yb)


def _pad_rows(v, rows):
    return jnp.concatenate([v, jnp.zeros((rows - v.shape[0],) + v.shape[1:], v.dtype)], axis=0)


def _mixer_params(conv_w, a_log, dt_bias, gdn_norm_w, mu_shift, w0, w2, a0, a2, g2, k_k, k_a, r_k, lnx_w, lnx_b):
    gvec = jnp.zeros((SUBLANES, LANES), F32)
    gvec = gvec.at[0, GDN_HEADS:2 * GDN_HEADS].set(a_log).at[1, GDN_HEADS:2 * GDN_HEADS].set(dt_bias)
    gvec = gvec.at[2, :].set(gdn_norm_w)
    rvec = _pad_rows(jnp.stack([w0, a0, k_k, k_a, r_k.reshape(-1), lnx_w, lnx_b]), SUBLANES)
    lora_w = jnp.zeros((LANES, 2 * RWKV_WIDTH), F32)
    lora_w = lora_w.at[0:DECAY_LORA, 0:RWKV_WIDTH].set(w2).at[DECAY_LORA:, RWKV_WIDTH:].set(a2)
    return dict(conv_w=conv_w, gvec=gvec, mu=mu_shift[None, :], rvec=rvec,
                lora_w=lora_w.astype(BF16), g2=g2.astype(BF16))


def _layer(x_prompt, x_sample, state_conv, state_shift, state_gdn, state_rwkv,
           w_in, mixer_params, w_o, ln1_g, ln1_b, w_router, b_router,
           w_gate, b_gate, w_up, b_up, w_down, b_down, ln2_g, ln2_b):
    bp, tp, d = x_prompt.shape
    bs, ts, _ = x_sample.shape
    n_p, n_s = bp * tp, bs * ts
    n_tok = n_p + n_s
    xp2d = x_prompt.reshape(n_p, d)
    xs2d = x_sample.reshape(n_s, d)

    in_cols = w_in.shape[1]
    off_ba = COL_Z + GDN_WIDTH
    w_in_r = jnp.concatenate([w_in[:, :off_ba], w_in[:, off_ba + 2 * GDN_HEADS:],
                              w_in[:, off_ba:off_ba + 2 * GDN_HEADS],
                              jnp.zeros((d, PROJ_COLS - in_cols), w_in.dtype)], axis=1).astype(BF16)

    rows_s = -(-ts // SUBLANES) * SUBLANES
    proj_p = _input_projection(xp2d, w_in_r).reshape(bp, tp, PROJ_COLS)
    proj_s = _input_projection(xs2d, w_in_r).reshape(bs, ts, PROJ_COLS)
    if rows_s != ts:
        proj_s = jnp.pad(proj_s, ((0, 0), (0, rows_s - ts), (0, 0)))
    zeros_p = (jnp.zeros((bp, SUBLANES, GDN_QKV), F32), jnp.zeros((bp, 1, RWKV_COLS), F32),
               jnp.zeros((bp, GDN_HEADS, GDN_HEAD_DIM, GDN_HEAD_DIM), F32),
               jnp.zeros((bp, RWKV_HEADS, RWKV_HEAD_DIM, RWKV_HEAD_DIM), F32))
    mix_p, conv_p, shift_p, gdn_p, rwkv_p = _mixer(proj_p, tp, *zeros_p, mixer_params, BF16,
                                                   SEQS_PER_STEP_LONG)
    conv8 = jnp.pad(state_conv, ((0, 0), (SUBLANES - (CONV_K - 1), 0), (0, 0)))
    mix_s, conv_s, shift_s, gdn_s, rwkv_s = _mixer(
        proj_s, ts, conv8, state_shift[:, None, :], state_gdn, state_rwkv, mixer_params, F32, SEQS_PER_STEP_SHORT)

    ln1 = _pad_rows(jnp.stack([ln1_g, ln1_b]), SUBLANES)
    ln2 = _pad_rows(jnp.stack([ln2_g, ln2_b]), SUBLANES)
    wrt = _pad_rows(w_router.T, LANES)
    brc = _pad_rows(b_router[:, None], LANES)
    h2d, idx, rank, gates, cnt = _post_mixer(mix_p.reshape(n_p, d), mix_s.reshape(n_s, d), xp2d, xs2d,
                                             w_o.astype(BF16), ln1, wrt, brc)

    counts = cnt[:N_EXPERTS, 0]
    padded = (((counts + MOE_TILE - 1) // MOE_TILE) * MOE_TILE).astype(jnp.int32)
    pend = jnp.cumsum(padded).astype(jnp.int32)
    pstart = pend - padded
    n_blocks = -(-(n_tok * TOP_K) // MOE_TILE) + N_EXPERTS
    n_rows = n_blocks * MOE_TILE
    block_expert = jnp.minimum(
        jnp.sum(pend[None, :] <= (jnp.arange(n_blocks) * MOE_TILE)[:, None], axis=1), N_EXPERTS - 1).astype(jnp.int32)
    n_used = pend[-1:] // MOE_TILE

    dest = _slots(idx, rank, pstart)
    xb = _dispatch(h2d, dest, pend, padded, n_rows)
    yb = _experts(xb, block_expert, n_used, w_gate, b_gate, w_up, b_up, w_down, b_down)
    y_p, y_s = _combine(yb, h2d, dest, gates, ln2, n_p)

    return (y_p.reshape(bp, tp, d), y_s.reshape(bs, ts, d), conv_p, shift_p[:, 0], gdn_p, rwkv_p,
            conv_s, shift_s[:, 0], gdn_s, rwkv_s)


def kernel(x_prompt, x_sample, state_conv, state_shift, state_gdn, state_rwkv, w_in, conv_w, a_log, dt_bias,
           gdn_norm_w, mu_shift, w0, w2, a0, a2, g2, k_k, k_a, r_k, lnx_w, lnx_b, w_o, ln1_g, ln1_b,
           w_router, b_router, w_gate, b_gate, w_up, b_up, w_down, b_down, ln2_g, ln2_b):
    mp = _mixer_params(conv_w, a_log, dt_bias, gdn_norm_w, mu_shift, w0, w2, a0, a2, g2, k_k, k_a, r_k,
                       lnx_w, lnx_b)
    return _layer(x_prompt, x_sample, state_conv, state_shift, state_gdn, state_rwkv,
                  w_in, mp, w_o, ln1_g, ln1_b, w_router, b_router,
                  w_gate, b_gate, w_up, b_up, w_down, b_down, ln2_g, ln2_b)
```

```python
import functools
import math

import jax
import jax.numpy as jnp
from jax import lax
from jax.experimental import pallas as pl
from jax.experimental.pallas import tpu as pltpu

F32 = jnp.float32
BF16 = jnp.bfloat16

D_MODEL = 1024
GDN_HEADS = 4
GDN_HEAD_DIM = 128
GDN_WIDTH = GDN_HEADS * GDN_HEAD_DIM
GDN_QKV = 3 * GDN_WIDTH
CONV_K = 4
RWKV_HEADS = 8
RWKV_HEAD_DIM = 64
RWKV_WIDTH = RWKV_HEADS * RWKV_HEAD_DIM
DECAY_LORA = 64
AAA_LORA = 64
GATE_LORA = 128
RWKV_COLS = 3 * RWKV_WIDTH + DECAY_LORA + AAA_LORA + GATE_LORA
RWKV_GN_EPS = RWKV_HEAD_DIM * 1e-5
N_EXPERTS = 32
TOP_K = 4
SWIGLU_LIMIT = 7.0
SWIGLU_ALPHA = 1.702
DEPTH = 1
ALPHA = (2.0 * DEPTH) ** 0.25
LN_EPS = 1e-5

LANES = 128
SUBLANES = 8
VMEM_LIMIT_BYTES = 56 * 1024 * 1024

COL_Z = GDN_QKV
COL_RWKV = COL_Z + GDN_WIDTH
COL_BA = COL_RWKV + RWKV_COLS
PROJ_COLS = COL_BA + LANES

CHUNK = 64
GROUP = 4
RWKV_GROUPS = RWKV_HEADS // GROUP
GROUP_W = GROUP * RWKV_HEAD_DIM
SEQS_PER_STEP_LONG = 4
SEQS_PER_STEP_SHORT = 8
PROJ_TILE = 512
TOK_TILE = 256
MOE_TILE = 512
DISPATCH_TILE = 512
COMBINE_TILE = 512
DMA_UNROLL = 8


class _Seqs:
    def __init__(self, vals):
        self.v = list(vals)

    def __getitem__(self, idx):
        return _Seqs([a[idx] for a in self.v])

    def __add__(self, o):
        return _lift(lambda a, b: a + b)(self, o)

    def __radd__(self, o):
        return _lift(lambda a, b: b + a)(self, o)

    def __sub__(self, o):
        return _lift(lambda a, b: a - b)(self, o)

    def __rsub__(self, o):
        return _lift(lambda a, b: b - a)(self, o)

    def __mul__(self, o):
        return _lift(lambda a, b: a * b)(self, o)

    def __rmul__(self, o):
        return _lift(lambda a, b: b * a)(self, o)

    def __neg__(self):
        return _Seqs([-a for a in self.v])

    @property
    def shape(self):
        return self.v[0].shape

    @property
    def T(self):
        return _Seqs([a.T for a in self.v])

    def astype(self, dt):
        return _Seqs([a.astype(dt) for a in self.v])


def _lift(f):
    def g(*args, **kw):
        n = next((len(a.v) for a in args if isinstance(a, _Seqs)), None)
        if n is None:
            return f(*args, **kw)
        return _Seqs([f(*[a.v[i] if isinstance(a, _Seqs) else a for a in args], **kw) for i in range(n)])
    return g


def _cat(parts, axis):
    n = next((len(a.v) for a in parts if isinstance(a, _Seqs)), None)
    if n is None:
        return jnp.concatenate(parts, axis=axis)
    return _Seqs([jnp.concatenate([a.v[i] if isinstance(a, _Seqs) else a for a in parts], axis=axis)
                  for i in range(n)])


_exp = _lift(jnp.exp)
_tanh = _lift(jnp.tanh)
_where = _lift(jnp.where)
_rsqrt = _lift(lax.rsqrt)
_sum = _lift(jnp.sum)
_roll = _lift(pltpu.roll)
_bcast = _lift(jnp.broadcast_to)


@_lift
def _mm(a, b):
    return jnp.dot(a.astype(BF16), b.astype(BF16), preferred_element_type=F32)


@_lift
def _mm_nt(a, b):
    return lax.dot_general(a.astype(BF16), b.astype(BF16), (((1,), (1,)), ((), ())), preferred_element_type=F32)


def _mm_tn(a, b):
    return _mm(a.T, b)


@_lift
def _mm_f32(a, b):
    return jnp.dot(a, b, precision=lax.Precision.HIGHEST, preferred_element_type=F32)


def _sigmoid(x):
    return 0.5 * _tanh(0.5 * x) + 0.5


@_lift
def _softplus(x):
    return jnp.maximum(x, 0.0) + jnp.log1p(jnp.exp(-jnp.abs(x)))


def _proj_kernel(x_ref, w_ref, o_ref):
    o_ref[...] = jnp.dot(x_ref[...].astype(BF16), w_ref[...], preferred_element_type=F32)


def _input_projection(x2d, w_in_bf16):
    n_tok = x2d.shape[0]
    tile = math.gcd(n_tok, PROJ_TILE)
    assert tile % SUBLANES == 0
    return pl.pallas_call(
        _proj_kernel,
        grid=(n_tok // tile,),
        in_specs=[pl.BlockSpec((tile, D_MODEL), lambda i: (i, 0)),
                  pl.BlockSpec((D_MODEL, PROJ_COLS), lambda i: (0, 0))],
        out_specs=pl.BlockSpec((tile, PROJ_COLS), lambda i: (i, 0)),
        out_shape=jax.ShapeDtypeStruct((n_tok, PROJ_COLS), F32),
        compiler_params=pltpu.CompilerParams(dimension_semantics=("parallel",),
                                             vmem_limit_bytes=VMEM_LIMIT_BYTES),
        name="input_projection",
    )(x2d, w_in_bf16)


def _shift_rows(cur, prev, s, row_ids):
    return _where(row_ids < s, _roll(prev, s, axis=0), _roll(cur, s, axis=0))


@_lift
def _block_stack(x, n, width):
    grp = lax.broadcasted_iota(jnp.int32, x.shape, 1) // width
    if x.shape[0] % (2 * SUBLANES) == 0:
        x16 = x.astype(BF16)
        return jnp.concatenate([x16 * jnp.where(grp == i, 1.0, 0.0).astype(BF16) for i in range(n)], axis=0)
    return jnp.concatenate([jnp.where(grp == i, x, 0.0) for i in range(n)], axis=0)


def _inv_series(nils, c, n):
    row = lax.broadcasted_iota(jnp.int32, nils[0].shape, 0)
    col = lax.broadcasted_iota(jnp.int32, nils[0].shape, 1) % c
    eye = jnp.where(row == col, 1.0, 0.0)
    xs = [eye + nil for nil in nils]
    qs = [_mm(nil, _block_stack(nil, n, c)) for nil in nils]
    levels = int(math.log2(c))
    for lvl in range(1, levels):
        nxt_q, nxt_x = [], []
        for q, x in zip(qs, xs):
            bd = _block_stack(q, n, c)
            if lvl == levels - 1:
                nxt_x.append(x + _mm(x, bd))
            else:
                both = _mm(_cat([q, x], 0), bd)
                nxt_q.append(both[0:c])
                nxt_x.append(x + both[c:2 * c])
        qs, xs = nxt_q, nxt_x
    return xs


def _head_sum(x, width):
    pieces = []
    for j in range(x.shape[1] // LANES):
        xb = x[:, j * LANES:(j + 1) * LANES]
        if width == LANES:
            pieces.append(_bcast(_sum(xb, axis=-1, keepdims=True), xb.shape))
        else:
            lo = lax.broadcasted_iota(jnp.int32, xb.shape, 1) < width
            s0 = _sum(_where(lo, xb, 0.0), axis=-1, keepdims=True)
            s1 = _sum(_where(lo, 0.0, xb), axis=-1, keepdims=True)
            pieces.append(_where(lo, s0, s1))
    return _cat(pieces, 1)


def _gdn_head(a, h):
    return a[:, h * GDN_HEAD_DIM:(h + 1) * GDN_HEAD_DIM]


def _gdn_prepare(x, prev, c, t_last, masks, convw_ref, gvec_ref):
    row_p, col_p, grp_p, lower_p, strict_p, tri = masks
    masked = t_last < c

    def vmask(a):
        if not masked:
            return a
        return _where(lax.broadcasted_iota(jnp.int32, a.shape, 0) < t_last, a, 0.0)

    xq = x[:, 0:GDN_QKV]
    pq = prev[:, 0:GDN_QKV]
    rid = lax.broadcasted_iota(jnp.int32, (c, GDN_QKV), 0)
    conv = _shift_rows(xq, pq, 3, rid) * convw_ref[0:1, :]
    conv = conv + _shift_rows(xq, pq, 2, rid) * convw_ref[1:2, :]
    conv = conv + _shift_rows(xq, pq, 1, rid) * convw_ref[2:3, :]
    conv = conv + xq * convw_ref[3:4, :]
    conv = conv * _sigmoid(conv)

    ba = x[:, COL_BA:COL_BA + LANES]
    beta_blk = vmask(_sigmoid(ba))
    g_blk = vmask(-jnp.exp(gvec_ref[0:1, :]) * _softplus(ba + gvec_ref[1:2, :]))

    qn, kn, kb, vb = [], [], [], []
    for h in range(GDN_HEADS):
        qh = _gdn_head(conv[:, 0:GDN_WIDTH], h)
        kh = _gdn_head(conv[:, GDN_WIDTH:2 * GDN_WIDTH], h)
        vh = vmask(_gdn_head(conv[:, 2 * GDN_WIDTH:GDN_QKV], h))
        qh = qh * (_rsqrt(_sum(qh * qh, axis=-1, keepdims=True) + 1e-6) * (GDN_HEAD_DIM ** -0.5))
        kh = vmask(kh * _rsqrt(_sum(kh * kh, axis=-1, keepdims=True) + 1e-6))
        bh = beta_blk[:, h:h + 1]
        qn.append(qh), kn.append(kh), kb.append(kh * bh), vb.append(vh * bh)

    ma = _mm_nt(_cat([_cat(kb, 1), _cat(qn, 1)], 0),
                _block_stack(_cat(kn, 1), GDN_HEADS, GDN_HEAD_DIM))
    gexp = _where(grp_p == 0, g_blk[:, GDN_HEADS:GDN_HEADS + 1], 0.0)
    for h in range(1, GDN_HEADS):
        gexp = _where(grp_p == h, g_blk[:, GDN_HEADS + h:GDN_HEADS + h + 1], gexp)
    gcol = _mm_f32(tri, gexp)
    grow = _sum(_where(row_p == col_p, gcol, 0.0), axis=0, keepdims=True)
    decay = _exp(_where(lower_p, gcol - grow, -jnp.inf))
    m_p = _where(strict_p, ma[0:c] * decay, 0.0)
    attn = ma[c:2 * c] * decay
    egc, kdec, glast = [], [], []
    for h in range(GDN_HEADS):
        gch = gcol[:, h * c:h * c + 1]
        glh = gcol[c - 1:c, h * c:h * c + 1]
        egc.append(_exp(gch))
        kdec.append(kn[h] * _exp(glh - gch))
        glast.append(_exp(glh))
    rhs2 = _cat([_block_stack(_cat(vb, 1), GDN_HEADS, GDN_HEAD_DIM),
                 _block_stack(_cat([kb[h] * egc[h] for h in range(GDN_HEADS)], 1), GDN_HEADS, GDN_HEAD_DIM)], 1)
    qdec = [qn[h] * egc[h] for h in range(GDN_HEADS)]
    return dict(nil=-m_p, attn=attn, rhs2=rhs2, qdec=qdec, kdec=kdec, glast=glast)


def _gdn_finish(p, tinv, x, c, states, gvec_ref):
    uw = _mm(tinv, p["rhs2"])
    vnew, o1 = [], []
    for h in range(GDN_HEADS):
        ws = _mm(_cat([_gdn_head(uw[:, GDN_WIDTH:2 * GDN_WIDTH], h), p["qdec"][h]], 0), states[h])
        vnew.append(_gdn_head(uw[:, 0:GDN_WIDTH], h) - ws[0:c])
        o1.append(ws[c:2 * c])
    o2 = _mm(p["attn"], _block_stack(_cat(vnew, 1), GDN_HEADS, GDN_HEAD_DIM))
    out, new_states = [], []
    for h in range(GDN_HEADS):
        new_states.append(states[h] * p["glast"][h] + _mm_tn(p["kdec"][h], vnew[h]))
        oh = o1[h] + _gdn_head(o2, h)
        zh = _gdn_head(x[:, COL_Z:COL_Z + GDN_WIDTH], h)
        oh = oh * _rsqrt(_sum(oh * oh, axis=-1, keepdims=True) * (1.0 / GDN_HEAD_DIM) + 1e-6) * gvec_ref[2:3, :]
        out.append(oh * (zh * _sigmoid(zh)))
    return out, new_states


def _rwkv_prepare(x, prev, c, t_last, masks, mu_ref, rvec_ref, loraw_ref, g2_ref):
    row_p, col_p, grp_p, lower_p, strict_p, tri = masks
    masked = t_last < c
    gc4 = GROUP * c

    def vmask(a):
        if not masked:
            return a
        return _where(lax.broadcasted_iota(jnp.int32, a.shape, 0) < t_last, a, 0.0)

    rw = x[:, COL_RWKV:COL_BA]
    rid = lax.broadcasted_iota(jnp.int32, (c, RWKV_COLS), 0)
    prev_row = _shift_rows(rw, prev[:, COL_RWKV:COL_BA], 1, rid)
    rs = rw + (prev_row - rw) * mu_ref[...]
    o1_, o2_, o3_ = RWKV_WIDTH, 2 * RWKV_WIDTH, 3 * RWKV_WIDTH
    r = rs[:, 0:o1_]
    kr = rs[:, o1_:o2_]
    vr = rs[:, o2_:o3_]
    la = rs[:, o3_:o3_ + LANES]
    gl = rs[:, o3_ + LANES:o3_ + 2 * LANES]
    lane1 = lax.broadcasted_iota(jnp.int32, (c, LANES), 1)
    wa = _mm(_where(lane1 < DECAY_LORA, _tanh(la), la), loraw_ref[...])
    logw = vmask(-math.exp(-0.5) * _sigmoid(rvec_ref[0:1, :] + wa[:, 0:RWKV_WIDTH]))
    a = _sigmoid(rvec_ref[1:2, :] + wa[:, RWKV_WIDTH:2 * RWKV_WIDTH])
    gate = _mm(_sigmoid(gl), g2_ref[...])
    kkr = kr * rvec_ref[2:3, :]
    kk = vmask(kkr * _rsqrt(_head_sum(kkr * kkr, RWKV_HEAD_DIM) + 1e-6))
    kr2 = vmask(kr * (1.0 + (a - 1.0) * rvec_ref[3:4, :]))
    gcum = _mm_f32(tri, logw)
    e_pos = _exp(gcum)
    e_neg = _exp(-gcum)
    a_t = -kk * _exp(gcum - logw)
    b_t = kk * a * e_neg
    k_t = kr2 * e_neg
    r_t = r * e_pos
    groups = []
    for g in range(RWKV_GROUPS):
        sl = slice(g * GROUP_W, (g + 1) * GROUP_W)
        at_g, bt_g, kt_g, rt_g, v_g = a_t[:, sl], b_t[:, sl], k_t[:, sl], r_t[:, sl], vr[:, sl]
        aa = _mm_nt(_cat([at_g, rt_g], 0),
                    _cat([_block_stack(bt_g, GROUP, RWKV_HEAD_DIM), _block_stack(kt_g, GROUP, RWKV_HEAD_DIM)], 0))
        a_ab = _where(strict_p, aa[0:c, 0:gc4], 0.0)
        a_ak = _where(strict_p, aa[0:c, gc4:2 * gc4], 0.0)
        a_rb = _where(lower_p, aa[c:2 * c, 0:gc4], 0.0)
        a_rk = _where(lower_p, aa[c:2 * c, gc4:2 * gc4], 0.0)
        akv = _mm(a_ak, _block_stack(v_g, GROUP, RWKV_HEAD_DIM))
        groups.append(dict(nil=a_ab, at=at_g, bt=bt_g, kt=kt_g, rt=rt_g, v=v_g, akv=akv,
                           a_r=_cat([a_rb, a_rk], 1), e_last=e_pos[c - 1:c, sl]))
    bonus = _head_sum(r * kr2 * rvec_ref[4:5, :], RWKV_HEAD_DIM) * vr
    return dict(groups=groups, gate=gate, bonus=bonus)


def _rwkv_finish(p, tinvs, c, states, rvec_ref):
    bd_mask = (lax.broadcasted_iota(jnp.int32, (GROUP_W, GROUP_W), 0) // RWKV_HEAD_DIM
               == lax.broadcasted_iota(jnp.int32, (GROUP_W, GROUP_W), 1) // RWKV_HEAD_DIM)
    qs = p["groups"]
    rng = range(RWKV_GROUPS)
    wu0 = [_mm(tinvs[g], _cat([_block_stack(qs[g]["at"], GROUP, RWKV_HEAD_DIM),
                               _block_stack(qs[g]["akv"], GROUP, RWKV_HEAD_DIM)], 1)) for g in rng]
    wr = [_mm_nt(_cat([wu0[g][:, 0:GROUP_W], qs[g]["rt"]], 0), states[g]) for g in rng]
    u = [wr[g][0:c] + wu0[g][:, GROUP_W:2 * GROUP_W] for g in rng]
    o_g = [wr[g][c:2 * c] + _mm(qs[g]["a_r"], _cat([_block_stack(u[g], GROUP, RWKV_HEAD_DIM),
                                                    _block_stack(qs[g]["v"], GROUP, RWKV_HEAD_DIM)], 0)) for g in rng]
    upd = [_mm_tn(_cat([u[g], qs[g]["v"]], 0), _cat([qs[g]["bt"], qs[g]["kt"]], 0)) for g in rng]
    new_states = [(states[g] + _where(bd_mask, upd[g], 0.0)) * qs[g]["e_last"] for g in rng]
    o_r = _cat(o_g, 1)
    mean_o = _head_sum(o_r, RWKV_HEAD_DIM) * (1.0 / RWKV_HEAD_DIM)
    dev = o_r - mean_o
    var_o = _head_sum(dev * dev, RWKV_HEAD_DIM) * (1.0 / RWKV_HEAD_DIM)
    o_n = dev * _rsqrt(var_o + RWKV_GN_EPS) * rvec_ref[5:6, :] + rvec_ref[6:7, :]
    return (o_n + p["bonus"]) * p["gate"], new_states


def _mixer_kernel(c, rows_in, t_out, t_last, n_chunks, nb,
                  proj_ref, cinit_ref, sinit_ref, sg0_ref, sr0_ref,
                  convw_ref, gvec_ref, mu_ref, rvec_ref, loraw_ref, g2_ref,
                  mix_ref, convo_ref, shifto_ref, sgo_ref, sro_ref,
                  prev_scr, sg_scr, sr_scr):
    ci = pl.program_id(1)
    gc4 = GROUP * c
    hd = RWKV_HEAD_DIM

    @pl.when(ci == 0)
    def _():
        prev_scr[...] = jnp.zeros(prev_scr.shape, F32)
        for j in range(nb):
            prev_scr[j, c - SUBLANES:c, 0:GDN_QKV] = cinit_ref[j]
            prev_scr[j, c - 1:c, COL_RWKV:COL_BA] = sinit_ref[j]
            for g in range(RWKV_GROUPS):
                blocks = []
                for h in range(GROUP):
                    parts = [sr0_ref[j, g * GROUP + h]]
                    if h > 0:
                        parts.insert(0, jnp.zeros((hd, h * hd), F32))
                    if h < GROUP - 1:
                        parts.append(jnp.zeros((hd, (GROUP - 1 - h) * hd), F32))
                    blocks.append(jnp.concatenate(parts, axis=1))
                sr_scr[j, g] = jnp.concatenate(blocks, axis=0)
        sg_scr[...] = sg0_ref[...]

    row_p = lax.broadcasted_iota(jnp.int32, (c, gc4), 0)
    col_p = lax.broadcasted_iota(jnp.int32, (c, gc4), 1) % c
    grp_p = lax.broadcasted_iota(jnp.int32, (c, gc4), 1) // c
    tri = jnp.where(lax.broadcasted_iota(jnp.int32, (c, c), 1) <= lax.broadcasted_iota(jnp.int32, (c, c), 0),
                    1.0, 0.0)
    masks = (row_p, col_p, grp_p, col_p <= row_p, col_p < row_p, tri)

    def load_x(j):
        x = proj_ref[j]
        if rows_in < c:
            x = jnp.concatenate([x, jnp.zeros((c - rows_in, PROJ_COLS), F32)], axis=0)
        return x

    s_gdn = [_Seqs([sg_scr[j, h] for j in range(nb)]) for h in range(GDN_HEADS)]
    s_rwkv = [_Seqs([sr_scr[j, g] for j in range(nb)]) for g in range(RWKV_GROUPS)]
    x = _Seqs([load_x(j) for j in range(nb)])
    prev = _Seqs([prev_scr[j] for j in range(nb)])

    gdn = _gdn_prepare(x, prev, c, t_last, masks, convw_ref, gvec_ref)
    rwkv = _rwkv_prepare(x, prev, c, t_last, masks, mu_ref, rvec_ref, loraw_ref, g2_ref)
    tinvs = _inv_series([gdn["nil"]] + [q["nil"] for q in rwkv["groups"]], c, GROUP)
    o_gdn, new_g = _gdn_finish(gdn, tinvs[0], x, c, s_gdn, gvec_ref)
    o_rwkv, new_r = _rwkv_finish(rwkv, tinvs[1:], c, s_rwkv, rvec_ref)
    mix = _cat(o_gdn + [o_rwkv], 1)

    for j in range(nb):
        mix_ref[j] = mix.v[j][0:t_out].astype(mix_ref.dtype)
        prev_scr[j] = x.v[j][:, 0:COL_BA]
        for h in range(GDN_HEADS):
            sg_scr[j, h] = new_g[h].v[j]
        for g in range(RWKV_GROUPS):
            sr_scr[j, g] = new_r[g].v[j]

    @pl.when(ci == n_chunks - 1)
    def _():
        for j in range(nb):
            convo_ref[j] = x.v[j][t_last - (CONV_K - 1):t_last, 0:GDN_QKV]
            shifto_ref[j] = x.v[j][t_last - 1:t_last, COL_RWKV:COL_BA]
            for g in range(RWKV_GROUPS):
                s_g = new_r[g].v[j]
                for h in range(GROUP):
                    sro_ref[j, g * GROUP + h] = s_g[h * hd:(h + 1) * hd, h * hd:(h + 1) * hd]
            for h in range(GDN_HEADS):
                sgo_ref[j, h] = new_g[h].v[j]


def _mixer(proj3d, seq_len, conv_init8, shift_init, s_gdn, s_rwkv, mp, mix_dtype, nb):
    bsz, rows, _ = proj3d.shape
    nb = math.gcd(bsz, nb)
    if rows >= CHUNK:
        assert rows == seq_len and seq_len % CHUNK == 0
        c = CHUNK
        n_chunks, rows_in, t_out, t_last = seq_len // c, c, c, c
    else:
        c = rows
        n_chunks, rows_in, t_out, t_last = 1, rows, seq_len, seq_len
    assert t_last >= CONV_K - 1 and c % SUBLANES == 0
    const2 = lambda b, i: (0, 0)
    perb3 = lambda b, i: (b, 0, 0)
    perb4 = lambda b, i: (b, 0, 0, 0)
    kern = functools.partial(_mixer_kernel, c, rows_in, t_out, t_last, n_chunks, nb)
    state_g = (nb, GDN_HEADS, GDN_HEAD_DIM, GDN_HEAD_DIM)
    state_r = (nb, RWKV_HEADS, RWKV_HEAD_DIM, RWKV_HEAD_DIM)
    return pl.pallas_call(
        kern,
        grid=(bsz // nb, n_chunks),
        in_specs=[pl.BlockSpec((nb, rows_in, PROJ_COLS), lambda b, i: (b, i, 0)),
                  pl.BlockSpec((nb, SUBLANES, GDN_QKV), perb3),
                  pl.BlockSpec((nb, 1, RWKV_COLS), perb3),
                  pl.BlockSpec(state_g, perb4),
                  pl.BlockSpec(state_r, perb4),
                  pl.BlockSpec((CONV_K, GDN_QKV), const2),
                  pl.BlockSpec((SUBLANES, LANES), const2),
                  pl.BlockSpec((1, RWKV_COLS), const2),
                  pl.BlockSpec((SUBLANES, RWKV_WIDTH), const2),
                  pl.BlockSpec((LANES, 2 * RWKV_WIDTH), const2),
                  pl.BlockSpec((GATE_LORA, RWKV_WIDTH), const2)],
        out_specs=[pl.BlockSpec((nb, t_out, D_MODEL), lambda b, i: (b, i, 0)),
                   pl.BlockSpec((nb, CONV_K - 1, GDN_QKV), perb3),
                   pl.BlockSpec((nb, 1, RWKV_COLS), perb3),
                   pl.BlockSpec(state_g, perb4),
                   pl.BlockSpec(state_r, perb4)],
        out_shape=[jax.ShapeDtypeStruct((bsz, seq_len, D_MODEL), mix_dtype),
                   jax.ShapeDtypeStruct((bsz, CONV_K - 1, GDN_QKV), F32),
                   jax.ShapeDtypeStruct((bsz, 1, RWKV_COLS), F32),
                   jax.ShapeDtypeStruct((bsz,) + state_g[1:], F32),
                   jax.ShapeDtypeStruct((bsz,) + state_r[1:], F32)],
        scratch_shapes=[pltpu.VMEM((nb, c, COL_BA), F32),
                        pltpu.VMEM(state_g, F32),
                        pltpu.VMEM((nb, RWKV_GROUPS, GROUP_W, GROUP_W), F32)],
        compiler_params=pltpu.CompilerParams(dimension_semantics=("parallel", "arbitrary"),
                                             vmem_limit_bytes=VMEM_LIMIT_BYTES),
        name="sequence_mixers",
    )(proj3d, conv_init8, shift_init, s_gdn, s_rwkv,
      mp["conv_w"], mp["gvec"], mp["mu"], mp["rvec"], mp["lora_w"], mp["g2"])


def _layernorm(x, g, b):
    mu = jnp.mean(x, axis=-1, keepdims=True)
    d = x - mu
    var = jnp.mean(d * d, axis=-1, keepdims=True)
    return d * lax.rsqrt(var + LN_EPS) * g + b


def _post_mixer_kernel(n_first, mixp_ref, mixs_ref, xp_ref, xs_ref, wo_ref, lnv_ref, wrt_ref, br_ref,
                       h_ref, idx_ref, rank_ref, gate_ref, cnt_ref, base_scr, hp_scr):
    i = pl.program_id(0)
    tt = h_ref.shape[0]

    @pl.when(i == 0)
    def _():
        base_scr[...] = jnp.zeros(base_scr.shape, F32)

    @pl.when(i < n_first)
    def _():
        hp_scr[...] = ALPHA * xp_ref[...] + jnp.dot(mixp_ref[...].astype(BF16), wo_ref[...],
                                                    preferred_element_type=F32)

    @pl.when(i >= n_first)
    def _():
        hp_scr[...] = ALPHA * xs_ref[...] + jnp.dot(mixs_ref[...].astype(BF16), wo_ref[...],
                                                    preferred_element_type=F32)

    h = _layernorm(hp_scr[...], lnv_ref[0:1, :], lnv_ref[1:2, :])
    h_ref[...] = h
    lt = lax.dot_general(wrt_ref[...], h, (((1,), (1,)), ((), ())),
                         precision=lax.Precision.HIGHEST, preferred_element_type=F32) + br_ref[...]
    eid = lax.broadcasted_iota(jnp.int32, lt.shape, 0)
    lt = jnp.where(eid < N_EXPERTS, lt, -jnp.inf)
    upper = jnp.where(lax.broadcasted_iota(jnp.int32, (tt, tt), 0) < lax.broadcasted_iota(jnp.int32, (tt, tt), 1),
                      1.0, 0.0).astype(BF16)
    base = base_scr[...]
    vals, idxs, ranks = [], [], []
    for _ in range(TOP_K):
        m = jnp.max(lt, axis=0, keepdims=True)
        sel = jnp.min(jnp.where(lt == m, eid, LANES), axis=0, keepdims=True)
        onehot = eid == sel
        lt = jnp.where(onehot, -jnp.inf, lt)
        oh = jnp.where(onehot, 1.0, 0.0)
        before = jnp.dot(oh.astype(BF16), upper, preferred_element_type=F32)
        ranks.append(jnp.sum(oh * (base + before), axis=0, keepdims=True))
        base = base + jnp.sum(oh, axis=1, keepdims=True)
        vals.append(m)
        idxs.append(sel)
    base_scr[...] = base
    ex = [jnp.exp(v - vals[0]) for v in vals]
    den = ex[0] + ex[1] + ex[2] + ex[3]
    pad_i = jnp.zeros((SUBLANES - TOP_K, tt), jnp.int32)
    idx_ref[...] = jnp.concatenate(idxs + [pad_i], axis=0)
    rank_ref[...] = jnp.concatenate([rk.astype(jnp.int32) for rk in ranks] + [pad_i], axis=0)
    gates = jnp.concatenate([e / den for e in ex] + [jnp.zeros((SUBLANES - TOP_K, tt), F32)], axis=0)
    gate_ref[...] = gates.T

    @pl.when(i == pl.num_programs(0) - 1)
    def _():
        cnt_ref[...] = base[:, 0:LANES].astype(jnp.int32)


def _post_mixer(mix_p, mix_s, x_p, x_s, w_o_bf16, ln1, w_router_t, b_router_col):
    tt = TOK_TILE
    n_p, n_s = x_p.shape[0], x_s.shape[0]
    assert n_p % tt == 0 and n_s % tt == 0
    n1, n2 = n_p // tt, n_s // tt
    n_tok = n_p + n_s
    const2 = lambda i: (0, 0)
    first = lambda i: (jnp.minimum(i, n1 - 1), 0)
    second = lambda i: (jnp.maximum(i - n1, 0), 0)
    return pl.pallas_call(
        functools.partial(_post_mixer_kernel, n1),
        grid=(n1 + n2,),
        in_specs=[pl.BlockSpec((tt, D_MODEL), first),
                  pl.BlockSpec((tt, D_MODEL), second),
                  pl.BlockSpec((tt, D_MODEL), first),
                  pl.BlockSpec((tt, D_MODEL), second),
                  pl.BlockSpec((D_MODEL, D_MODEL), const2),
                  pl.BlockSpec((SUBLANES, D_MODEL), const2),
                  pl.BlockSpec((LANES, D_MODEL), const2),
                  pl.BlockSpec((LANES, 1), const2)],
        out_specs=[pl.BlockSpec((tt, D_MODEL), lambda i: (i, 0)),
                   pl.BlockSpec((SUBLANES, tt), lambda i: (0, i)),
                   pl.BlockSpec((SUBLANES, tt), lambda i: (0, i)),
                   pl.BlockSpec((tt, SUBLANES), lambda i: (i, 0)),
                   pl.BlockSpec((LANES, LANES), const2)],
        out_shape=[jax.ShapeDtypeStruct((n_tok, D_MODEL), F32),
                   jax.ShapeDtypeStruct((SUBLANES, n_tok), jnp.int32),
                   jax.ShapeDtypeStruct((SUBLANES, n_tok), jnp.int32),
                   jax.ShapeDtypeStruct((n_tok, SUBLANES), F32),
                   jax.ShapeDtypeStruct((LANES, LANES), jnp.int32)],
        scratch_shapes=[pltpu.VMEM((LANES, tt), F32), pltpu.VMEM((tt, D_MODEL), F32)],
        compiler_params=pltpu.CompilerParams(dimension_semantics=("arbitrary",),
                                             vmem_limit_bytes=VMEM_LIMIT_BYTES),
        name="outproj_norm_router",
    )(mix_p, mix_s, x_p, x_s, w_o_bf16, ln1, w_router_t, b_router_col)


def _slot_kernel(pstart_ref, idx_ref, rank_ref, dest_ref):
    idx = idx_ref[...]
    dest = rank_ref[...]
    for e in range(N_EXPERTS):
        dest = dest + jnp.where(idx == e, pstart_ref[e], 0)
    dest_ref[...] = dest


def _slots(idx, rank, pstart):
    grid_spec = pltpu.PrefetchScalarGridSpec(
        num_scalar_prefetch=1,
        grid=(1,),
        in_specs=[pl.BlockSpec(idx.shape, lambda i, ps: (0, 0)),
                  pl.BlockSpec(idx.shape, lambda i, ps: (0, 0))],
        out_specs=pl.BlockSpec(idx.shape, lambda i, ps: (0, 0)),
    )
    return pl.pallas_call(
        _slot_kernel,
        grid_spec=grid_spec,
        out_shape=jax.ShapeDtypeStruct(idx.shape, jnp.int32),
        name="moe_slots",
    )(pstart, idx, rank)


def _dispatch_kernel(pend_ref, padded_ref, dest_ref, h_ref, xb_hbm, zero_scr, sem):
    i = pl.program_id(0)
    tt = h_ref.shape[0]
    tm = zero_scr.shape[0]

    @pl.when(i == 0)
    def _():
        zero_scr[...] = jnp.zeros(zero_scr.shape, F32)

        def tail_copy(e):
            start = pl.multiple_of(pend_ref[e] - tm, tm)
            return pltpu.make_async_copy(zero_scr, xb_hbm.at[pl.ds(start, tm), :], sem)

        for e in range(N_EXPERTS):
            @pl.when(padded_ref[e] > 0)
            def _():
                tail_copy(e).start()
        for e in range(N_EXPERTS):
            @pl.when(padded_ref[e] > 0)
            def _():
                tail_copy(e).wait()

        def spare_copy(b):
            return pltpu.make_async_copy(zero_scr, xb_hbm.at[pl.ds(pl.multiple_of(b * tm, tm), tm), :], sem)

        def spare_start(b, carry):
            spare_copy(b).start()
            return carry

        def spare_wait(b, carry):
            spare_copy(b).wait()
            return carry

        first_spare = pend_ref[N_EXPERTS - 1] // tm
        lax.fori_loop(first_spare, xb_hbm.shape[0] // tm, spare_start, 0)
        lax.fori_loop(first_spare, xb_hbm.shape[0] // tm, spare_wait, 0)

    def row_copy(t, k):
        return pltpu.make_async_copy(h_ref.at[pl.ds(t, 1), :], xb_hbm.at[pl.ds(dest_ref[k, t], 1), :], sem)

    def issue(t, carry):
        for k in range(TOP_K):
            row_copy(t, k).start(priority=k % 2)
        return carry

    lax.fori_loop(0, tt, issue, 0, unroll=DMA_UNROLL)

    def drain(t, carry):
        for k in range(TOP_K):
            row_copy(t, k).wait()
        return carry

    lax.fori_loop(0, tt, drain, 0, unroll=DMA_UNROLL)


def _dispatch(h2d, dest, pend, padded, n_rows):
    n_tok = h2d.shape[0]
    tt = math.gcd(n_tok, DISPATCH_TILE)
    assert tt % LANES == 0
    grid_spec = pltpu.PrefetchScalarGridSpec(
        num_scalar_prefetch=2,
        grid=(n_tok // tt,),
        in_specs=[pl.BlockSpec((SUBLANES, tt), lambda i, pe, pa: (0, i), memory_space=pltpu.SMEM),
                  pl.BlockSpec((tt, D_MODEL), lambda i, pe, pa: (i, 0))],
        out_specs=pl.BlockSpec(memory_space=pl.ANY),
        scratch_shapes=[pltpu.VMEM((MOE_TILE, D_MODEL), F32), pltpu.SemaphoreType.DMA(())],
    )
    return pl.pallas_call(
        _dispatch_kernel,
        grid_spec=grid_spec,
        out_shape=jax.ShapeDtypeStruct((n_rows, D_MODEL), F32),
        compiler_params=pltpu.CompilerParams(dimension_semantics=("arbitrary",)),
        name="moe_dispatch",
    )(pend, padded, dest, h2d)


def _expert_kernel(be_ref, nused_ref, x_ref, wg_ref, wu_ref, wd_ref, bg_ref, bu_ref, bd_ref, y_ref,
                   wg16, wu16, wd16):
    i = pl.program_id(0)
    prev_e = be_ref[jnp.maximum(i - 1, 0)]
    fresh = jnp.logical_or(i == 0, be_ref[i] != prev_e)
    used = i < nused_ref[0]

    @pl.when(jnp.logical_and(fresh, used))
    def _():
        wg16[...] = wg_ref[0].astype(BF16)
        wu16[...] = wu_ref[0].astype(BF16)
        wd16[...] = wd_ref[0].astype(BF16)

    @pl.when(used)
    def _():
        x16 = x_ref[...].astype(BF16)
        gt = jnp.dot(x16, wg16[...], preferred_element_type=F32) + bg_ref[0]
        up = jnp.dot(x16, wu16[...], preferred_element_type=F32) + bu_ref[0]
        gt = jnp.minimum(gt, SWIGLU_LIMIT)
        up = jnp.clip(up, -SWIGLU_LIMIT, SWIGLU_LIMIT)
        hid = (up + 1.0) * gt * _sigmoid(SWIGLU_ALPHA * gt)
        y_ref[...] = jnp.dot(hid.astype(BF16), wd16[...], preferred_element_type=F32) + bd_ref[0]

    @pl.when(jnp.logical_not(used))
    def _():
        y_ref[...] = jnp.zeros(y_ref.shape, F32)


def _experts(xb, block_expert, n_used, w_gate, b_gate, w_up, b_up, w_down, b_down):
    n_rows = xb.shape[0]
    tm = MOE_TILE
    n_blocks = n_rows // tm
    d_e = w_gate.shape[2]
    wspec = lambda shape: pl.BlockSpec(shape, lambda i, be, nu: (be[i], 0, 0))
    grid_spec = pltpu.PrefetchScalarGridSpec(
        num_scalar_prefetch=2,
        grid=(n_blocks,),
        in_specs=[pl.BlockSpec((tm, D_MODEL), lambda i, be, nu: (jnp.maximum(jnp.minimum(i, nu[0] - 1), 0), 0)),
                  wspec((1, D_MODEL, d_e)), wspec((1, D_MODEL, d_e)), wspec((1, d_e, D_MODEL)),
                  wspec((1, 1, d_e)), wspec((1, 1, d_e)), wspec((1, 1, D_MODEL))],
        out_specs=pl.BlockSpec((tm, D_MODEL), lambda i, be, nu: (i, 0)),
        scratch_shapes=[pltpu.VMEM((D_MODEL, d_e), BF16), pltpu.VMEM((D_MODEL, d_e), BF16),
                        pltpu.VMEM((d_e, D_MODEL), BF16)],
    )
    return pl.pallas_call(
        _expert_kernel,
        grid_spec=grid_spec,
        out_shape=jax.ShapeDtypeStruct((n_rows, D_MODEL), F32),
        compiler_params=pltpu.CompilerParams(dimension_semantics=("arbitrary",),
                                             vmem_limit_bytes=VMEM_LIMIT_BYTES),
        name="moe_experts",
    )(block_expert, n_used, xb, w_gate, w_up, w_down,
      b_gate[:, None, :], b_up[:, None, :], b_down[:, None, :])


def _combine_kernel(n_first, dest_ref, h_ref, gate_ref, lnv_ref, yb_hbm, yp_ref, ys_ref, buf, sem):
    i = pl.program_id(0)
    tt = h_ref.shape[0]

    def row_copy(t, k):
        return pltpu.make_async_copy(yb_hbm.at[pl.ds(dest_ref[k, t], 1), :],
                                     buf.at[pl.ds(t, 1), pl.ds(k * D_MODEL, D_MODEL)], sem)

    def issue(t, carry):
        for k in range(TOP_K):
            row_copy(t, k).start(priority=k % 2)
        return carry

    lax.fori_loop(0, tt, issue, 0, unroll=DMA_UNROLL)

    def drain(t, carry):
        for k in range(TOP_K):
            row_copy(t, k).wait()
        return carry

    lax.fori_loop(0, tt, drain, 0, unroll=DMA_UNROLL)
    gates = gate_ref[...]
    f = buf[:, 0:D_MODEL] * gates[:, 0:1]
    for k in range(1, TOP_K):
        f = f + buf[:, k * D_MODEL:(k + 1) * D_MODEL] * gates[:, k:k + 1]
    y = _layernorm(ALPHA * h_ref[...] + f, lnv_ref[0:1, :], lnv_ref[1:2, :])

    @pl.when(i < n_first)
    def _():
        yp_ref[...] = y

    @pl.when(i >= n_first)
    def _():
        ys_ref[...] = y


def _combine(yb, h2d, dest, gates, ln2, n_p):
    n_tok = h2d.shape[0]
    n_s = n_tok - n_p
    tt = math.gcd(math.gcd(n_p, n_s), COMBINE_TILE)
    assert tt % LANES == 0
    n1, n2 = n_p // tt, n_s // tt
    return pl.pallas_call(
        functools.partial(_combine_kernel, n1),
        grid=(n1 + n2,),
        in_specs=[pl.BlockSpec((SUBLANES, tt), lambda i: (0, i), memory_space=pltpu.SMEM),
                  pl.BlockSpec((tt, D_MODEL), lambda i: (i, 0)),
                  pl.BlockSpec((tt, SUBLANES), lambda i: (i, 0)),
                  pl.BlockSpec((SUBLANES, D_MODEL), lambda i: (0, 0)),
                  pl.BlockSpec(memory_space=pl.ANY)],
        out_specs=[pl.BlockSpec((tt, D_MODEL), lambda i: (jnp.minimum(i, n1 - 1), 0)),
                   pl.BlockSpec((tt, D_MODEL), lambda i: (jnp.maximum(i - n1, 0), 0))],
        out_shape=[jax.ShapeDtypeStruct((n_p, D_MODEL), F32),
                   jax.ShapeDtypeStruct((n_s, D_MODEL), F32)],
        scratch_shapes=[pltpu.VMEM((tt, TOP_K * D_MODEL), F32), pltpu.SemaphoreType.DMA(())],
        compiler_params=pltpu.CompilerParams(dimension_semantics=("arbitrary",),
                                             vmem_limit_bytes=VMEM_LIMIT_BYTES),
        name="moe_combine_norm",
    )(dest, h2d, gates, ln2, yb)


def _pad_rows(v, rows):
    return jnp.concatenate([v, jnp.zeros((rows - v.shape[0],) + v.shape[1:], v.dtype)], axis=0)


def _mixer_params(conv_w, a_log, dt_bias, gdn_norm_w, mu_shift, w0, w2, a0, a2, g2, k_k, k_a, r_k, lnx_w, lnx_b):
    gvec = jnp.zeros((SUBLANES, LANES), F32)
    gvec = gvec.at[0, GDN_HEADS:2 * GDN_HEADS].set(a_log).at[1, GDN_HEADS:2 * GDN_HEADS].set(dt_bias)
    gvec = gvec.at[2, :].set(gdn_norm_w)
    rvec = _pad_rows(jnp.stack([w0, a0, k_k, k_a, r_k.reshape(-1), lnx_w, lnx_b]), SUBLANES)
    lora_w = jnp.zeros((LANES, 2 * RWKV_WIDTH), F32)
    lora_w = lora_w.at[0:DECAY_LORA, 0:RWKV_WIDTH].set(w2).at[DECAY_LORA:, RWKV_WIDTH:].set(a2)
    return dict(conv_w=conv_w, gvec=gvec, mu=mu_shift[None, :], rvec=rvec,
                lora_w=lora_w.astype(BF16), g2=g2.astype(BF16))


def _layer(x_prompt, x_sample, state_conv, state_shift, state_gdn, state_rwkv,
           w_in, mixer_params, w_o, ln1_g, ln1_b, w_router, b_router,
           w_gate, b_gate, w_up, b_up, w_down, b_down, ln2_g, ln2_b):
    bp, tp, d = x_prompt.shape
    bs, ts, _ = x_sample.shape
    n_p, n_s = bp * tp, bs * ts
    n_tok = n_p + n_s
    xp2d = x_prompt.reshape(n_p, d)
    xs2d = x_sample.reshape(n_s, d)

    in_cols = w_in.shape[1]
    off_ba = COL_Z + GDN_WIDTH
    w_in_r = jnp.concatenate([w_in[:, :off_ba], w_in[:, off_ba + 2 * GDN_HEADS:],
                              w_in[:, off_ba:off_ba + 2 * GDN_HEADS],
                              jnp.zeros((d, PROJ_COLS - in_cols), w_in.dtype)], axis=1).astype(BF16)

    rows_s = -(-ts // SUBLANES) * SUBLANES
    proj_p = _input_projection(xp2d, w_in_r).reshape(bp, tp, PROJ_COLS)
    proj_s = _input_projection(xs2d, w_in_r).reshape(bs, ts, PROJ_COLS)
    if rows_s != ts:
        proj_s = jnp.pad(proj_s, ((0, 0), (0, rows_s - ts), (0, 0)))
    zeros_p = (jnp.zeros((bp, SUBLANES, GDN_QKV), F32), jnp.zeros((bp, 1, RWKV_COLS), F32),
               jnp.zeros((bp, GDN_HEADS, GDN_HEAD_DIM, GDN_HEAD_DIM), F32),
               jnp.zeros((bp, RWKV_HEADS, RWKV_HEAD_DIM, RWKV_HEAD_DIM), F32))
    mix_p, conv_p, shift_p, gdn_p, rwkv_p = _mixer(proj_p, tp, *zeros_p, mixer_params, BF16,
                                                   SEQS_PER_STEP_LONG)
    conv8 = jnp.pad(state_conv, ((0, 0), (SUBLANES - (CONV_K - 1), 0), (0, 0)))
    mix_s, conv_s, shift_s, gdn_s, rwkv_s = _mixer(
        proj_s, ts, conv8, state_shift[:, None, :], state_gdn, state_rwkv, mixer_params, F32, SEQS_PER_STEP_SHORT)

    ln1 = _pad_rows(jnp.stack([ln1_g, ln1_b]), SUBLANES)
    ln2 = _pad_rows(jnp.stack([ln2_g, ln2_b]), SUBLANES)
    wrt = _pad_rows(w_router.T, LANES)
    brc = _pad_rows(b_router[:, None], LANES)
    h2d, idx, rank, gates, cnt = _post_mixer(mix_p.reshape(n_p, d), mix_s.reshape(n_s, d), xp2d, xs2d,
                                             w_o.astype(BF16), ln1, wrt, brc)

    counts = cnt[:N_EXPERTS, 0]
    padded = (((counts + MOE_TILE - 1) // MOE_TILE) * MOE_TILE).astype(jnp.int32)
    pend = jnp.cumsum(padded).astype(jnp.int32)
    pstart = pend - padded
    n_blocks = -(-(n_tok * TOP_K) // MOE_TILE) + N_EXPERTS
    n_rows = n_blocks * MOE_TILE
    block_expert = jnp.minimum(
        jnp.sum(pend[None, :] <= (jnp.arange(n_blocks) * MOE_TILE)[:, None], axis=1), N_EXPERTS - 1).astype(jnp.int32)
    n_used = pend[-1:] // MOE_TILE

    dest = _slots(idx, rank, pstart)
    xb = _dispatch(h2d, dest, pend, padded, n_rows)
    yb = _experts(xb, block_expert, n_used, w_gate, b_gate, w_up, b_up, w_down, b_down)
    y_p, y_s = _combine(yb, h2d, dest, gates, ln2, n_p)

    return (y_p.reshape(bp, tp, d), y_s.reshape(bs, ts, d), conv_p, shift_p[:, 0], gdn_p, rwkv_p,
            conv_s, shift_s[:, 0], gdn_s, rwkv_s)


def kernel(x_prompt, x_sample, state_conv, state_shift, state_gdn, state_rwkv, w_in, conv_w, a_log, dt_bias,
           gdn_norm_w, mu_shift, w0, w2, a0, a2, g2, k_k, k_a, r_k, lnx_w, lnx_b, w_o, ln1_g, ln1_b,
           w_router, b_router, w_gate, b_gate, w_up, b_up, w_down, b_down, ln2_g, ln2_b):
    mp = _mixer_params(conv_w, a_log, dt_bias, gdn_norm_w, mu_shift, w0, w2, a0, a2, g2, k_k, k_a, r_k,
                       lnx_w, lnx_b)
    return _layer(x_prompt, x_sample, state_conv, state_shift, state_gdn, state_rwkv,
                  w_in, mp, w_o, ln1_g, ln1_b, w_router, b_router,
                  w_gate, b_gate, w_up, b_up, w_down, b_down, ln2_g, ln2_b)
```

```python
import functools
import math

import jax
import jax.numpy as jnp
from jax import lax
from jax.experimental import pallas as pl
from jax.experimental.pallas import tpu as pltpu

F32 = jnp.float32
BF16 = jnp.bfloat16

D_MODEL = 1024
GDN_HEADS = 4
GDN_HEAD_DIM = 128
GDN_WIDTH = GDN_HEADS * GDN_HEAD_DIM
GDN_QKV = 3 * GDN_WIDTH
CONV_K = 4
RWKV_HEADS = 8
RWKV_HEAD_DIM = 64
RWKV_WIDTH = RWKV_HEADS * RWKV_HEAD_DIM
DECAY_LORA = 64
AAA_LORA = 64
GATE_LORA = 128
RWKV_COLS = 3 * RWKV_WIDTH + DECAY_LORA + AAA_LORA + GATE_LORA
RWKV_GN_EPS = RWKV_HEAD_DIM * 1e-5
N_EXPERTS = 32
TOP_K = 4
SWIGLU_LIMIT = 7.0
SWIGLU_ALPHA = 1.702
DEPTH = 1
ALPHA = (2.0 * DEPTH) ** 0.25
LN_EPS = 1e-5

LANES = 128
SUBLANES = 8
VMEM_LIMIT_BYTES = 56 * 1024 * 1024

COL_Z = GDN_QKV
COL_RWKV = COL_Z + GDN_WIDTH
COL_BA = COL_RWKV + RWKV_COLS
PROJ_COLS = COL_BA + LANES

CHUNK = 64
GROUP = 4
RWKV_GROUPS = RWKV_HEADS // GROUP
GROUP_W = GROUP * RWKV_HEAD_DIM
SEQS_PER_STEP_LONG = 4
SEQS_PER_STEP_SHORT = 8
PROJ_TILE = 512
TOK_TILE = 256
MOE_TILE = 512
DISPATCH_TILE = 512
COMBINE_TILE = 512
DMA_UNROLL = 8


class _Seqs:
    def __init__(self, vals):
        self.v = list(vals)

    def __getitem__(self, idx):
        return _Seqs([a[idx] for a in self.v])

    def __add__(self, o):
        return _lift(lambda a, b: a + b)(self, o)

    def __radd__(self, o):
        return _lift(lambda a, b: b + a)(self, o)

    def __sub__(self, o):
        return _lift(lambda a, b: a - b)(self, o)

    def __rsub__(self, o):
        return _lift(lambda a, b: b - a)(self, o)

    def __mul__(self, o):
        return _lift(lambda a, b: a * b)(self, o)

    def __rmul__(self, o):
        return _lift(lambda a, b: b * a)(self, o)

    def __neg__(self):
        return _Seqs([-a for a in self.v])

    @property
    def shape(self):
        return self.v[0].shape

    @property
    def T(self):
        return _Seqs([a.T for a in self.v])

    def astype(self, dt):
        return _Seqs([a.astype(dt) for a in self.v])


def _lift(f):
    def g(*args, **kw):
        n = next((len(a.v) for a in args if isinstance(a, _Seqs)), None)
        if n is None:
            return f(*args, **kw)
        return _Seqs([f(*[a.v[i] if isinstance(a, _Seqs) else a for a in args], **kw) for i in range(n)])
    return g


def _cat(parts, axis):
    n = next((len(a.v) for a in parts if isinstance(a, _Seqs)), None)
    if n is None:
        return jnp.concatenate(parts, axis=axis)
    return _Seqs([jnp.concatenate([a.v[i] if isinstance(a, _Seqs) else a for a in parts], axis=axis)
                  for i in range(n)])


_exp = _lift(jnp.exp)
_tanh = _lift(jnp.tanh)
_where = _lift(jnp.where)
_rsqrt = _lift(lax.rsqrt)
_sum = _lift(jnp.sum)
_roll = _lift(pltpu.roll)
_bcast = _lift(jnp.broadcast_to)


@_lift
def _mm(a, b):
    return jnp.dot(a.astype(BF16), b.astype(BF16), preferred_element_type=F32)


@_lift
def _mm_nt(a, b):
    return lax.dot_general(a.astype(BF16), b.astype(BF16), (((1,), (1,)), ((), ())), preferred_element_type=F32)


def _mm_tn(a, b):
    return _mm(a.T, b)


@_lift
def _mm_f32(a, b):
    return jnp.dot(a, b, precision=lax.Precision.HIGHEST, preferred_element_type=F32)


def _sigmoid(x):
    return 0.5 * _tanh(0.5 * x) + 0.5


@_lift
def _softplus(x):
    return jnp.maximum(x, 0.0) + jnp.log1p(jnp.exp(-jnp.abs(x)))


def _proj_kernel(x_ref, w_ref, o_ref):
    o_ref[...] = jnp.dot(x_ref[...].astype(BF16), w_ref[...], preferred_element_type=F32)


def _input_projection(x2d, w_in_bf16):
    n_tok = x2d.shape[0]
    tile = math.gcd(n_tok, PROJ_TILE)
    assert tile % SUBLANES == 0
    return pl.pallas_call(
        _proj_kernel,
        grid=(n_tok // tile,),
        in_specs=[pl.BlockSpec((tile, D_MODEL), lambda i: (i, 0)),
                  pl.BlockSpec((D_MODEL, PROJ_COLS), lambda i: (0, 0))],
        out_specs=pl.BlockSpec((tile, PROJ_COLS), lambda i: (i, 0)),
        out_shape=jax.ShapeDtypeStruct((n_tok, PROJ_COLS), F32),
        compiler_params=pltpu.CompilerParams(dimension_semantics=("parallel",),
                                             vmem_limit_bytes=VMEM_LIMIT_BYTES),
        name="input_projection",
    )(x2d, w_in_bf16)


def _shift_rows(cur, prev, s, row_ids):
    return _where(row_ids < s, _roll(prev, s, axis=0), _roll(cur, s, axis=0))


@_lift
def _block_stack(x, n, width):
    grp = lax.broadcasted_iota(jnp.int32, x.shape, 1) // width
    if x.shape[0] % (2 * SUBLANES) == 0:
        x16 = x.astype(BF16)
        return jnp.concatenate([x16 * jnp.where(grp == i, 1.0, 0.0).astype(BF16) for i in range(n)], axis=0)
    return jnp.concatenate([jnp.where(grp == i, x, 0.0) for i in range(n)], axis=0)


def _inv_series(nils, c, n):
    row = lax.broadcasted_iota(jnp.int32, nils[0].shape, 0)
    col = lax.broadcasted_iota(jnp.int32, nils[0].shape, 1) % c
    eye = jnp.where(row == col, 1.0, 0.0)
    xs = [eye + nil for nil in nils]
    qs = [_mm(nil, _block_stack(nil, n, c)) for nil in nils]
    levels = int(math.log2(c))
    for lvl in range(1, levels):
        nxt_q, nxt_x = [], []
        for q, x in zip(qs, xs):
            bd = _block_stack(q, n, c)
            if lvl == levels - 1:
                nxt_x.append(x + _mm(x, bd))
            else:
                both = _mm(_cat([q, x], 0), bd)
                nxt_q.append(both[0:c])
                nxt_x.append(x + both[c:2 * c])
        qs, xs = nxt_q, nxt_x
    return xs


def _head_sum(x, width):
    pieces = []
    for j in range(x.shape[1] // LANES):
        xb = x[:, j * LANES:(j + 1) * LANES]
        if width == LANES:
            pieces.append(_bcast(_sum(xb, axis=-1, keepdims=True), xb.shape))
        else:
            lo = lax.broadcasted_iota(jnp.int32, xb.shape, 1) < width
            s0 = _sum(_where(lo, xb, 0.0), axis=-1, keepdims=True)
            s1 = _sum(_where(lo, 0.0, xb), axis=-1, keepdims=True)
            pieces.append(_where(lo, s0, s1))
    return _cat(pieces, 1)


def _gdn_head(a, h):
    return a[:, h * GDN_HEAD_DIM:(h + 1) * GDN_HEAD_DIM]


def _gdn_prepare(x, prev, c, t_last, masks, convw_ref, gvec_ref):
    row_p, col_p, grp_p, lower_p, strict_p, tri = masks
    masked = t_last < c

    def vmask(a):
        if not masked:
            return a
        return _where(lax.broadcasted_iota(jnp.int32, a.shape, 0) < t_last, a, 0.0)

    xq = x[:, 0:GDN_QKV]
    pq = prev[:, 0:GDN_QKV]
    rid = lax.broadcasted_iota(jnp.int32, (c, GDN_QKV), 0)
    conv = _shift_rows(xq, pq, 3, rid) * convw_ref[0:1, :]
    conv = conv + _shift_rows(xq, pq, 2, rid) * convw_ref[1:2, :]
    conv = conv + _shift_rows(xq, pq, 1, rid) * convw_ref[2:3, :]
    conv = conv + xq * convw_ref[3:4, :]
    conv = conv * _sigmoid(conv)

    ba = x[:, COL_BA:COL_BA + LANES]
    beta_blk = vmask(_sigmoid(ba))
    g_blk = vmask(-jnp.exp(gvec_ref[0:1, :]) * _softplus(ba + gvec_ref[1:2, :]))

    qn, kn, kb, vb = [], [], [], []
    for h in range(GDN_HEADS):
        qh = _gdn_head(conv[:, 0:GDN_WIDTH], h)
        kh = _gdn_head(conv[:, GDN_WIDTH:2 * GDN_WIDTH], h)
        vh = vmask(_gdn_head(conv[:, 2 * GDN_WIDTH:GDN_QKV], h))
        qh = qh * (_rsqrt(_sum(qh * qh, axis=-1, keepdims=True) + 1e-6) * (GDN_HEAD_DIM ** -0.5))
        kh = vmask(kh * _rsqrt(_sum(kh * kh, axis=-1, keepdims=True) + 1e-6))
        bh = beta_blk[:, h:h + 1]
        qn.append(qh), kn.append(kh), kb.append(kh * bh), vb.append(vh * bh)

    ma = _mm_nt(_cat([_cat(kb, 1), _cat(qn, 1)], 0),
                _block_stack(_cat(kn, 1), GDN_HEADS, GDN_HEAD_DIM))
    gexp = _where(grp_p == 0, g_blk[:, GDN_HEADS:GDN_HEADS + 1], 0.0)
    for h in range(1, GDN_HEADS):
        gexp = _where(grp_p == h, g_blk[:, GDN_HEADS + h:GDN_HEADS + h + 1], gexp)
    gcol = _mm_f32(tri, gexp)
    grow = _sum(_where(row_p == col_p, gcol, 0.0), axis=0, keepdims=True)
    decay = _exp(_where(lower_p, gcol - grow, -jnp.inf))
    m_p = _where(strict_p, ma[0:c] * decay, 0.0)
    attn = ma[c:2 * c] * decay
    egc, kdec, glast = [], [], []
    for h in range(GDN_HEADS):
        gch = gcol[:, h * c:h * c + 1]
        glh = gcol[c - 1:c, h * c:h * c + 1]
        egc.append(_exp(gch))
        kdec.append(kn[h] * _exp(glh - gch))
        glast.append(_exp(glh))
    rhs2 = _cat([_block_stack(_cat(vb, 1), GDN_HEADS, GDN_HEAD_DIM),
                 _block_stack(_cat([kb[h] * egc[h] for h in range(GDN_HEADS)], 1), GDN_HEADS, GDN_HEAD_DIM)], 1)
    qdec = [qn[h] * egc[h] for h in range(GDN_HEADS)]
    return dict(nil=-m_p, attn=attn, rhs2=rhs2, qdec=qdec, kdec=kdec, glast=glast)


def _gdn_finish(p, tinv, x, c, states, gvec_ref):
    uw = _mm(tinv, p["rhs2"])
    vnew, o1 = [], []
    for h in range(GDN_HEADS):
        ws = _mm(_cat([_gdn_head(uw[:, GDN_WIDTH:2 * GDN_WIDTH], h), p["qdec"][h]], 0), states[h])
        vnew.append(_gdn_head(uw[:, 0:GDN_WIDTH], h) - ws[0:c])
        o1.append(ws[c:2 * c])
    o2 = _mm(p["attn"], _block_stack(_cat(vnew, 1), GDN_HEADS, GDN_HEAD_DIM))
    out, new_states = [], []
    for h in range(GDN_HEADS):
        new_states.append(states[h] * p["glast"][h] + _mm_tn(p["kdec"][h], vnew[h]))
        oh = o1[h] + _gdn_head(o2, h)
        zh = _gdn_head(x[:, COL_Z:COL_Z + GDN_WIDTH], h)
        oh = oh * _rsqrt(_sum(oh * oh, axis=-1, keepdims=True) * (1.0 / GDN_HEAD_DIM) + 1e-6) * gvec_ref[2:3, :]
        out.append(oh * (zh * _sigmoid(zh)))
    return out, new_states


def _rwkv_prepare(x, prev, c, t_last, masks, mu_ref, rvec_ref, loraw_ref, g2_ref):
    row_p, col_p, grp_p, lower_p, strict_p, tri = masks
    masked = t_last < c
    gc4 = GROUP * c

    def vmask(a):
        if not masked:
            return a
        return _where(lax.broadcasted_iota(jnp.int32, a.shape, 0) < t_last, a, 0.0)

    rw = x[:, COL_RWKV:COL_BA]
    rid = lax.broadcasted_iota(jnp.int32, (c, RWKV_COLS), 0)
    prev_row = _shift_rows(rw, prev[:, COL_RWKV:COL_BA], 1, rid)
    rs = rw + (prev_row - rw) * mu_ref[...]
    o1_, o2_, o3_ = RWKV_WIDTH, 2 * RWKV_WIDTH, 3 * RWKV_WIDTH
    r = rs[:, 0:o1_]
    kr = rs[:, o1_:o2_]
    vr = rs[:, o2_:o3_]
    la = rs[:, o3_:o3_ + LANES]
    gl = rs[:, o3_ + LANES:o3_ + 2 * LANES]
    lane1 = lax.broadcasted_iota(jnp.int32, (c, LANES), 1)
    wa = _mm(_where(lane1 < DECAY_LORA, _tanh(la), la), loraw_ref[...])
    logw = vmask(-math.exp(-0.5) * _sigmoid(rvec_ref[0:1, :] + wa[:, 0:RWKV_WIDTH]))
    a = _sigmoid(rvec_ref[1:2, :] + wa[:, RWKV_WIDTH:2 * RWKV_WIDTH])
    gate = _mm(_sigmoid(gl), g2_ref[...])
    kkr = kr * rvec_ref[2:3, :]
    kk = vmask(kkr * _rsqrt(_head_sum(kkr * kkr, RWKV_HEAD_DIM) + 1e-6))
    kr2 = vmask(kr * (1.0 + (a - 1.0) * rvec_ref[3:4, :]))
    gcum = _mm_f32(tri, logw)
    e_pos = _exp(gcum)
    e_neg = _exp(-gcum)
    a_t = -kk * _exp(gcum - logw)
    b_t = kk * a * e_neg
    k_t = kr2 * e_neg
    r_t = r * e_pos
    groups = []
    for g in range(RWKV_GROUPS):
        sl = slice(g * GROUP_W, (g + 1) * GROUP_W)
        at_g, bt_g, kt_g, rt_g, v_g = a_t[:, sl], b_t[:, sl], k_t[:, sl], r_t[:, sl], vr[:, sl]
        aa = _mm_nt(_cat([at_g, rt_g], 0),
                    _cat([_block_stack(bt_g, GROUP, RWKV_HEAD_DIM), _block_stack(kt_g, GROUP, RWKV_HEAD_DIM)], 0))
        a_ab = _where(strict_p, aa[0:c, 0:gc4], 0.0)
        a_ak = _where(strict_p, aa[0:c, gc4:2 * gc4], 0.0)
        a_rb = _where(lower_p, aa[c:2 * c, 0:gc4], 0.0)
        a_rk = _where(lower_p, aa[c:2 * c, gc4:2 * gc4], 0.0)
        akv = _mm(a_ak, _block_stack(v_g, GROUP, RWKV_HEAD_DIM))
        groups.append(dict(nil=a_ab, at=at_g, bt=bt_g, kt=kt_g, rt=rt_g, v=v_g, akv=akv,
                           a_r=_cat([a_rb, a_rk], 1), e_last=e_pos[c - 1:c, sl]))
    bonus = _head_sum(r * kr2 * rvec_ref[4:5, :], RWKV_HEAD_DIM) * vr
    return dict(groups=groups, gate=gate, bonus=bonus)


def _rwkv_finish(p, tinvs, c, states, rvec_ref):
    bd_mask = (lax.broadcasted_iota(jnp.int32, (GROUP_W, GROUP_W), 0) // RWKV_HEAD_DIM
               == lax.broadcasted_iota(jnp.int32, (GROUP_W, GROUP_W), 1) // RWKV_HEAD_DIM)
    qs = p["groups"]
    rng = range(RWKV_GROUPS)
    wu0 = [_mm(tinvs[g], _cat([_block_stack(qs[g]["at"], GROUP, RWKV_HEAD_DIM),
                               _block_stack(qs[g]["akv"], GROUP, RWKV_HEAD_DIM)], 1)) for g in rng]
    wr = [_mm_nt(_cat([wu0[g][:, 0:GROUP_W], qs[g]["rt"]], 0), states[g]) for g in rng]
    u = [wr[g][0:c] + wu0[g][:, GROUP_W:2 * GROUP_W] for g in rng]
    o_g = [wr[g][c:2 * c] + _mm(qs[g]["a_r"], _cat([_block_stack(u[g], GROUP, RWKV_HEAD_DIM),
                                                    _block_stack(qs[g]["v"], GROUP, RWKV_HEAD_DIM)], 0)) for g in rng]
    upd = [_mm_tn(_cat([u[g], qs[g]["v"]], 0), _cat([qs[g]["bt"], qs[g]["kt"]], 0)) for g in rng]
    new_states = [(states[g] + _where(bd_mask, upd[g], 0.0)) * qs[g]["e_last"] for g in rng]
    o_r = _cat(o_g, 1)
    mean_o = _head_sum(o_r, RWKV_HEAD_DIM) * (1.0 / RWKV_HEAD_DIM)
    dev = o_r - mean_o
    var_o = _head_sum(dev * dev, RWKV_HEAD_DIM) * (1.0 / RWKV_HEAD_DIM)
    o_n = dev * _rsqrt(var_o + RWKV_GN_EPS) * rvec_ref[5:6, :] + rvec_ref[6:7, :]
    return (o_n + p["bonus"]) * p["gate"], new_states


def _mixer_kernel(c, rows_in, t_out, t_last, n_chunks, nb,
                  proj_ref, cinit_ref, sinit_ref, sg0_ref, sr0_ref,
                  convw_ref, gvec_ref, mu_ref, rvec_ref, loraw_ref, g2_ref,
                  mix_ref, convo_ref, shifto_ref, sgo_ref, sro_ref,
                  prev_scr, sg_scr, sr_scr):
    ci = pl.program_id(1)
    gc4 = GROUP * c
    hd = RWKV_HEAD_DIM

    @pl.when(ci == 0)
    def _():
        prev_scr[...] = jnp.zeros(prev_scr.shape, F32)
        for j in range(nb):
            prev_scr[j, c - SUBLANES:c, 0:GDN_QKV] = cinit_ref[j]
            prev_scr[j, c - 1:c, COL_RWKV:COL_BA] = sinit_ref[j]
            for g in range(RWKV_GROUPS):
                blocks = []
                for h in range(GROUP):
                    parts = [sr0_ref[j, g * GROUP + h]]
                    if h > 0:
                        parts.insert(0, jnp.zeros((hd, h * hd), F32))
                    if h < GROUP - 1:
                        parts.append(jnp.zeros((hd, (GROUP - 1 - h) * hd), F32))
                    blocks.append(jnp.concatenate(parts, axis=1))
                sr_scr[j, g] = jnp.concatenate(blocks, axis=0)
        sg_scr[...] = sg0_ref[...]

    row_p = lax.broadcasted_iota(jnp.int32, (c, gc4), 0)
    col_p = lax.broadcasted_iota(jnp.int32, (c, gc4), 1) % c
    grp_p = lax.broadcasted_iota(jnp.int32, (c, gc4), 1) // c
    tri = jnp.where(lax.broadcasted_iota(jnp.int32, (c, c), 1) <= lax.broadcasted_iota(jnp.int32, (c, c), 0),
                    1.0, 0.0)
    masks = (row_p, col_p, grp_p, col_p <= row_p, col_p < row_p, tri)

    def load_x(j):
        x = proj_ref[j]
        if rows_in < c:
            x = jnp.concatenate([x, jnp.zeros((c - rows_in, PROJ_COLS), F32)], axis=0)
        return x

    s_gdn = [_Seqs([sg_scr[j, h] for j in range(nb)]) for h in range(GDN_HEADS)]
    s_rwkv = [_Seqs([sr_scr[j, g] for j in range(nb)]) for g in range(RWKV_GROUPS)]
    x = _Seqs([load_x(j) for j in range(nb)])
    prev = _Seqs([prev_scr[j] for j in range(nb)])

    gdn = _gdn_prepare(x, prev, c, t_last, masks, convw_ref, gvec_ref)
    rwkv = _rwkv_prepare(x, prev, c, t_last, masks, mu_ref, rvec_ref, loraw_ref, g2_ref)
    tinvs = _inv_series([gdn["nil"]] + [q["nil"] for q in rwkv["groups"]], c, GROUP)
    o_gdn, new_g = _gdn_finish(gdn, tinvs[0], x, c, s_gdn, gvec_ref)
    o_rwkv, new_r = _rwkv_finish(rwkv, tinvs[1:], c, s_rwkv, rvec_ref)
    mix = _cat(o_gdn + [o_rwkv], 1)

    for j in range(nb):
        mix_ref[j] = mix.v[j][0:t_out].astype(mix_ref.dtype)
        prev_scr[j] = x.v[j][:, 0:COL_BA]
        for h in range(GDN_HEADS):
            sg_scr[j, h] = new_g[h].v[j]
        for g in range(RWKV_GROUPS):
            sr_scr[j, g] = new_r[g].v[j]

    @pl.when(ci == n_chunks - 1)
    def _():
        for j in range(nb):
            convo_ref[j] = x.v[j][t_last - (CONV_K - 1):t_last, 0:GDN_QKV]
            shifto_ref[j] = x.v[j][t_last - 1:t_last, COL_RWKV:COL_BA]
            for g in range(RWKV_GROUPS):
                s_g = new_r[g].v[j]
                for h in range(GROUP):
                    sro_ref[j, g * GROUP + h] = s_g[h * hd:(h + 1) * hd, h * hd:(h + 1) * hd]
            for h in range(GDN_HEADS):
                sgo_ref[j, h] = new_g[h].v[j]


def _mixer(proj3d, seq_len, conv_init8, shift_init, s_gdn, s_rwkv, mp, mix_dtype, nb):
    bsz, rows, _ = proj3d.shape
    nb = math.gcd(bsz, nb)
    if rows >= CHUNK:
        assert rows == seq_len and seq_len % CHUNK == 0
        c = CHUNK
        n_chunks, rows_in, t_out, t_last = seq_len // c, c, c, c
    else:
        c = rows
        n_chunks, rows_in, t_out, t_last = 1, rows, seq_len, seq_len
    assert t_last >= CONV_K - 1 and c % SUBLANES == 0
    const2 = lambda b, i: (0, 0)
    perb3 = lambda b, i: (b, 0, 0)
    perb4 = lambda b, i: (b, 0, 0, 0)
    kern = functools.partial(_mixer_kernel, c, rows_in, t_out, t_last, n_chunks, nb)
    state_g = (nb, GDN_HEADS, GDN_HEAD_DIM, GDN_HEAD_DIM)
    state_r = (nb, RWKV_HEADS, RWKV_HEAD_DIM, RWKV_HEAD_DIM)
    return pl.pallas_call(
        kern,
        grid=(bsz // nb, n_chunks),
        in_specs=[pl.BlockSpec((nb, rows_in, PROJ_COLS), lambda b, i: (b, i, 0)),
                  pl.BlockSpec((nb, SUBLANES, GDN_QKV), perb3),
                  pl.BlockSpec((nb, 1, RWKV_COLS), perb3),
                  pl.BlockSpec(state_g, perb4),
                  pl.BlockSpec(state_r, perb4),
                  pl.BlockSpec((CONV_K, GDN_QKV), const2),
                  pl.BlockSpec((SUBLANES, LANES), const2),
                  pl.BlockSpec((1, RWKV_COLS), const2),
                  pl.BlockSpec((SUBLANES, RWKV_WIDTH), const2),
                  pl.BlockSpec((LANES, 2 * RWKV_WIDTH), const2),
                  pl.BlockSpec((GATE_LORA, RWKV_WIDTH), const2)],
        out_specs=[pl.BlockSpec((nb, t_out, D_MODEL), lambda b, i: (b, i, 0)),
                   pl.BlockSpec((nb, CONV_K - 1, GDN_QKV), perb3),
                   pl.BlockSpec((nb, 1, RWKV_COLS), perb3),
                   pl.BlockSpec(state_g, perb4),
                   pl.BlockSpec(state_r, perb4)],
        out_shape=[jax.ShapeDtypeStruct((bsz, seq_len, D_MODEL), mix_dtype),
                   jax.ShapeDtypeStruct((bsz, CONV_K - 1, GDN_QKV), F32),
                   jax.ShapeDtypeStruct((bsz, 1, RWKV_COLS), F32),
                   jax.ShapeDtypeStruct((bsz,) + state_g[1:], F32),
                   jax.ShapeDtypeStruct((bsz,) + state_r[1:], F32)],
        scratch_shapes=[pltpu.VMEM((nb, c, COL_BA), F32),
                        pltpu.VMEM(state_g, F32),
                        pltpu.VMEM((nb, RWKV_GROUPS, GROUP_W, GROUP_W), F32)],
        compiler_params=pltpu.CompilerParams(dimension_semantics=("parallel", "arbitrary"),
                                             vmem_limit_bytes=VMEM_LIMIT_BYTES),
        name="sequence_mixers",
    )(proj3d, conv_init8, shift_init, s_gdn, s_rwkv,
      mp["conv_w"], mp["gvec"], mp["mu"], mp["rvec"], mp["lora_w"], mp["g2"])


ROW_TILES = D_MODEL // LANES


def _layernorm(x, g, b):
    mu = jnp.mean(x, axis=-1, keepdims=True)
    d = x - mu
    var = jnp.mean(d * d, axis=-1, keepdims=True)
    return d * lax.rsqrt(var + LN_EPS) * g + b


def _post_mixer_kernel(n_first, mixp_ref, mixs_ref, xp_ref, xs_ref, wo_ref, lnv_ref, wrt_ref, br_ref,
                       h_ref, ht_ref, idx_ref, rank_ref, gate_ref, cnt_ref, base_scr, hp_scr):
    i = pl.program_id(0)
    tt = h_ref.shape[0]

    @pl.when(i == 0)
    def _():
        base_scr[...] = jnp.zeros(base_scr.shape, F32)

    @pl.when(i < n_first)
    def _():
        hp_scr[...] = ALPHA * xp_ref[...] + jnp.dot(mixp_ref[...].astype(BF16), wo_ref[...],
                                                    preferred_element_type=F32)

    @pl.when(i >= n_first)
    def _():
        hp_scr[...] = ALPHA * xs_ref[...] + jnp.dot(mixs_ref[...].astype(BF16), wo_ref[...],
                                                    preferred_element_type=F32)

    h = _layernorm(hp_scr[...], lnv_ref[0:1, :], lnv_ref[1:2, :])
    h_ref[...] = h
    for j in range(ROW_TILES):
        ht_ref[:, j, :] = h[:, j * LANES:(j + 1) * LANES]
    lt = lax.dot_general(wrt_ref[...], h, (((1,), (1,)), ((), ())),
                         precision=lax.Precision.HIGHEST, preferred_element_type=F32) + br_ref[...]
    eid = lax.broadcasted_iota(jnp.int32, lt.shape, 0)
    lt = jnp.where(eid < N_EXPERTS, lt, -jnp.inf)
    upper = jnp.where(lax.broadcasted_iota(jnp.int32, (tt, tt), 0) < lax.broadcasted_iota(jnp.int32, (tt, tt), 1),
                      1.0, 0.0).astype(BF16)
    base = base_scr[...]
    vals, idxs, ranks = [], [], []
    for _ in range(TOP_K):
        m = jnp.max(lt, axis=0, keepdims=True)
        sel = jnp.min(jnp.where(lt == m, eid, LANES), axis=0, keepdims=True)
        onehot = eid == sel
        lt = jnp.where(onehot, -jnp.inf, lt)
        oh = jnp.where(onehot, 1.0, 0.0)
        before = jnp.dot(oh.astype(BF16), upper, preferred_element_type=F32)
        ranks.append(jnp.sum(oh * (base + before), axis=0, keepdims=True))
        base = base + jnp.sum(oh, axis=1, keepdims=True)
        vals.append(m)
        idxs.append(sel)
    base_scr[...] = base
    ex = [jnp.exp(v - vals[0]) for v in vals]
    den = ex[0] + ex[1] + ex[2] + ex[3]
    pad_i = jnp.zeros((SUBLANES - TOP_K, tt), jnp.int32)
    idx_ref[...] = jnp.concatenate(idxs + [pad_i], axis=0)
    rank_ref[...] = jnp.concatenate([rk.astype(jnp.int32) for rk in ranks] + [pad_i], axis=0)
    gates = jnp.concatenate([e / den for e in ex] + [jnp.zeros((SUBLANES - TOP_K, tt), F32)], axis=0)
    gate_ref[...] = gates.T

    @pl.when(i == pl.num_programs(0) - 1)
    def _():
        cnt_ref[...] = base[:, 0:LANES].astype(jnp.int32)


def _post_mixer(mix_p, mix_s, x_p, x_s, w_o_bf16, ln1, w_router_t, b_router_col):
    tt = TOK_TILE
    n_p, n_s = x_p.shape[0], x_s.shape[0]
    assert n_p % tt == 0 and n_s % tt == 0
    n1, n2 = n_p // tt, n_s // tt
    n_tok = n_p + n_s
    const2 = lambda i: (0, 0)
    first = lambda i: (jnp.minimum(i, n1 - 1), 0)
    second = lambda i: (jnp.maximum(i - n1, 0), 0)
    return pl.pallas_call(
        functools.partial(_post_mixer_kernel, n1),
        grid=(n1 + n2,),
        in_specs=[pl.BlockSpec((tt, D_MODEL), first),
                  pl.BlockSpec((tt, D_MODEL), second),
                  pl.BlockSpec((tt, D_MODEL), first),
                  pl.BlockSpec((tt, D_MODEL), second),
                  pl.BlockSpec((D_MODEL, D_MODEL), const2),
                  pl.BlockSpec((SUBLANES, D_MODEL), const2),
                  pl.BlockSpec((LANES, D_MODEL), const2),
                  pl.BlockSpec((LANES, 1), const2)],
        out_specs=[pl.BlockSpec((tt, D_MODEL), lambda i: (i, 0)),
                   pl.BlockSpec((tt, ROW_TILES, LANES), lambda i: (i, 0, 0)),
                   pl.BlockSpec((SUBLANES, tt), lambda i: (0, i)),
                   pl.BlockSpec((SUBLANES, tt), lambda i: (0, i)),
                   pl.BlockSpec((tt, SUBLANES), lambda i: (i, 0)),
                   pl.BlockSpec((LANES, LANES), const2)],
        out_shape=[jax.ShapeDtypeStruct((n_tok, D_MODEL), F32),
                   jax.ShapeDtypeStruct((n_tok, ROW_TILES, LANES), F32),
                   jax.ShapeDtypeStruct((SUBLANES, n_tok), jnp.int32),
                   jax.ShapeDtypeStruct((SUBLANES, n_tok), jnp.int32),
                   jax.ShapeDtypeStruct((n_tok, SUBLANES), F32),
                   jax.ShapeDtypeStruct((LANES, LANES), jnp.int32)],
        scratch_shapes=[pltpu.VMEM((LANES, tt), F32), pltpu.VMEM((tt, D_MODEL), F32)],
        compiler_params=pltpu.CompilerParams(dimension_semantics=("arbitrary",),
                                             vmem_limit_bytes=VMEM_LIMIT_BYTES),
        name="outproj_norm_router",
    )(mix_p, mix_s, x_p, x_s, w_o_bf16, ln1, w_router_t, b_router_col)


def _slot_kernel(pstart_ref, idx_ref, rank_ref, dest_ref):
    idx = idx_ref[...]
    dest = rank_ref[...]
    for e in range(N_EXPERTS):
        dest = dest + jnp.where(idx == e, pstart_ref[e], 0)
    dest_ref[...] = dest


def _slots(idx, rank, pstart):
    grid_spec = pltpu.PrefetchScalarGridSpec(
        num_scalar_prefetch=1,
        grid=(1,),
        in_specs=[pl.BlockSpec(idx.shape, lambda i, ps: (0, 0)),
                  pl.BlockSpec(idx.shape, lambda i, ps: (0, 0))],
        out_specs=pl.BlockSpec(idx.shape, lambda i, ps: (0, 0)),
    )
    return pl.pallas_call(
        _slot_kernel,
        grid_spec=grid_spec,
        out_shape=jax.ShapeDtypeStruct(idx.shape, jnp.int32),
        name="moe_slots",
    )(pstart, idx, rank)


def _dispatch_kernel(pend_ref, padded_ref, dest_ref, h_ref, xb_hbm, zero_scr, sem):
    i = pl.program_id(0)
    tt = h_ref.shape[0]
    tm = zero_scr.shape[0]

    @pl.when(i == 0)
    def _():
        zero_scr[...] = jnp.zeros(zero_scr.shape, F32)

        def tail_copy(e):
            start = pl.multiple_of(pend_ref[e] - tm, tm)
            return pltpu.make_async_copy(zero_scr, xb_hbm.at[pl.ds(start, tm)], sem)

        for e in range(N_EXPERTS):
            @pl.when(padded_ref[e] > 0)
            def _():
                tail_copy(e).start()
        for e in range(N_EXPERTS):
            @pl.when(padded_ref[e] > 0)
            def _():
                tail_copy(e).wait()

        def spare_copy(b):
            return pltpu.make_async_copy(zero_scr, xb_hbm.at[pl.ds(pl.multiple_of(b * tm, tm), tm)], sem)

        def spare_start(b, carry):
            spare_copy(b).start()
            return carry

        def spare_wait(b, carry):
            spare_copy(b).wait()
            return carry

        first_spare = pend_ref[N_EXPERTS - 1] // tm
        lax.fori_loop(first_spare, xb_hbm.shape[0] // tm, spare_start, 0)
        lax.fori_loop(first_spare, xb_hbm.shape[0] // tm, spare_wait, 0)

    def row_copy(t, k):
        return pltpu.make_async_copy(h_ref.at[t], xb_hbm.at[dest_ref[k, t]], sem)

    def issue(t, carry):
        for k in range(TOP_K):
            row_copy(t, k).start(priority=k % 2)
        return carry

    lax.fori_loop(0, tt, issue, 0, unroll=DMA_UNROLL)

    def drain(t, carry):
        for k in range(TOP_K):
            row_copy(t, k).wait()
        return carry

    lax.fori_loop(0, tt, drain, 0, unroll=DMA_UNROLL)


def _dispatch(h_tiles, dest, pend, padded, n_rows):
    n_tok = h_tiles.shape[0]
    tt = math.gcd(n_tok, DISPATCH_TILE)
    assert tt % LANES == 0
    grid_spec = pltpu.PrefetchScalarGridSpec(
        num_scalar_prefetch=2,
        grid=(n_tok // tt,),
        in_specs=[pl.BlockSpec((SUBLANES, tt), lambda i, pe, pa: (0, i), memory_space=pltpu.SMEM),
                  pl.BlockSpec((tt, ROW_TILES, LANES), lambda i, pe, pa: (i, 0, 0))],
        out_specs=pl.BlockSpec(memory_space=pl.ANY),
        scratch_shapes=[pltpu.VMEM((MOE_TILE, ROW_TILES, LANES), F32), pltpu.SemaphoreType.DMA(())],
    )
    return pl.pallas_call(
        _dispatch_kernel,
        grid_spec=grid_spec,
        out_shape=jax.ShapeDtypeStruct((n_rows, ROW_TILES, LANES), F32),
        compiler_params=pltpu.CompilerParams(dimension_semantics=("arbitrary",)),
        name="moe_dispatch",
    )(pend, padded, dest, h_tiles)


def _expert_kernel(be_ref, nused_ref, xb_hbm, wg_ref, wu_ref, wd_ref, bg_ref, bu_ref, bd_ref, y_ref,
                   wg16, wu16, wd16, xbuf, xsem):
    i = pl.program_id(0)
    tm = y_ref.shape[0]
    n_used = nused_ref[0]
    prev_e = be_ref[jnp.maximum(i - 1, 0)]
    fresh = jnp.logical_or(i == 0, be_ref[i] != prev_e)
    used = i < n_used

    def slab_copies(block, slot):
        r0 = pl.multiple_of(block * tm, tm)
        return [pltpu.make_async_copy(xb_hbm.at[pl.ds(r0, tm), j, :],
                                      xbuf.at[slot, :, pl.ds(j * LANES, LANES)], xsem.at[slot])
                for j in range(ROW_TILES)]

    @pl.when(jnp.logical_and(i == 0, used))
    def _():
        for cp in slab_copies(0, 0):
            cp.start()

    @pl.when(i + 1 < n_used)
    def _():
        for cp in slab_copies(i + 1, (i + 1) % 2):
            cp.start()

    @pl.when(jnp.logical_and(fresh, used))
    def _():
        wg16[...] = wg_ref[0].astype(BF16)
        wu16[...] = wu_ref[0].astype(BF16)
        wd16[...] = wd_ref[0].astype(BF16)

    @pl.when(used)
    def _():
        for cp in slab_copies(i, i % 2):
            cp.wait()
        x16 = xbuf[i % 2].astype(BF16)
        gt = jnp.dot(x16, wg16[...], preferred_element_type=F32) + bg_ref[0]
        up = jnp.dot(x16, wu16[...], preferred_element_type=F32) + bu_ref[0]
        gt = jnp.minimum(gt, SWIGLU_LIMIT)
        up = jnp.clip(up, -SWIGLU_LIMIT, SWIGLU_LIMIT)
        hid = (up + 1.0) * gt * _sigmoid(SWIGLU_ALPHA * gt)
        y_ref[...] = jnp.dot(hid.astype(BF16), wd16[...], preferred_element_type=F32) + bd_ref[0]

    @pl.when(jnp.logical_not(used))
    def _():
        y_ref[...] = jnp.zeros(y_ref.shape, F32)


def _experts(xb, block_expert, n_used, w_gate, b_gate, w_up, b_up, w_down, b_down):
    n_rows = xb.shape[0]
    tm = MOE_TILE
    n_blocks = n_rows // tm
    d_e = w_gate.shape[2]
    wspec = lambda shape: pl.BlockSpec(shape, lambda i, be, nu: (be[i], 0, 0))
    grid_spec = pltpu.PrefetchScalarGridSpec(
        num_scalar_prefetch=2,
        grid=(n_blocks,),
        in_specs=[pl.BlockSpec(memory_space=pl.ANY),
                  wspec((1, D_MODEL, d_e)), wspec((1, D_MODEL, d_e)), wspec((1, d_e, D_MODEL)),
                  wspec((1, 1, d_e)), wspec((1, 1, d_e)), wspec((1, 1, D_MODEL))],
        out_specs=pl.BlockSpec((tm, D_MODEL), lambda i, be, nu: (i, 0)),
        scratch_shapes=[pltpu.VMEM((D_MODEL, d_e), BF16), pltpu.VMEM((D_MODEL, d_e), BF16),
                        pltpu.VMEM((d_e, D_MODEL), BF16), pltpu.VMEM((2, tm, D_MODEL), F32),
                        pltpu.SemaphoreType.DMA((2,))],
    )
    return pl.pallas_call(
        _expert_kernel,
        grid_spec=grid_spec,
        out_shape=jax.ShapeDtypeStruct((n_rows, D_MODEL), F32),
        compiler_params=pltpu.CompilerParams(dimension_semantics=("arbitrary",),
                                             vmem_limit_bytes=VMEM_LIMIT_BYTES),
        name="moe_experts",
    )(block_expert, n_used, xb, w_gate, w_up, w_down,
      b_gate[:, None, :], b_up[:, None, :], b_down[:, None, :])


def _combine_kernel(n_first, dest_ref, h_ref, gate_ref, lnv_ref, yb_hbm, yp_ref, ys_ref, buf, sem):
    i = pl.program_id(0)
    tt = h_ref.shape[0]

    def row_copy(t, k):
        return pltpu.make_async_copy(yb_hbm.at[pl.ds(dest_ref[k, t], 1), :],
                                     buf.at[pl.ds(t, 1), pl.ds(k * D_MODEL, D_MODEL)], sem)

    def issue(t, carry):
        for k in range(TOP_K):
            row_copy(t, k).start(priority=k % 2)
        return carry

    lax.fori_loop(0, tt, issue, 0, unroll=DMA_UNROLL)

    def drain(t, carry):
        for k in range(TOP_K):
            row_copy(t, k).wait()
        return carry

    lax.fori_loop(0, tt, drain, 0, unroll=DMA_UNROLL)
    gates = gate_ref[...]
    f = buf[:, 0:D_MODEL] * gates[:, 0:1]
    for k in range(1, TOP_K):
        f = f + buf[:, k * D_MODEL:(k + 1) * D_MODEL] * gates[:, k:k + 1]
    y = _layernorm(ALPHA * h_ref[...] + f, lnv_ref[0:1, :], lnv_ref[1:2, :])

    @pl.when(i < n_first)
    def _():
        yp_ref[...] = y

    @pl.when(i >= n_first)
    def _():
        ys_ref[...] = y


def _combine(yb, h2d, dest, gates, ln2, n_p):
    n_tok = h2d.shape[0]
    n_s = n_tok - n_p
    tt = math.gcd(math.gcd(n_p, n_s), COMBINE_TILE)
    assert tt % LANES == 0
    n1, n2 = n_p // tt, n_s // tt
    return pl.pallas_call(
        functools.partial(_combine_kernel, n1),
        grid=(n1 + n2,),
        in_specs=[pl.BlockSpec((SUBLANES, tt), lambda i: (0, i), memory_space=pltpu.SMEM),
                  pl.BlockSpec((tt, D_MODEL), lambda i: (i, 0)),
                  pl.BlockSpec((tt, SUBLANES), lambda i: (i, 0)),
                  pl.BlockSpec((SUBLANES, D_MODEL), lambda i: (0, 0)),
                  pl.BlockSpec(memory_space=pl.ANY)],
        out_specs=[pl.BlockSpec((tt, D_MODEL), lambda i: (jnp.minimum(i, n1 - 1), 0)),
                   pl.BlockSpec((tt, D_MODEL), lambda i: (jnp.maximum(i - n1, 0), 0))],
        out_shape=[jax.ShapeDtypeStruct((n_p, D_MODEL), F32),
                   jax.ShapeDtypeStruct((n_s, D_MODEL), F32)],
        scratch_shapes=[pltpu.VMEM((tt, TOP_K * D_MODEL), F32), pltpu.SemaphoreType.DMA(())],
        compiler_params=pltpu.CompilerParams(dimension_semantics=("arbitrary",),
                                             vmem_limit_bytes=VMEM_LIMIT_BYTES),
        name="moe_combine_norm",
    )(dest, h2d, gates, ln2, yb)


def _pad_rows(v, rows):
    return jnp.concatenate([v, jnp.zeros((rows - v.shape[0],) + v.shape[1:], v.dtype)], axis=0)


def _mixer_params(conv_w, a_log, dt_bias, gdn_norm_w, mu_shift, w0, w2, a0, a2, g2, k_k, k_a, r_k, lnx_w, lnx_b):
    gvec = jnp.zeros((SUBLANES, LANES), F32)
    gvec = gvec.at[0, GDN_HEADS:2 * GDN_HEADS].set(a_log).at[1, GDN_HEADS:2 * GDN_HEADS].set(dt_bias)
    gvec = gvec.at[2, :].set(gdn_norm_w)
    rvec = _pad_rows(jnp.stack([w0, a0, k_k, k_a, r_k.reshape(-1), lnx_w, lnx_b]), SUBLANES)
    lora_w = jnp.zeros((LANES, 2 * RWKV_WIDTH), F32)
    lora_w = lora_w.at[0:DECAY_LORA, 0:RWKV_WIDTH].set(w2).at[DECAY_LORA:, RWKV_WIDTH:].set(a2)
    return dict(conv_w=conv_w, gvec=gvec, mu=mu_shift[None, :], rvec=rvec,
                lora_w=lora_w.astype(BF16), g2=g2.astype(BF16))


def _layer(x_prompt, x_sample, state_conv, state_shift, state_gdn, state_rwkv,
           w_in, mixer_params, w_o, ln1_g, ln1_b, w_router, b_router,
           w_gate, b_gate, w_up, b_up, w_down, b_down, ln2_g, ln2_b):
    bp, tp, d = x_prompt.shape
    bs, ts, _ = x_sample.shape
    n_p, n_s = bp * tp, bs * ts
    n_tok = n_p + n_s
    xp2d = x_prompt.reshape(n_p, d)
    xs2d = x_sample.reshape(n_s, d)

    in_cols = w_in.shape[1]
    off_ba = COL_Z + GDN_WIDTH
    w_in_r = jnp.concatenate([w_in[:, :off_ba], w_in[:, off_ba + 2 * GDN_HEADS:],
                              w_in[:, off_ba:off_ba + 2 * GDN_HEADS],
                              jnp.zeros((d, PROJ_COLS - in_cols), w_in.dtype)], axis=1).astype(BF16)

    rows_s = -(-ts // SUBLANES) * SUBLANES
    proj_p = _input_projection(xp2d, w_in_r).reshape(bp, tp, PROJ_COLS)
    proj_s = _input_projection(xs2d, w_in_r).reshape(bs, ts, PROJ_COLS)
    if rows_s != ts:
        proj_s = jnp.pad(proj_s, ((0, 0), (0, rows_s - ts), (0, 0)))
    zeros_p = (jnp.zeros((bp, SUBLANES, GDN_QKV), F32), jnp.zeros((bp, 1, RWKV_COLS), F32),
               jnp.zeros((bp, GDN_HEADS, GDN_HEAD_DIM, GDN_HEAD_DIM), F32),
               jnp.zeros((bp, RWKV_HEADS, RWKV_HEAD_DIM, RWKV_HEAD_DIM), F32))
    mix_p, conv_p, shift_p, gdn_p, rwkv_p = _mixer(proj_p, tp, *zeros_p, mixer_params, BF16,
                                                   SEQS_PER_STEP_LONG)
    conv8 = jnp.pad(state_conv, ((0, 0), (SUBLANES - (CONV_K - 1), 0), (0, 0)))
    mix_s, conv_s, shift_s, gdn_s, rwkv_s = _mixer(
        proj_s, ts, conv8, state_shift[:, None, :], state_gdn, state_rwkv, mixer_params, F32, SEQS_PER_STEP_SHORT)

    ln1 = _pad_rows(jnp.stack([ln1_g, ln1_b]), SUBLANES)
    ln2 = _pad_rows(jnp.stack([ln2_g, ln2_b]), SUBLANES)
    wrt = _pad_rows(w_router.T, LANES)
    brc = _pad_rows(b_router[:, None], LANES)
    h2d, h_tiles, idx, rank, gates, cnt = _post_mixer(mix_p.reshape(n_p, d), mix_s.reshape(n_s, d), xp2d, xs2d,
                                             w_o.astype(BF16), ln1, wrt, brc)

    counts = cnt[:N_EXPERTS, 0]
    padded = (((counts + MOE_TILE - 1) // MOE_TILE) * MOE_TILE).astype(jnp.int32)
    pend = jnp.cumsum(padded).astype(jnp.int32)
    pstart = pend - padded
    n_blocks = -(-(n_tok * TOP_K) // MOE_TILE) + N_EXPERTS
    n_rows = n_blocks * MOE_TILE
    block_expert = jnp.minimum(
        jnp.sum(pend[None, :] <= (jnp.arange(n_blocks) * MOE_TILE)[:, None], axis=1), N_EXPERTS - 1).astype(jnp.int32)
    n_used = pend[-1:] // MOE_TILE

    dest = _slots(idx, rank, pstart)
    xb = _dispatch(h_tiles, dest, pend, padded, n_rows)
    yb = _experts(xb, block_expert, n_used, w_gate, b_gate, w_up, b_up, w_down, b_down)
    y_p, y_s = _combine(yb, h2d, dest, gates, ln2, n_p)

    return (y_p.reshape(bp, tp, d), y_s.reshape(bs, ts, d), conv_p, shift_p[:, 0], gdn_p, rwkv_p,
            conv_s, shift_s[:, 0], gdn_s, rwkv_s)


def kernel(x_prompt, x_sample, state_conv, state_shift, state_gdn, state_rwkv, w_in, conv_w, a_log, dt_bias,
           gdn_norm_w, mu_shift, w0, w2, a0, a2, g2, k_k, k_a, r_k, lnx_w, lnx_b, w_o, ln1_g, ln1_b,
           w_router, b_router, w_gate, b_gate, w_up, b_up, w_down, b_down, ln2_g, ln2_b):
    mp = _mixer_params(conv_w, a_log, dt_bias, gdn_norm_w, mu_shift, w0, w2, a0, a2, g2, k_k, k_a, r_k,
                       lnx_w, lnx_b)
    return _layer(x_prompt, x_sample, state_conv, state_shift, state_gdn, state_rwkv,
                  w_in, mp, w_o, ln1_g, ln1_b, w_router, b_router,
                  w_gate, b_gate, w_up, b_up, w_down, b_down, ln2_g, ln2_b)
```

```python
import functools
import math

import jax
import jax.numpy as jnp
from jax import lax
from jax.experimental import pallas as pl
from jax.experimental.pallas import tpu as pltpu

F32 = jnp.float32
BF16 = jnp.bfloat16

D_MODEL = 1024
GDN_HEADS = 4
GDN_HEAD_DIM = 128
GDN_WIDTH = GDN_HEADS * GDN_HEAD_DIM
GDN_QKV = 3 * GDN_WIDTH
CONV_K = 4
RWKV_HEADS = 8
RWKV_HEAD_DIM = 64
RWKV_WIDTH = RWKV_HEADS * RWKV_HEAD_DIM
DECAY_LORA = 64
AAA_LORA = 64
GATE_LORA = 128
RWKV_COLS = 3 * RWKV_WIDTH + DECAY_LORA + AAA_LORA + GATE_LORA
RWKV_GN_EPS = RWKV_HEAD_DIM * 1e-5
N_EXPERTS = 32
TOP_K = 4
SWIGLU_LIMIT = 7.0
SWIGLU_ALPHA = 1.702
DEPTH = 1
ALPHA = (2.0 * DEPTH) ** 0.25
LN_EPS = 1e-5

LANES = 128
SUBLANES = 8
VMEM_LIMIT_BYTES = 56 * 1024 * 1024

COL_Z = GDN_QKV
COL_RWKV = COL_Z + GDN_WIDTH
COL_BA = COL_RWKV + RWKV_COLS
PROJ_COLS = COL_BA + LANES

CHUNK = 64
GROUP = 4
RWKV_GROUPS = RWKV_HEADS // GROUP
GROUP_W = GROUP * RWKV_HEAD_DIM
SEQS_PER_STEP_LONG = 4
SEQS_PER_STEP_SHORT = 8
PROJ_TILE = 512
TOK_TILE = 256
MOE_TILE = 512
DISPATCH_TILE = 512
COMBINE_TILE = 512
DMA_UNROLL = 8


class _Seqs:
    def __init__(self, vals):
        self.v = list(vals)

    def __getitem__(self, idx):
        return _Seqs([a[idx] for a in self.v])

    def __add__(self, o):
        return _lift(lambda a, b: a + b)(self, o)

    def __radd__(self, o):
        return _lift(lambda a, b: b + a)(self, o)

    def __sub__(self, o):
        return _lift(lambda a, b: a - b)(self, o)

    def __rsub__(self, o):
        return _lift(lambda a, b: b - a)(self, o)

    def __mul__(self, o):
        return _lift(lambda a, b: a * b)(self, o)

    def __rmul__(self, o):
        return _lift(lambda a, b: b * a)(self, o)

    def __neg__(self):
        return _Seqs([-a for a in self.v])

    @property
    def shape(self):
        return self.v[0].shape

    @property
    def T(self):
        return _Seqs([a.T for a in self.v])

    def astype(self, dt):
        return _Seqs([a.astype(dt) for a in self.v])


def _lift(f):
    def g(*args, **kw):
        n = next((len(a.v) for a in args if isinstance(a, _Seqs)), None)
        if n is None:
            return f(*args, **kw)
        return _Seqs([f(*[a.v[i] if isinstance(a, _Seqs) else a for a in args], **kw) for i in range(n)])
    return g


def _cat(parts, axis):
    n = next((len(a.v) for a in parts if isinstance(a, _Seqs)), None)
    if n is None:
        return jnp.concatenate(parts, axis=axis)
    return _Seqs([jnp.concatenate([a.v[i] if isinstance(a, _Seqs) else a for a in parts], axis=axis)
                  for i in range(n)])


_exp = _lift(jnp.exp)
_tanh = _lift(jnp.tanh)
_where = _lift(jnp.where)
_rsqrt = _lift(lax.rsqrt)
_sum = _lift(jnp.sum)
_roll = _lift(pltpu.roll)
_bcast = _lift(jnp.broadcast_to)


@_lift
def _mm(a, b):
    return jnp.dot(a.astype(BF16), b.astype(BF16), preferred_element_type=F32)


@_lift
def _mm_nt(a, b):
    return lax.dot_general(a.astype(BF16), b.astype(BF16), (((1,), (1,)), ((), ())), preferred_element_type=F32)


def _mm_tn(a, b):
    return _mm(a.T, b)


@_lift
def _mm_f32(a, b):
    return jnp.dot(a, b, precision=lax.Precision.HIGHEST, preferred_element_type=F32)


def _sigmoid(x):
    return 0.5 * _tanh(0.5 * x) + 0.5


@_lift
def _softplus(x):
    return jnp.maximum(x, 0.0) + jnp.log1p(jnp.exp(-jnp.abs(x)))


def _proj_kernel(x_ref, w_ref, o_ref):
    o_ref[...] = jnp.dot(x_ref[...].astype(BF16), w_ref[...], preferred_element_type=F32)


def _input_projection(x2d, w_in_bf16):
    n_tok = x2d.shape[0]
    tile = math.gcd(n_tok, PROJ_TILE)
    assert tile % SUBLANES == 0
    return pl.pallas_call(
        _proj_kernel,
        grid=(n_tok // tile,),
        in_specs=[pl.BlockSpec((tile, D_MODEL), lambda i: (i, 0)),
                  pl.BlockSpec((D_MODEL, PROJ_COLS), lambda i: (0, 0))],
        out_specs=pl.BlockSpec((tile, PROJ_COLS), lambda i: (i, 0)),
        out_shape=jax.ShapeDtypeStruct((n_tok, PROJ_COLS), F32),
        compiler_params=pltpu.CompilerParams(dimension_semantics=("parallel",),
                                             vmem_limit_bytes=VMEM_LIMIT_BYTES),
        name="input_projection",
    )(x2d, w_in_bf16)


def _shift_rows(cur, prev, s, row_ids):
    return _where(row_ids < s, _roll(prev, s, axis=0), _roll(cur, s, axis=0))


@_lift
def _block_stack(x, n, width):
    grp = lax.broadcasted_iota(jnp.int32, x.shape, 1) // width
    if x.shape[0] % (2 * SUBLANES) == 0:
        x16 = x.astype(BF16)
        return jnp.concatenate([x16 * jnp.where(grp == i, 1.0, 0.0).astype(BF16) for i in range(n)], axis=0)
    return jnp.concatenate([jnp.where(grp == i, x, 0.0) for i in range(n)], axis=0)


def _inv_series(nils, c, n):
    row = lax.broadcasted_iota(jnp.int32, nils[0].shape, 0)
    col = lax.broadcasted_iota(jnp.int32, nils[0].shape, 1) % c
    eye = jnp.where(row == col, 1.0, 0.0)
    xs = [eye + nil for nil in nils]
    qs = [_mm(nil, _block_stack(nil, n, c)) for nil in nils]
    levels = int(math.log2(c))
    for lvl in range(1, levels):
        nxt_q, nxt_x = [], []
        for q, x in zip(qs, xs):
            bd = _block_stack(q, n, c)
            if lvl == levels - 1:
                nxt_x.append(x + _mm(x, bd))
            else:
                both = _mm(_cat([q, x], 0), bd)
                nxt_q.append(both[0:c])
                nxt_x.append(x + both[c:2 * c])
        qs, xs = nxt_q, nxt_x
    return xs


def _head_sum(x, width):
    pieces = []
    for j in range(x.shape[1] // LANES):
        xb = x[:, j * LANES:(j + 1) * LANES]
        if width == LANES:
            pieces.append(_bcast(_sum(xb, axis=-1, keepdims=True), xb.shape))
        else:
            lo = lax.broadcasted_iota(jnp.int32, xb.shape, 1) < width
            s0 = _sum(_where(lo, xb, 0.0), axis=-1, keepdims=True)
            s1 = _sum(_where(lo, 0.0, xb), axis=-1, keepdims=True)
            pieces.append(_where(lo, s0, s1))
    return _cat(pieces, 1)


def _gdn_head(a, h):
    return a[:, h * GDN_HEAD_DIM:(h + 1) * GDN_HEAD_DIM]


def _gdn_prepare(x, prev, c, t_last, masks, convw_ref, gvec_ref):
    row_p, col_p, grp_p, lower_p, strict_p, tri = masks
    masked = t_last < c

    def vmask(a):
        if not masked:
            return a
        return _where(lax.broadcasted_iota(jnp.int32, a.shape, 0) < t_last, a, 0.0)

    xq = x[:, 0:GDN_QKV]
    pq = prev[:, 0:GDN_QKV]
    rid = lax.broadcasted_iota(jnp.int32, (c, GDN_QKV), 0)
    conv = _shift_rows(xq, pq, 3, rid) * convw_ref[0:1, :]
    conv = conv + _shift_rows(xq, pq, 2, rid) * convw_ref[1:2, :]
    conv = conv + _shift_rows(xq, pq, 1, rid) * convw_ref[2:3, :]
    conv = conv + xq * convw_ref[3:4, :]
    conv = conv * _sigmoid(conv)

    ba = x[:, COL_BA:COL_BA + LANES]
    beta_blk = vmask(_sigmoid(ba))
    g_blk = vmask(-jnp.exp(gvec_ref[0:1, :]) * _softplus(ba + gvec_ref[1:2, :]))

    qn, kn, kb, vb = [], [], [], []
    for h in range(GDN_HEADS):
        qh = _gdn_head(conv[:, 0:GDN_WIDTH], h)
        kh = _gdn_head(conv[:, GDN_WIDTH:2 * GDN_WIDTH], h)
        vh = vmask(_gdn_head(conv[:, 2 * GDN_WIDTH:GDN_QKV], h))
        qh = qh * (_rsqrt(_sum(qh * qh, axis=-1, keepdims=True) + 1e-6) * (GDN_HEAD_DIM ** -0.5))
        kh = vmask(kh * _rsqrt(_sum(kh * kh, axis=-1, keepdims=True) + 1e-6))
        bh = beta_blk[:, h:h + 1]
        qn.append(qh), kn.append(kh), kb.append(kh * bh), vb.append(vh * bh)

    ma = _mm_nt(_cat([_cat(kb, 1), _cat(qn, 1)], 0),
                _block_stack(_cat(kn, 1), GDN_HEADS, GDN_HEAD_DIM))
    gexp = _where(grp_p == 0, g_blk[:, GDN_HEADS:GDN_HEADS + 1], 0.0)
    for h in range(1, GDN_HEADS):
        gexp = _where(grp_p == h, g_blk[:, GDN_HEADS + h:GDN_HEADS + h + 1], gexp)
    gcol = _mm_f32(tri, gexp)
    grow = _sum(_where(row_p == col_p, gcol, 0.0), axis=0, keepdims=True)
    decay = _exp(_where(lower_p, gcol - grow, -jnp.inf))
    m_p = _where(strict_p, ma[0:c] * decay, 0.0)
    attn = ma[c:2 * c] * decay
    egc, kdec, glast = [], [], []
    for h in range(GDN_HEADS):
        gch = gcol[:, h * c:h * c + 1]
        glh = gcol[c - 1:c, h * c:h * c + 1]
        egc.append(_exp(gch))
        kdec.append(kn[h] * _exp(glh - gch))
        glast.append(_exp(glh))
    rhs2 = _cat([_block_stack(_cat(vb, 1), GDN_HEADS, GDN_HEAD_DIM),
                 _block_stack(_cat([kb[h] * egc[h] for h in range(GDN_HEADS)], 1), GDN_HEADS, GDN_HEAD_DIM)], 1)
    qdec = [qn[h] * egc[h] for h in range(GDN_HEADS)]
    return dict(nil=-m_p, attn=attn, rhs2=rhs2, qdec=qdec, kdec=kdec, glast=glast)


def _gdn_finish(p, tinv, x, c, states, gvec_ref):
    uw = _mm(tinv, p["rhs2"])
    vnew, o1 = [], []
    for h in range(GDN_HEADS):
        ws = _mm(_cat([_gdn_head(uw[:, GDN_WIDTH:2 * GDN_WIDTH], h), p["qdec"][h]], 0), states[h])
        vnew.append(_gdn_head(uw[:, 0:GDN_WIDTH], h) - ws[0:c])
        o1.append(ws[c:2 * c])
    o2 = _mm(p["attn"], _block_stack(_cat(vnew, 1), GDN_HEADS, GDN_HEAD_DIM))
    out, new_states = [], []
    for h in range(GDN_HEADS):
        new_states.append(states[h] * p["glast"][h] + _mm_tn(p["kdec"][h], vnew[h]))
        oh = o1[h] + _gdn_head(o2, h)
        zh = _gdn_head(x[:, COL_Z:COL_Z + GDN_WIDTH], h)
        oh = oh * _rsqrt(_sum(oh * oh, axis=-1, keepdims=True) * (1.0 / GDN_HEAD_DIM) + 1e-6) * gvec_ref[2:3, :]
        out.append(oh * (zh * _sigmoid(zh)))
    return out, new_states


def _rwkv_prepare(x, prev, c, t_last, masks, mu_ref, rvec_ref, loraw_ref, g2_ref):
    row_p, col_p, grp_p, lower_p, strict_p, tri = masks
    masked = t_last < c
    gc4 = GROUP * c

    def vmask(a):
        if not masked:
            return a
        return _where(lax.broadcasted_iota(jnp.int32, a.shape, 0) < t_last, a, 0.0)

    rw = x[:, COL_RWKV:COL_BA]
    rid = lax.broadcasted_iota(jnp.int32, (c, RWKV_COLS), 0)
    prev_row = _shift_rows(rw, prev[:, COL_RWKV:COL_BA], 1, rid)
    rs = rw + (prev_row - rw) * mu_ref[...]
    o1_, o2_, o3_ = RWKV_WIDTH, 2 * RWKV_WIDTH, 3 * RWKV_WIDTH
    r = rs[:, 0:o1_]
    kr = rs[:, o1_:o2_]
    vr = rs[:, o2_:o3_]
    la = rs[:, o3_:o3_ + LANES]
    gl = rs[:, o3_ + LANES:o3_ + 2 * LANES]
    lane1 = lax.broadcasted_iota(jnp.int32, (c, LANES), 1)
    wa = _mm(_where(lane1 < DECAY_LORA, _tanh(la), la), loraw_ref[...])
    logw = vmask(-math.exp(-0.5) * _sigmoid(rvec_ref[0:1, :] + wa[:, 0:RWKV_WIDTH]))
    a = _sigmoid(rvec_ref[1:2, :] + wa[:, RWKV_WIDTH:2 * RWKV_WIDTH])
    gate = _mm(_sigmoid(gl), g2_ref[...])
    kkr = kr * rvec_ref[2:3, :]
    kk = vmask(kkr * _rsqrt(_head_sum(kkr * kkr, RWKV_HEAD_DIM) + 1e-6))
    kr2 = vmask(kr * (1.0 + (a - 1.0) * rvec_ref[3:4, :]))
    gcum = _mm_f32(tri, logw)
    e_pos = _exp(gcum)
    e_neg = _exp(-gcum)
    a_t = -kk * _exp(gcum - logw)
    b_t = kk * a * e_neg
    k_t = kr2 * e_neg
    r_t = r * e_pos
    groups = []
    for g in range(RWKV_GROUPS):
        sl = slice(g * GROUP_W, (g + 1) * GROUP_W)
        at_g, bt_g, kt_g, rt_g, v_g = a_t[:, sl], b_t[:, sl], k_t[:, sl], r_t[:, sl], vr[:, sl]
        aa = _mm_nt(_cat([at_g, rt_g], 0),
                    _cat([_block_stack(bt_g, GROUP, RWKV_HEAD_DIM), _block_stack(kt_g, GROUP, RWKV_HEAD_DIM)], 0))
        a_ab = _where(strict_p, aa[0:c, 0:gc4], 0.0)
        a_ak = _where(strict_p, aa[0:c, gc4:2 * gc4], 0.0)
        a_rb = _where(lower_p, aa[c:2 * c, 0:gc4], 0.0)
        a_rk = _where(lower_p, aa[c:2 * c, gc4:2 * gc4], 0.0)
        akv = _mm(a_ak, _block_stack(v_g, GROUP, RWKV_HEAD_DIM))
        groups.append(dict(nil=a_ab, at=at_g, bt=bt_g, kt=kt_g, rt=rt_g, v=v_g, akv=akv,
                           a_r=_cat([a_rb, a_rk], 1), e_last=e_pos[c - 1:c, sl]))
    bonus = _head_sum(r * kr2 * rvec_ref[4:5, :], RWKV_HEAD_DIM) * vr
    return dict(groups=groups, gate=gate, bonus=bonus)


def _rwkv_finish(p, tinvs, c, states, rvec_ref):
    bd_mask = (lax.broadcasted_iota(jnp.int32, (GROUP_W, GROUP_W), 0) // RWKV_HEAD_DIM
               == lax.broadcasted_iota(jnp.int32, (GROUP_W, GROUP_W), 1) // RWKV_HEAD_DIM)
    qs = p["groups"]
    rng = range(RWKV_GROUPS)
    wu0 = [_mm(tinvs[g], _cat([_block_stack(qs[g]["at"], GROUP, RWKV_HEAD_DIM),
                               _block_stack(qs[g]["akv"], GROUP, RWKV_HEAD_DIM)], 1)) for g in rng]
    wr = [_mm_nt(_cat([wu0[g][:, 0:GROUP_W], qs[g]["rt"]], 0), states[g]) for g in rng]
    u = [wr[g][0:c] + wu0[g][:, GROUP_W:2 * GROUP_W] for g in rng]
    o_g = [wr[g][c:2 * c] + _mm(qs[g]["a_r"], _cat([_block_stack(u[g], GROUP, RWKV_HEAD_DIM),
                                                    _block_stack(qs[g]["v"], GROUP, RWKV_HEAD_DIM)], 0)) for g in rng]
    upd = [_mm_tn(_cat([u[g], qs[g]["v"]], 0), _cat([qs[g]["bt"], qs[g]["kt"]], 0)) for g in rng]
    new_states = [(states[g] + _where(bd_mask, upd[g], 0.0)) * qs[g]["e_last"] for g in rng]
    o_r = _cat(o_g, 1)
    mean_o = _head_sum(o_r, RWKV_HEAD_DIM) * (1.0 / RWKV_HEAD_DIM)
    dev = o_r - mean_o
    var_o = _head_sum(dev * dev, RWKV_HEAD_DIM) * (1.0 / RWKV_HEAD_DIM)
    o_n = dev * _rsqrt(var_o + RWKV_GN_EPS) * rvec_ref[5:6, :] + rvec_ref[6:7, :]
    return (o_n + p["bonus"]) * p["gate"], new_states


def _mixer_kernel(c, rows_in, t_out, t_last, n_chunks, nb,
                  proj_ref, cinit_ref, sinit_ref, sg0_ref, sr0_ref,
                  convw_ref, gvec_ref, mu_ref, rvec_ref, loraw_ref, g2_ref,
                  mix_ref, convo_ref, shifto_ref, sgo_ref, sro_ref,
                  prev_scr, sg_scr, sr_scr):
    ci = pl.program_id(1)
    gc4 = GROUP * c
    hd = RWKV_HEAD_DIM

    @pl.when(ci == 0)
    def _():
        prev_scr[...] = jnp.zeros(prev_scr.shape, F32)
        for j in range(nb):
            prev_scr[j, c - SUBLANES:c, 0:GDN_QKV] = cinit_ref[j]
            prev_scr[j, c - 1:c, COL_RWKV:COL_BA] = sinit_ref[j]
            for g in range(RWKV_GROUPS):
                blocks = []
                for h in range(GROUP):
                    parts = [sr0_ref[j, g * GROUP + h]]
                    if h > 0:
                        parts.insert(0, jnp.zeros((hd, h * hd), F32))
                    if h < GROUP - 1:
                        parts.append(jnp.zeros((hd, (GROUP - 1 - h) * hd), F32))
                    blocks.append(jnp.concatenate(parts, axis=1))
                sr_scr[j, g] = jnp.concatenate(blocks, axis=0)
        sg_scr[...] = sg0_ref[...]

    row_p = lax.broadcasted_iota(jnp.int32, (c, gc4), 0)
    col_p = lax.broadcasted_iota(jnp.int32, (c, gc4), 1) % c
    grp_p = lax.broadcasted_iota(jnp.int32, (c, gc4), 1) // c
    tri = jnp.where(lax.broadcasted_iota(jnp.int32, (c, c), 1) <= lax.broadcasted_iota(jnp.int32, (c, c), 0),
                    1.0, 0.0)
    masks = (row_p, col_p, grp_p, col_p <= row_p, col_p < row_p, tri)

    def load_x(j):
        x = proj_ref[j]
        if rows_in < c:
            x = jnp.concatenate([x, jnp.zeros((c - rows_in, PROJ_COLS), F32)], axis=0)
        return x

    s_gdn = [_Seqs([sg_scr[j, h] for j in range(nb)]) for h in range(GDN_HEADS)]
    s_rwkv = [_Seqs([sr_scr[j, g] for j in range(nb)]) for g in range(RWKV_GROUPS)]
    x = _Seqs([load_x(j) for j in range(nb)])
    prev = _Seqs([prev_scr[j] for j in range(nb)])

    gdn = _gdn_prepare(x, prev, c, t_last, masks, convw_ref, gvec_ref)
    rwkv = _rwkv_prepare(x, prev, c, t_last, masks, mu_ref, rvec_ref, loraw_ref, g2_ref)
    tinvs = _inv_series([gdn["nil"]] + [q["nil"] for q in rwkv["groups"]], c, GROUP)
    o_gdn, new_g = _gdn_finish(gdn, tinvs[0], x, c, s_gdn, gvec_ref)
    o_rwkv, new_r = _rwkv_finish(rwkv, tinvs[1:], c, s_rwkv, rvec_ref)
    mix = _cat(o_gdn + [o_rwkv], 1)

    for j in range(nb):
        mix_ref[j] = mix.v[j][0:t_out].astype(mix_ref.dtype)
        prev_scr[j] = x.v[j][:, 0:COL_BA]
        for h in range(GDN_HEADS):
            sg_scr[j, h] = new_g[h].v[j]
        for g in range(RWKV_GROUPS):
            sr_scr[j, g] = new_r[g].v[j]

    @pl.when(ci == n_chunks - 1)
    def _():
        for j in range(nb):
            convo_ref[j] = x.v[j][t_last - (CONV_K - 1):t_last, 0:GDN_QKV]
            shifto_ref[j] = x.v[j][t_last - 1:t_last, COL_RWKV:COL_BA]
            for g in range(RWKV_GROUPS):
                s_g = new_r[g].v[j]
                for h in range(GROUP):
                    sro_ref[j, g * GROUP + h] = s_g[h * hd:(h + 1) * hd, h * hd:(h + 1) * hd]
            for h in range(GDN_HEADS):
                sgo_ref[j, h] = new_g[h].v[j]


def _mixer(proj3d, seq_len, conv_init8, shift_init, s_gdn, s_rwkv, mp, mix_dtype, nb):
    bsz, rows, _ = proj3d.shape
    nb = math.gcd(bsz, nb)
    if rows >= CHUNK:
        assert rows == seq_len and seq_len % CHUNK == 0
        c = CHUNK
        n_chunks, rows_in, t_out, t_last = seq_len // c, c, c, c
    else:
        c = rows
        n_chunks, rows_in, t_out, t_last = 1, rows, seq_len, seq_len
    assert t_last >= CONV_K - 1 and c % SUBLANES == 0
    const2 = lambda b, i: (0, 0)
    perb3 = lambda b, i: (b, 0, 0)
    perb4 = lambda b, i: (b, 0, 0, 0)
    kern = functools.partial(_mixer_kernel, c, rows_in, t_out, t_last, n_chunks, nb)
    state_g = (nb, GDN_HEADS, GDN_HEAD_DIM, GDN_HEAD_DIM)
    state_r = (nb, RWKV_HEADS, RWKV_HEAD_DIM, RWKV_HEAD_DIM)
    return pl.pallas_call(
        kern,
        grid=(bsz // nb, n_chunks),
        in_specs=[pl.BlockSpec((nb, rows_in, PROJ_COLS), lambda b, i: (b, i, 0)),
                  pl.BlockSpec((nb, SUBLANES, GDN_QKV), perb3),
                  pl.BlockSpec((nb, 1, RWKV_COLS), perb3),
                  pl.BlockSpec(state_g, perb4),
                  pl.BlockSpec(state_r, perb4),
                  pl.BlockSpec((CONV_K, GDN_QKV), const2),
                  pl.BlockSpec((SUBLANES, LANES), const2),
                  pl.BlockSpec((1, RWKV_COLS), const2),
                  pl.BlockSpec((SUBLANES, RWKV_WIDTH), const2),
                  pl.BlockSpec((LANES, 2 * RWKV_WIDTH), const2),
                  pl.BlockSpec((GATE_LORA, RWKV_WIDTH), const2)],
        out_specs=[pl.BlockSpec((nb, t_out, D_MODEL), lambda b, i: (b, i, 0)),
                   pl.BlockSpec((nb, CONV_K - 1, GDN_QKV), perb3),
                   pl.BlockSpec((nb, 1, RWKV_COLS), perb3),
                   pl.BlockSpec(state_g, perb4),
                   pl.BlockSpec(state_r, perb4)],
        out_shape=[jax.ShapeDtypeStruct((bsz, seq_len, D_MODEL), mix_dtype),
                   jax.ShapeDtypeStruct((bsz, CONV_K - 1, GDN_QKV), F32),
                   jax.ShapeDtypeStruct((bsz, 1, RWKV_COLS), F32),
                   jax.ShapeDtypeStruct((bsz,) + state_g[1:], F32),
                   jax.ShapeDtypeStruct((bsz,) + state_r[1:], F32)],
        scratch_shapes=[pltpu.VMEM((nb, c, COL_BA), F32),
                        pltpu.VMEM(state_g, F32),
                        pltpu.VMEM((nb, RWKV_GROUPS, GROUP_W, GROUP_W), F32)],
        compiler_params=pltpu.CompilerParams(dimension_semantics=("parallel", "arbitrary"),
                                             vmem_limit_bytes=VMEM_LIMIT_BYTES),
        name="sequence_mixers",
    )(proj3d, conv_init8, shift_init, s_gdn, s_rwkv,
      mp["conv_w"], mp["gvec"], mp["mu"], mp["rvec"], mp["lora_w"], mp["g2"])


ROW_TILES = D_MODEL // LANES


def _layernorm(x, g, b):
    mu = jnp.mean(x, axis=-1, keepdims=True)
    d = x - mu
    var = jnp.mean(d * d, axis=-1, keepdims=True)
    return d * lax.rsqrt(var + LN_EPS) * g + b


def _post_mixer_kernel(n_first, mixp_ref, mixs_ref, xp_ref, xs_ref, wo_ref, lnv_ref, wrt_ref, br_ref,
                       h_ref, ht_ref, idx_ref, rank_ref, gate_ref, cnt_ref, base_scr, hp_scr):
    i = pl.program_id(0)
    tt = h_ref.shape[0]

    @pl.when(i == 0)
    def _():
        base_scr[...] = jnp.zeros(base_scr.shape, F32)

    @pl.when(i < n_first)
    def _():
        hp_scr[...] = ALPHA * xp_ref[...] + jnp.dot(mixp_ref[...].astype(BF16), wo_ref[...],
                                                    preferred_element_type=F32)

    @pl.when(i >= n_first)
    def _():
        hp_scr[...] = ALPHA * xs_ref[...] + jnp.dot(mixs_ref[...].astype(BF16), wo_ref[...],
                                                    preferred_element_type=F32)

    h = _layernorm(hp_scr[...], lnv_ref[0:1, :], lnv_ref[1:2, :])
    h_ref[...] = h
    for j in range(ROW_TILES):
        ht_ref[:, j, :] = h[:, j * LANES:(j + 1) * LANES]
    lt = lax.dot_general(wrt_ref[...], h, (((1,), (1,)), ((), ())),
                         precision=lax.Precision.HIGHEST, preferred_element_type=F32) + br_ref[...]
    eid = lax.broadcasted_iota(jnp.int32, lt.shape, 0)
    lt = jnp.where(eid < N_EXPERTS, lt, -jnp.inf)
    upper = jnp.where(lax.broadcasted_iota(jnp.int32, (tt, tt), 0) < lax.broadcasted_iota(jnp.int32, (tt, tt), 1),
                      1.0, 0.0).astype(BF16)
    base = base_scr[...]
    vals, idxs, ranks = [], [], []
    for _ in range(TOP_K):
        m = jnp.max(lt, axis=0, keepdims=True)
        sel = jnp.min(jnp.where(lt == m, eid, LANES), axis=0, keepdims=True)
        onehot = eid == sel
        lt = jnp.where(onehot, -jnp.inf, lt)
        oh = jnp.where(onehot, 1.0, 0.0)
        before = jnp.dot(oh.astype(BF16), upper, preferred_element_type=F32)
        ranks.append(jnp.sum(oh * (base + before), axis=0, keepdims=True))
        base = base + jnp.sum(oh, axis=1, keepdims=True)
        vals.append(m)
        idxs.append(sel)
    base_scr[...] = base
    ex = [jnp.exp(v - vals[0]) for v in vals]
    den = ex[0] + ex[1] + ex[2] + ex[3]
    pad_i = jnp.zeros((SUBLANES - TOP_K, tt), jnp.int32)
    idx_ref[...] = jnp.concatenate(idxs + [pad_i], axis=0)
    rank_ref[...] = jnp.concatenate([rk.astype(jnp.int32) for rk in ranks] + [pad_i], axis=0)
    gates = jnp.concatenate([e / den for e in ex] + [jnp.zeros((SUBLANES - TOP_K, tt), F32)], axis=0)
    gate_ref[...] = gates.T

    @pl.when(i == pl.num_programs(0) - 1)
    def _():
        cnt_ref[...] = base[:, 0:LANES].astype(jnp.int32)


def _post_mixer(mix_p, mix_s, x_p, x_s, w_o_bf16, ln1, w_router_t, b_router_col):
    tt = TOK_TILE
    n_p, n_s = x_p.shape[0], x_s.shape[0]
    assert n_p % tt == 0 and n_s % tt == 0
    n1, n2 = n_p // tt, n_s // tt
    n_tok = n_p + n_s
    const2 = lambda i: (0, 0)
    first = lambda i: (jnp.minimum(i, n1 - 1), 0)
    second = lambda i: (jnp.maximum(i - n1, 0), 0)
    return pl.pallas_call(
        functools.partial(_post_mixer_kernel, n1),
        grid=(n1 + n2,),
        in_specs=[pl.BlockSpec((tt, D_MODEL), first),
                  pl.BlockSpec((tt, D_MODEL), second),
                  pl.BlockSpec((tt, D_MODEL), first),
                  pl.BlockSpec((tt, D_MODEL), second),
                  pl.BlockSpec((D_MODEL, D_MODEL), const2),
                  pl.BlockSpec((SUBLANES, D_MODEL), const2),
                  pl.BlockSpec((LANES, D_MODEL), const2),
                  pl.BlockSpec((LANES, 1), const2)],
        out_specs=[pl.BlockSpec((tt, D_MODEL), lambda i: (i, 0)),
                   pl.BlockSpec((tt, ROW_TILES, LANES), lambda i: (i, 0, 0)),
                   pl.BlockSpec((SUBLANES, tt), lambda i: (0, i)),
                   pl.BlockSpec((SUBLANES, tt), lambda i: (0, i)),
                   pl.BlockSpec((tt, SUBLANES), lambda i: (i, 0)),
                   pl.BlockSpec((LANES, LANES), const2)],
        out_shape=[jax.ShapeDtypeStruct((n_tok, D_MODEL), F32),
                   jax.ShapeDtypeStruct((n_tok, ROW_TILES, LANES), F32),
                   jax.ShapeDtypeStruct((SUBLANES, n_tok), jnp.int32),
                   jax.ShapeDtypeStruct((SUBLANES, n_tok), jnp.int32),
                   jax.ShapeDtypeStruct((n_tok, SUBLANES), F32),
                   jax.ShapeDtypeStruct((LANES, LANES), jnp.int32)],
        scratch_shapes=[pltpu.VMEM((LANES, tt), F32), pltpu.VMEM((tt, D_MODEL), F32)],
        compiler_params=pltpu.CompilerParams(dimension_semantics=("arbitrary",),
                                             vmem_limit_bytes=VMEM_LIMIT_BYTES),
        name="outproj_norm_router",
    )(mix_p, mix_s, x_p, x_s, w_o_bf16, ln1, w_router_t, b_router_col)


def _slot_kernel(pstart_ref, idx_ref, rank_ref, dest_ref):
    idx = idx_ref[...]
    dest = rank_ref[...]
    for e in range(N_EXPERTS):
        dest = dest + jnp.where(idx == e, pstart_ref[e], 0)
    dest_ref[...] = dest


def _slots(idx, rank, pstart):
    grid_spec = pltpu.PrefetchScalarGridSpec(
        num_scalar_prefetch=1,
        grid=(1,),
        in_specs=[pl.BlockSpec(idx.shape, lambda i, ps: (0, 0)),
                  pl.BlockSpec(idx.shape, lambda i, ps: (0, 0))],
        out_specs=pl.BlockSpec(idx.shape, lambda i, ps: (0, 0)),
    )
    return pl.pallas_call(
        _slot_kernel,
        grid_spec=grid_spec,
        out_shape=jax.ShapeDtypeStruct(idx.shape, jnp.int32),
        name="moe_slots",
    )(pstart, idx, rank)


def _dispatch_kernel(pend_ref, padded_ref, dest_ref, h_ref, xb_hbm, zero_scr, sem):
    i = pl.program_id(0)
    tt = h_ref.shape[0]
    tm = zero_scr.shape[0]

    @pl.when(i == 0)
    def _():
        zero_scr[...] = jnp.zeros(zero_scr.shape, F32)

        def tail_copy(e):
            start = pl.multiple_of(pend_ref[e] - tm, tm)
            return pltpu.make_async_copy(zero_scr, xb_hbm.at[pl.ds(start, tm)], sem)

        for e in range(N_EXPERTS):
            @pl.when(padded_ref[e] > 0)
            def _():
                tail_copy(e).start()
        for e in range(N_EXPERTS):
            @pl.when(padded_ref[e] > 0)
            def _():
                tail_copy(e).wait()

        def spare_copy(b):
            return pltpu.make_async_copy(zero_scr, xb_hbm.at[pl.ds(pl.multiple_of(b * tm, tm), tm)], sem)

        def spare_start(b, carry):
            spare_copy(b).start()
            return carry

        def spare_wait(b, carry):
            spare_copy(b).wait()
            return carry

        first_spare = pend_ref[N_EXPERTS - 1] // tm
        lax.fori_loop(first_spare, xb_hbm.shape[0] // tm, spare_start, 0)
        lax.fori_loop(first_spare, xb_hbm.shape[0] // tm, spare_wait, 0)

    def row_copy(t, k):
        return pltpu.make_async_copy(h_ref.at[t], xb_hbm.at[dest_ref[k, t]], sem)

    def issue(t, carry):
        for k in range(TOP_K):
            row_copy(t, k).start(priority=k % 2)
        return carry

    lax.fori_loop(0, tt, issue, 0, unroll=DMA_UNROLL)

    def drain(t, carry):
        for k in range(TOP_K):
            row_copy(t, k).wait()
        return carry

    lax.fori_loop(0, tt, drain, 0, unroll=DMA_UNROLL)


def _dispatch(h_tiles, dest, pend, padded, n_rows):
    n_tok = h_tiles.shape[0]
    tt = math.gcd(n_tok, DISPATCH_TILE)
    assert tt % LANES == 0
    grid_spec = pltpu.PrefetchScalarGridSpec(
        num_scalar_prefetch=2,
        grid=(n_tok // tt,),
        in_specs=[pl.BlockSpec((SUBLANES, tt), lambda i, pe, pa: (0, i), memory_space=pltpu.SMEM),
                  pl.BlockSpec((tt, ROW_TILES, LANES), lambda i, pe, pa: (i, 0, 0))],
        out_specs=pl.BlockSpec(memory_space=pl.ANY),
        scratch_shapes=[pltpu.VMEM((MOE_TILE, ROW_TILES, LANES), F32), pltpu.SemaphoreType.DMA(())],
    )
    return pl.pallas_call(
        _dispatch_kernel,
        grid_spec=grid_spec,
        out_shape=jax.ShapeDtypeStruct((n_rows, ROW_TILES, LANES), F32),
        compiler_params=pltpu.CompilerParams(dimension_semantics=("arbitrary",)),
        name="moe_dispatch",
    )(pend, padded, dest, h_tiles)


def _expert_kernel(be_ref, nused_ref, wslot_ref, nexte_ref, hasnext_ref,
                   xb_hbm, wg_hbm, wu_hbm, wd_hbm, bg_ref, bu_ref, bd_ref, y_ref,
                   wg16, wu16, wd16, xbuf, wbuf, xsem, wsem):
    i = pl.program_id(0)
    tm = y_ref.shape[0]
    n_used = nused_ref[0]
    prev_e = be_ref[jnp.maximum(i - 1, 0)]
    fresh = jnp.logical_or(i == 0, be_ref[i] != prev_e)
    used = i < n_used

    def slab_copies(block, slot):
        r0 = pl.multiple_of(block * tm, tm)
        return [pltpu.make_async_copy(xb_hbm.at[pl.ds(r0, tm), j, :],
                                      xbuf.at[slot, :, pl.ds(j * LANES, LANES)], xsem.at[slot])
                for j in range(ROW_TILES)]

    def weight_copies(expert, slot):
        return [pltpu.make_async_copy(w_hbm.at[expert], wbuf.at[slot, m], wsem.at[slot])
                for m, w_hbm in enumerate((wg_hbm, wu_hbm, wd_hbm))]

    @pl.when(jnp.logical_and(i == 0, used))
    def _():
        for cp in slab_copies(0, 0):
            cp.start()
        for cp in weight_copies(be_ref[0], wslot_ref[0]):
            cp.start()

    @pl.when(i + 1 < n_used)
    def _():
        for cp in slab_copies(i + 1, (i + 1) % 2):
            cp.start()

    @pl.when(jnp.logical_and(fresh, used))
    def _():
        slot = wslot_ref[i]
        for cp in weight_copies(be_ref[i], slot):
            cp.wait()
        wg16[...] = wbuf[slot, 0].astype(BF16)
        wu16[...] = wbuf[slot, 1].astype(BF16)
        wd16[...] = wbuf[slot, 2].astype(BF16)

        @pl.when(hasnext_ref[i] > 0)
        def _():
            for cp in weight_copies(nexte_ref[i], 1 - slot):
                cp.start()

    @pl.when(used)
    def _():
        for cp in slab_copies(i, i % 2):
            cp.wait()
        x16 = xbuf[i % 2].astype(BF16)
        gt = jnp.dot(x16, wg16[...], preferred_element_type=F32) + bg_ref[0]
        up = jnp.dot(x16, wu16[...], preferred_element_type=F32) + bu_ref[0]
        gt = jnp.minimum(gt, SWIGLU_LIMIT)
        up = jnp.clip(up, -SWIGLU_LIMIT, SWIGLU_LIMIT)
        hid = (up + 1.0) * gt * _sigmoid(SWIGLU_ALPHA * gt)
        y_ref[...] = jnp.dot(hid.astype(BF16), wd16[...], preferred_element_type=F32) + bd_ref[0]

    @pl.when(jnp.logical_not(used))
    def _():
        y_ref[...] = jnp.zeros(y_ref.shape, F32)


def _experts(xb, block_expert, n_used, w_gate, b_gate, w_up, b_up, w_down, b_down):
    n_rows = xb.shape[0]
    tm = MOE_TILE
    n_blocks = n_rows // tm
    d_e = w_gate.shape[2]
    pos = jnp.arange(n_blocks, dtype=jnp.int32)
    used = pos < n_used[0]
    change = jnp.concatenate([jnp.ones((1,), bool), block_expert[1:] != block_expert[:-1]]) & used
    w_slot = ((jnp.cumsum(change.astype(jnp.int32)) - 1) % 2).astype(jnp.int32)
    change_at = jnp.where(change, pos, n_blocks)
    later = jnp.concatenate([change_at[1:], jnp.full((1,), n_blocks, jnp.int32)])
    next_change = jnp.flip(lax.cummin(jnp.flip(later)))
    has_next = (next_change < n_blocks).astype(jnp.int32)
    next_e = block_expert[jnp.minimum(next_change, n_blocks - 1)]
    bspec = lambda shape: pl.BlockSpec(shape, lambda i, be, nu, ws, ne, hn: (be[i], 0, 0))
    grid_spec = pltpu.PrefetchScalarGridSpec(
        num_scalar_prefetch=5,
        grid=(n_blocks,),
        in_specs=[pl.BlockSpec(memory_space=pl.ANY), pl.BlockSpec(memory_space=pl.ANY),
                  pl.BlockSpec(memory_space=pl.ANY), pl.BlockSpec(memory_space=pl.ANY),
                  bspec((1, 1, d_e)), bspec((1, 1, d_e)), bspec((1, 1, D_MODEL))],
        out_specs=pl.BlockSpec((tm, D_MODEL), lambda i, be, nu, ws, ne, hn: (i, 0)),
        scratch_shapes=[pltpu.VMEM((D_MODEL, d_e), BF16), pltpu.VMEM((D_MODEL, d_e), BF16),
                        pltpu.VMEM((d_e, D_MODEL), BF16), pltpu.VMEM((2, tm, D_MODEL), F32),
                        pltpu.VMEM((2, 3, D_MODEL, d_e), F32),
                        pltpu.SemaphoreType.DMA((2,)), pltpu.SemaphoreType.DMA((2,))],
    )
    return pl.pallas_call(
        _expert_kernel,
        grid_spec=grid_spec,
        out_shape=jax.ShapeDtypeStruct((n_rows, D_MODEL), F32),
        compiler_params=pltpu.CompilerParams(dimension_semantics=("arbitrary",),
                                             vmem_limit_bytes=VMEM_LIMIT_BYTES),
        name="moe_experts",
    )(block_expert, n_used, w_slot, next_e, has_next, xb, w_gate, w_up, w_down,
      b_gate[:, None, :], b_up[:, None, :], b_down[:, None, :])


def _combine_kernel(n_first, dest_ref, h_ref, gate_ref, lnv_ref, yb_hbm, yp_ref, ys_ref, buf, sem):
    i = pl.program_id(0)
    tt = h_ref.shape[0]

    def row_copy(t, k):
        return pltpu.make_async_copy(yb_hbm.at[pl.ds(dest_ref[k, t], 1), :],
                                     buf.at[pl.ds(t, 1), pl.ds(k * D_MODEL, D_MODEL)], sem)

    def issue(t, carry):
        for k in range(TOP_K):
            row_copy(t, k).start(priority=k % 2)
        return carry

    lax.fori_loop(0, tt, issue, 0, unroll=DMA_UNROLL)

    def drain(t, carry):
        for k in range(TOP_K):
            row_copy(t, k).wait()
        return carry

    lax.fori_loop(0, tt, drain, 0, unroll=DMA_UNROLL)
    gates = gate_ref[...]
    f = buf[:, 0:D_MODEL] * gates[:, 0:1]
    for k in range(1, TOP_K):
        f = f + buf[:, k * D_MODEL:(k + 1) * D_MODEL] * gates[:, k:k + 1]
    y = _layernorm(ALPHA * h_ref[...] + f, lnv_ref[0:1, :], lnv_ref[1:2, :])

    @pl.when(i < n_first)
    def _():
        yp_ref[...] = y

    @pl.when(i >= n_first)
    def _():
        ys_ref[...] = y


def _combine(yb, h2d, dest, gates, ln2, n_p):
    n_tok = h2d.shape[0]
    n_s = n_tok - n_p
    tt = math.gcd(math.gcd(n_p, n_s), COMBINE_TILE)
    assert tt % LANES == 0
    n1, n2 = n_p // tt, n_s // tt
    return pl.pallas_call(
        functools.partial(_combine_kernel, n1),
        grid=(n1 + n2,),
        in_specs=[pl.BlockSpec((SUBLANES, tt), lambda i: (0, i), memory_space=pltpu.SMEM),
                  pl.BlockSpec((tt, D_MODEL), lambda i: (i, 0)),
                  pl.BlockSpec((tt, SUBLANES), lambda i: (i, 0)),
                  pl.BlockSpec((SUBLANES, D_MODEL), lambda i: (0, 0)),
                  pl.BlockSpec(memory_space=pl.ANY)],
        out_specs=[pl.BlockSpec((tt, D_MODEL), lambda i: (jnp.minimum(i, n1 - 1), 0)),
                   pl.BlockSpec((tt, D_MODEL), lambda i: (jnp.maximum(i - n1, 0), 0))],
        out_shape=[jax.ShapeDtypeStruct((n_p, D_MODEL), F32),
                   jax.ShapeDtypeStruct((n_s, D_MODEL), F32)],
        scratch_shapes=[pltpu.VMEM((tt, TOP_K * D_MODEL), F32), pltpu.SemaphoreType.DMA(())],
        compiler_params=pltpu.CompilerParams(dimension_semantics=("arbitrary",),
                                             vmem_limit_bytes=VMEM_LIMIT_BYTES),
        name="moe_combine_norm",
    )(dest, h2d, gates, ln2, yb)


def _pad_rows(v, rows):
    return jnp.concatenate([v, jnp.zeros((rows - v.shape[0],) + v.shape[1:], v.dtype)], axis=0)


def _mixer_params(conv_w, a_log, dt_bias, gdn_norm_w, mu_shift, w0, w2, a0, a2, g2, k_k, k_a, r_k, lnx_w, lnx_b):
    gvec = jnp.zeros((SUBLANES, LANES), F32)
    gvec = gvec.at[0, GDN_HEADS:2 * GDN_HEADS].set(a_log).at[1, GDN_HEADS:2 * GDN_HEADS].set(dt_bias)
    gvec = gvec.at[2, :].set(gdn_norm_w)
    rvec = _pad_rows(jnp.stack([w0, a0, k_k, k_a, r_k.reshape(-1), lnx_w, lnx_b]), SUBLANES)
    lora_w = jnp.zeros((LANES, 2 * RWKV_WIDTH), F32)
    lora_w = lora_w.at[0:DECAY_LORA, 0:RWKV_WIDTH].set(w2).at[DECAY_LORA:, RWKV_WIDTH:].set(a2)
    return dict(conv_w=conv_w, gvec=gvec, mu=mu_shift[None, :], rvec=rvec,
                lora_w=lora_w.astype(BF16), g2=g2.astype(BF16))


def _layer(x_prompt, x_sample, state_conv, state_shift, state_gdn, state_rwkv,
           w_in, mixer_params, w_o, ln1_g, ln1_b, w_router, b_router,
           w_gate, b_gate, w_up, b_up, w_down, b_down, ln2_g, ln2_b):
    bp, tp, d = x_prompt.shape
    bs, ts, _ = x_sample.shape
    n_p, n_s = bp * tp, bs * ts
    n_tok = n_p + n_s
    xp2d = x_prompt.reshape(n_p, d)
    xs2d = x_sample.reshape(n_s, d)

    in_cols = w_in.shape[1]
    off_ba = COL_Z + GDN_WIDTH
    w_in_r = jnp.concatenate([w_in[:, :off_ba], w_in[:, off_ba + 2 * GDN_HEADS:],
                              w_in[:, off_ba:off_ba + 2 * GDN_HEADS],
                              jnp.zeros((d, PROJ_COLS - in_cols), w_in.dtype)], axis=1).astype(BF16)

    rows_s = -(-ts // SUBLANES) * SUBLANES
    proj_p = _input_projection(xp2d, w_in_r).reshape(bp, tp, PROJ_COLS)
    proj_s = _input_projection(xs2d, w_in_r).reshape(bs, ts, PROJ_COLS)
    if rows_s != ts:
        proj_s = jnp.pad(proj_s, ((0, 0), (0, rows_s - ts), (0, 0)))
    zeros_p = (jnp.zeros((bp, SUBLANES, GDN_QKV), F32), jnp.zeros((bp, 1, RWKV_COLS), F32),
               jnp.zeros((bp, GDN_HEADS, GDN_HEAD_DIM, GDN_HEAD_DIM), F32),
               jnp.zeros((bp, RWKV_HEADS, RWKV_HEAD_DIM, RWKV_HEAD_DIM), F32))
    mix_p, conv_p, shift_p, gdn_p, rwkv_p = _mixer(proj_p, tp, *zeros_p, mixer_params, BF16,
                                                   SEQS_PER_STEP_LONG)
    conv8 = jnp.pad(state_conv, ((0, 0), (SUBLANES - (CONV_K - 1), 0), (0, 0)))
    mix_s, conv_s, shift_s, gdn_s, rwkv_s = _mixer(
        proj_s, ts, conv8, state_shift[:, None, :], state_gdn, state_rwkv, mixer_params, F32, SEQS_PER_STEP_SHORT)

    ln1 = _pad_rows(jnp.stack([ln1_g, ln1_b]), SUBLANES)
    ln2 = _pad_rows(jnp.stack([ln2_g, ln2_b]), SUBLANES)
    wrt = _pad_rows(w_router.T, LANES)
    brc = _pad_rows(b_router[:, None], LANES)
    h2d, h_tiles, idx, rank, gates, cnt = _post_mixer(mix_p.reshape(n_p, d), mix_s.reshape(n_s, d), xp2d, xs2d,
                                             w_o.astype(BF16), ln1, wrt, brc)

    counts = cnt[:N_EXPERTS, 0]
    padded = (((counts + MOE_TILE - 1) // MOE_TILE) * MOE_TILE).astype(jnp.int32)
    pend = jnp.cumsum(padded).astype(jnp.int32)
    pstart = pend - padded
    n_blocks = -(-(n_tok * TOP_K) // MOE_TILE) + N_EXPERTS
    n_rows = n_blocks * MOE_TILE
    block_expert = jnp.minimum(
        jnp.sum(pend[None, :] <= (jnp.arange(n_blocks) * MOE_TILE)[:, None], axis=1), N_EXPERTS - 1).astype(jnp.int32)
    n_used = pend[-1:] // MOE_TILE

    dest = _slots(idx, rank, pstart)
    xb = _dispatch(h_tiles, dest, pend, padded, n_rows)
    yb = _experts(xb, block_expert, n_used, w_gate, b_gate, w_up, b_up, w_down, b_down)
    y_p, y_s = _combine(yb, h2d, dest, gates, ln2, n_p)

    return (y_p.reshape(bp, tp, d), y_s.reshape(bs, ts, d), conv_p, shift_p[:, 0], gdn_p, rwkv_p,
            conv_s, shift_s[:, 0], gdn_s, rwkv_s)


def kernel(x_prompt, x_sample, state_conv, state_shift, state_gdn, state_rwkv, w_in, conv_w, a_log, dt_bias,
           gdn_norm_w, mu_shift, w0, w2, a0, a2, g2, k_k, k_a, r_k, lnx_w, lnx_b, w_o, ln1_g, ln1_b,
           w_router, b_router, w_gate, b_gate, w_up, b_up, w_down, b_down, ln2_g, ln2_b):
    mp = _mixer_params(conv_w, a_log, dt_bias, gdn_norm_w, mu_shift, w0, w2, a0, a2, g2, k_k, k_a, r_k,
                       lnx_w, lnx_b)
    return _layer(x_prompt, x_sample, state_conv, state_shift, state_gdn, state_rwkv,
                  w_in, mp, w_o, ln1_g, ln1_b, w_router, b_router,
                  w_gate, b_gate, w_up, b_up, w_down, b_down, ln2_g, ln2_b)
```
